```python
import jax, jax.numpy as jnp
from jax import lax
import numpy as np

D_MODEL = 1024
BATCH = 2
SEQ = 8192
DEPTH = 2
DEC_BATCH = 128
DEC_SEQ = 8
PAST_LEN = 8192
PAGE_SIZE = 128

H_A = 4
DH_A = D_MODEL // 8
W_A = H_A * DH_A
CHUNK_A = 64
DH = 64
ROT_DIM = DH // 4
ROPE_THETA = 500000.0
BLK = 128
DIL_WINDOWS = (128, 512, 2048)
DIL_RATES = (1, 4, 16)
N_DIL = 3
H_B = D_MODEL // (2 * DH)
W_B = H_B * DH
H_C = D_MODEL // DH
KV_C = 2
WIN_C = 128
D_FF = 4 * D_MODEL
N_AB = (DEPTH + 1) // 2
N_C = DEPTH // 2
ALPHA = (2.0 * DEPTH) ** 0.25
BETA = (8.0 * DEPTH) ** -0.25
LN_EPS = 1e-5
IN_AB = 4 * W_A + 2 * H_A + 3 * N_DIL * W_B
OUT_AB = W_A + W_B
IN_C = (H_C + 2 * KV_C) * DH
OUT_C = H_C * DH

kernel_name = 'hybrid_mlstm_dilated_swa_decoder_step'


def layer_norm(x, g, b):
    xf = x.astype(jnp.float32)
    mu = jnp.mean(xf, axis=-1, keepdims=True)
    var = jnp.mean(jnp.square(xf - mu), axis=-1, keepdims=True)
    y = (xf - mu) * lax.rsqrt(var + LN_EPS) * g.astype(jnp.float32) + b.astype(jnp.float32)
    return y.astype(x.dtype)


def sq_relu_mlp(x, w_up, w_down):
    return jnp.square(jax.nn.relu(x @ w_up)) @ w_down


def rope(x, pos):
    half = ROT_DIM // 2
    inv = ROPE_THETA ** (-jnp.arange(half, dtype=jnp.float32) / half)
    ang = pos.astype(jnp.float32)[:, None] * inv[None, :]
    cos = jnp.cos(ang)[:, None, :]
    sin = jnp.sin(ang)[:, None, :]
    x1 = x[..., :half].astype(jnp.float32)
    x2 = x[..., half:ROT_DIM].astype(jnp.float32)
    rot = jnp.concatenate([x1 * cos - x2 * sin, x2 * cos + x1 * sin], axis=-1).astype(x.dtype)
    return jnp.concatenate([rot, x[..., ROT_DIM:]], axis=-1)


def banded_attention(q, k, v, max_dist, sinks=None):
    n, length, hq, dh = q.shape
    hk = k.shape[2]
    grp = hq // hk
    lp = -(-length // BLK) * BLK
    pad = ((0, 0), (0, lp - length), (0, 0), (0, 0))
    q, k, v = jnp.pad(q, pad), jnp.pad(k, pad), jnp.pad(v, pad)
    nb = lp // BLK
    qb = q.reshape(n, nb, BLK, hk, grp, dh)
    kb = k.reshape(n, nb, BLK, hk, dh)
    vb = v.reshape(n, nb, BLK, hk, dh)
    shift = ((0, 0), (1, 0), (0, 0), (0, 0), (0, 0))
    kk = jnp.concatenate([jnp.pad(kb, shift)[:, :-1], kb], axis=2)
    vv = jnp.concatenate([jnp.pad(vb, shift)[:, :-1], vb], axis=2)
    s = jnp.einsum('nbqhgd,nbkhd->nbhgqk', qb, kk, preferred_element_type=jnp.float32) * (dh ** -0.5)
    qi = jnp.arange(BLK)[:, None] + BLK
    ki = jnp.arange(2 * BLK)[None, :]
    dist = qi - ki
    band = (dist >= 0) & (dist <= max_dist)
    mask = band[None] & ((jnp.arange(nb)[:, None, None] > 0) | (ki[None] >= BLK))
    s = jnp.where(mask[None, :, None, None], s, -jnp.inf)
    m = jnp.max(s, axis=-1)
    if sinks is not None:
        sk = sinks.astype(jnp.float32).reshape(hk, grp)[None, None, :, :, None]
        m = jnp.maximum(m, sk)
    p = jnp.exp(s - m[..., None])
    den = jnp.sum(p, axis=-1)
    if sinks is not None:
        den = den + jnp.exp(sk - m)
    o = jnp.einsum('nbhgqk,nbkhd->nbqhgd', p, vv.astype(jnp.float32))
    o = o / jnp.transpose(den, (0, 1, 4, 2, 3))[..., None]
    o = o.reshape(n, lp, hq, dh)[:, :length].astype(q.dtype)
    lse = jnp.transpose(m + jnp.log(den), (0, 1, 4, 2, 3)).reshape(n, lp, hq)[:, :length]
    return o, lse


def dilated_prompt(q, k, v, rate, span):
    bn, s_len, h, dh = q.shape
    sp = -(-s_len // rate) * rate
    length = sp // rate

    def to_strided(t):
        t = jnp.pad(t, ((0, 0), (0, sp - s_len), (0, 0), (0, 0)))
        return jnp.transpose(t.reshape(bn, length, rate, h, dh), (0, 2, 1, 3, 4)).reshape(bn * rate, length, h, dh)

    o, lse = banded_attention(to_strided(q), to_strided(k), to_strided(v), span)
    o = jnp.transpose(o.reshape(bn, rate, length, h, dh), (0, 2, 1, 3, 4)).reshape(bn, sp, h, dh)[:, :s_len]
    lse = jnp.transpose(lse.reshape(bn, rate, length, h), (0, 2, 1, 3)).reshape(bn, sp, h)[:, :s_len]
    return o, lse


def dilated_sample(q, k_all, v_all, rate, span, n_past):
    t_len = q.shape[1]
    idx = n_past + jnp.arange(t_len)[:, None] - rate * jnp.arange(span + 1)[None, :]
    valid = idx >= 0
    idx = jnp.maximum(idx, 0)
    kg = k_all[:, idx]
    vg = v_all[:, idx]
    s = jnp.einsum('bthd,btmhd->bthm', q, kg, preferred_element_type=jnp.float32) * (q.shape[-1] ** -0.5)
    s = jnp.where(valid[None, :, None, :], s, -jnp.inf)
    m = jnp.max(s, axis=-1)
    p = jnp.exp(s - m[..., None])
    den = jnp.sum(p, axis=-1)
    o = jnp.einsum('bthm,btmhd->bthd', p, vg.astype(jnp.float32)) / den[..., None]
    return o.astype(q.dtype), m + jnp.log(den)


def swa_sample(q, k_all, v_all, sinks, n_past):
    bd, t_len, hq, dh = q.shape
    hk = k_all.shape[2]
    grp = hq // hk
    qg = q.reshape(bd, t_len, hk, grp, dh)
    s = jnp.einsum('bthgd,bshd->bhgts', qg, k_all, preferred_element_type=jnp.float32) * (dh ** -0.5)
    dist = (n_past + jnp.arange(t_len))[:, None] - jnp.arange(n_past + t_len)[None, :]
    mask = (dist >= 0) & (dist < WIN_C)
    s = jnp.where(mask, s, -jnp.inf)
    sk = sinks.astype(jnp.float32).reshape(hk, grp)[None, :, :, None]
    m = jnp.maximum(jnp.max(s, axis=-1), sk)
    p = jnp.exp(s - m[..., None])
    den = jnp.sum(p, axis=-1) + jnp.exp(sk - m)
    o = jnp.einsum('bhgts,bshd->bthgd', p, v_all.astype(jnp.float32))
    o = o / jnp.transpose(den, (0, 3, 1, 2))[..., None]
    return o.reshape(bd, t_len, hq, dh).astype(q.dtype)


def mlstm_chunk(carry, xs):
    c_prev, n_prev, m_prev = carry
    q, k, v, logi, logf = xs
    q = q.astype(jnp.float32)
    k = k.astype(jnp.float32)
    v = v.astype(jnp.float32)
    length = q.shape[2]
    b = jnp.cumsum(logf, axis=-1)
    a = b + m_prev[..., None]
    causal = jnp.tril(jnp.ones((length, length), dtype=bool))
    dmat = jnp.where(causal, b[..., :, None] - b[..., None, :] + logi[..., None, :], -jnp.inf)
    mt = jnp.maximum(a, jnp.max(dmat, axis=-1))
    wts = jnp.exp(dmat - mt[..., None])
    inter = jnp.exp(a - mt)
    sqk = jnp.einsum('bhtd,bhsd->bhts', q, k) * wts
    num = inter[..., None] * jnp.einsum('bhtd,bhde->bhte', q, c_prev) + jnp.einsum('bhts,bhse->bhte', sqk, v)
    nq = inter * jnp.einsum('bhtd,bhd->bht', q, n_prev) + jnp.sum(sqk, axis=-1)
    h = num / jnp.maximum(jnp.abs(nq), jnp.exp(-mt))[..., None]
    w_last = wts[..., -1, :]
    c_new = inter[..., -1, None, None] * c_prev + jnp.einsum('bhs,bhsd,bhse->bhde', w_last, k, v)
    n_new = inter[..., -1, None] * n_prev + jnp.einsum('bhs,bhsd->bhd', w_last, k)
    return (c_new, n_new, mt[..., -1]), h


def mlstm_prompt(q, k, v, logi, logf):
    bn, s_len, h, d = q.shape
    nc = s_len // CHUNK_A

    def split_chunks(t):
        t = t.reshape((bn, nc, CHUNK_A) + t.shape[2:])
        return jnp.moveaxis(jnp.moveaxis(t, 1, 0), 3, 2)

    init = (jnp.zeros((bn, h, d, d), jnp.float32), jnp.zeros((bn, h, d), jnp.float32), jnp.zeros((bn, h), jnp.float32))
    xs = (split_chunks(q), split_chunks(k), split_chunks(v), split_chunks(logi), split_chunks(logf))
    final, hs = lax.scan(mlstm_chunk, init, xs)
    hs = jnp.transpose(hs, (1, 0, 3, 2, 4)).reshape(bn, s_len, h, d)
    return hs, final


def mlstm_sample(q, k, v, logi, logf, carry):
    xs = (jnp.transpose(q, (0, 2, 1, 3)), jnp.transpose(k, (0, 2, 1, 3)), jnp.transpose(v, (0, 2, 1, 3)),
          jnp.transpose(logi, (0, 2, 1)), jnp.transpose(logf, (0, 2, 1)))
    final, hs = mlstm_chunk(carry, xs)
    return jnp.transpose(hs, (0, 2, 1, 3)), final


def ab_project(x, w_in, b_gate, pos):
    bn, s_len, _ = x.shape
    z = x @ w_in
    cuts = [W_A, 2 * W_A, 3 * W_A, 4 * W_A, 4 * W_A + 2 * H_A,
            4 * W_A + 2 * H_A + N_DIL * W_B, 4 * W_A + 2 * H_A + 2 * N_DIL * W_B]
    qa, ka, va, oa, gates, qb, kb, vb = jnp.split(z, cuts, axis=-1)
    qa = qa.reshape(bn, s_len, H_A, DH_A)
    ka = ka.reshape(bn, s_len, H_A, DH_A) * (DH_A ** -0.5)
    va = va.reshape(bn, s_len, H_A, DH_A)
    gates = (gates + b_gate).astype(jnp.float32)
    logi = gates[..., :H_A]
    logf = jax.nn.log_sigmoid(gates[..., H_A:])
    qb = rope(qb.reshape(bn, s_len, N_DIL * H_B, DH), pos).reshape(bn, s_len, N_DIL, H_B, DH)
    kb = rope(kb.reshape(bn, s_len, N_DIL * H_B, DH), pos).reshape(bn, s_len, N_DIL, H_B, DH)
    vb = vb.reshape(bn, s_len, N_DIL, H_B, DH)
    return qa, ka, va, oa, logi, logf, qb, kb, vb


def ab_output(h_a, oa, outs, lses, w_out):
    bn, s_len = oa.shape[:2]
    ga = h_a.astype(oa.dtype).reshape(bn, s_len, W_A) * jax.nn.sigmoid(oa)
    wts = jax.nn.softmax(jnp.stack(lses, axis=0), axis=0)
    ob = jnp.sum(wts[..., None].astype(oa.dtype) * jnp.stack(outs, axis=0), axis=0)
    return jnp.concatenate([ga, ob.reshape(bn, s_len, W_B)], axis=-1) @ w_out


def c_project(x, w_in, pos):
    bn, s_len, _ = x.shape
    z = x @ w_in
    q, k, v = jnp.split(z, [H_C * DH, (H_C + KV_C) * DH], axis=-1)
    q = rope(q.reshape(bn, s_len, H_C, DH), pos)
    k = rope(k.reshape(bn, s_len, KV_C, DH), pos)
    v = v.reshape(bn, s_len, KV_C, DH)
    return q, k, v


def setup_inputs(seed: int = 0) -> dict:
    key = jax.random.key(seed)
    ks = jax.random.split(key, 24)

    def nrm(k, shape, scale):
        return jax.random.normal(k, shape, jnp.float32) * scale

    b_gate = jnp.concatenate([nrm(ks[10], (N_AB, H_A), 0.1),
                              jnp.linspace(3.0, 6.0, H_A, dtype=jnp.float32)[None, :] + nrm(ks[11], (N_AB, H_A), 0.1)], axis=-1)
    return {
        'x_prompt': nrm(ks[0], (BATCH, SEQ, D_MODEL), 1.0),
        'x_sample': nrm(ks[1], (DEC_BATCH, DEC_SEQ, D_MODEL), 1.0),
        'state_mlstm_C': nrm(ks[2], (N_AB, DEC_BATCH, H_A, DH_A, DH_A), DH_A ** -0.5),
        'state_mlstm_n': nrm(ks[3], (N_AB, DEC_BATCH, H_A, DH_A), 0.5),
        'state_mlstm_m': nrm(ks[4], (N_AB, DEC_BATCH, H_A), 1.0),
        'cache_dil1_kv': nrm(ks[5], (N_AB, DEC_BATCH, min(DIL_WINDOWS[0], PAST_LEN), 2, H_B, DH), 1.0),
        'cache_dil2_kv': nrm(ks[6], (N_AB, DEC_BATCH, min(DIL_WINDOWS[1], PAST_LEN), 2, H_B, DH), 1.0),
        'cache_dil3_kv': nrm(ks[7], (N_AB, DEC_BATCH, min(DIL_WINDOWS[2], PAST_LEN), 2, H_B, DH), 1.0),
        'cache_swa_kv': nrm(ks[8], (N_C, DEC_BATCH, min(WIN_C, PAST_LEN), 2, KV_C, DH), 1.0),
        'w_in_ab': nrm(ks[9], (N_AB, D_MODEL, IN_AB), D_MODEL ** -0.5),
        'b_gate_ab': b_gate,
        'w_out_ab': nrm(ks[12], (N_AB, OUT_AB, D_MODEL), BETA * OUT_AB ** -0.5),
        'w_in_c': nrm(ks[13], (N_C, D_MODEL, IN_C), D_MODEL ** -0.5),
        'sinks_c': nrm(ks[14], (N_C, H_C), 0.5),
        'w_out_c': nrm(ks[15], (N_C, OUT_C, D_MODEL), BETA * OUT_C ** -0.5),
        'ln1_g': 1.0 + nrm(ks[16], (DEPTH, D_MODEL), 0.05),
        'ln1_b': nrm(ks[17], (DEPTH, D_MODEL), 0.02),
        'ln2_g': 1.0 + nrm(ks[18], (DEPTH, D_MODEL), 0.05),
        'ln2_b': nrm(ks[19], (DEPTH, D_MODEL), 0.02),
        'w_up': nrm(ks[20], (DEPTH, D_MODEL, D_FF), D_MODEL ** -0.5),
        'w_down': nrm(ks[21], (DEPTH, D_FF, D_MODEL), BETA * D_FF ** -0.5),
    }


def reference(x_prompt, x_sample, state_mlstm_C, state_mlstm_n, state_mlstm_m, cache_dil1_kv, cache_dil2_kv,
              cache_dil3_kv, cache_swa_kv, w_in_ab, b_gate_ab, w_out_ab, w_in_c, sinks_c, w_out_c,
              ln1_g, ln1_b, ln2_g, ln2_b, w_up, w_down):
    s_len = x_prompt.shape[1]
    t_len = x_sample.shape[1]
    pos_p = jnp.arange(s_len, dtype=jnp.int32)
    pos_s = PAST_LEN + jnp.arange(t_len, dtype=jnp.int32)
    dil_caches = (cache_dil1_kv, cache_dil2_kv, cache_dil3_kv)
    mc_p, mc_s, mn_p, mn_s, mm_p, mm_s = [], [], [], [], [], []
    dil_p = [[] for _ in range(N_DIL)]
    dil_s = [[] for _ in range(N_DIL)]
    swa_p, swa_s = [], []
    xp, xs = x_prompt, x_sample
    for layer in range(DEPTH):
        j = layer // 2
        if layer % 2 == 0:
            qa, ka, va, oa, logi, logf, qb, kb, vb = ab_project(xp, w_in_ab[j], b_gate_ab[j], pos_p)
            h_a, (c_new, n_new, m_new) = mlstm_prompt(qa, ka, va, logi, logf)
            mc_p.append(c_new)
            mn_p.append(n_new)
            mm_p.append(m_new)
            outs, lses = [], []
            for g in range(N_DIL):
                o, l = dilated_prompt(qb[:, :, g], kb[:, :, g], vb[:, :, g], DIL_RATES[g], DIL_WINDOWS[g] // DIL_RATES[g])
                outs.append(o)
                lses.append(l)
                keep = min(DIL_WINDOWS[g], s_len)
                dil_p[g].append(jnp.stack([kb[:, s_len - keep:, g], vb[:, s_len - keep:, g]], axis=2))
            mix_p = ab_output(h_a, oa, outs, lses, w_out_ab[j])
            qa, ka, va, oa, logi, logf, qb, kb, vb = ab_project(xs, w_in_ab[j], b_gate_ab[j], pos_s)
            carry = (state_mlstm_C[j].astype(jnp.float32), state_mlstm_n[j].astype(jnp.float32),
                     state_mlstm_m[j].astype(jnp.float32))
            h_a, (c_new, n_new, m_new) = mlstm_sample(qa, ka, va, logi, logf, carry)
            mc_s.append(c_new)
            mn_s.append(n_new)
            mm_s.append(m_new)
            outs, lses = [], []
            for g in range(N_DIL):
                buf = dil_caches[g][j].astype(kb.dtype)
                n_past = buf.shape[1]
                k_all = jnp.concatenate([buf[:, :, 0], kb[:, :, g]], axis=1)
                v_all = jnp.concatenate([buf[:, :, 1], vb[:, :, g]], axis=1)
                o, l = dilated_sample(qb[:, :, g], k_all, v_all, DIL_RATES[g], DIL_WINDOWS[g] // DIL_RATES[g], n_past)
                outs.append(o)
                lses.append(l)
                keep = min(DIL_WINDOWS[g], n_past + t_len)
                dil_s[g].append(jnp.stack([k_all[:, n_past + t_len - keep:], v_all[:, n_past + t_len - keep:]], axis=2))
            mix_s = ab_output(h_a, oa, outs, lses, w_out_ab[j])
        else:
            q, k, v = c_project(xp, w_in_c[j], pos_p)
            o, _ = banded_attention(q, k, v, WIN_C - 1, sinks_c[j])
            keep = min(WIN_C, s_len)
            swa_p.append(jnp.stack([k[:, s_len - keep:], v[:, s_len - keep:]], axis=2))
            mix_p = o.reshape(xp.shape[0], s_len, OUT_C) @ w_out_c[j]
            q, k, v = c_project(xs, w_in_c[j], pos_s)
            buf = cache_swa_kv[j].astype(k.dtype)
            n_past = buf.shape[1]
            k_all = jnp.concatenate([buf[:, :, 0], k], axis=1)
            v_all = jnp.concatenate([buf[:, :, 1], v], axis=1)
            o = swa_sample(q, k_all, v_all, sinks_c[j], n_past)
            keep = min(WIN_C, n_past + t_len)
            swa_s.append(jnp.stack([k_all[:, n_past + t_len - keep:], v_all[:, n_past + t_len - keep:]], axis=2))
            mix_s = o.reshape(xs.shape[0], t_len, OUT_C) @ w_out_c[j]
        xp = layer_norm(ALPHA * xp + mix_p, ln1_g[layer], ln1_b[layer])
        xp = layer_norm(ALPHA * xp + sq_relu_mlp(xp, w_up[layer], w_down[layer]), ln2_g[layer], ln2_b[layer])
        xs = layer_norm(ALPHA * xs + mix_s, ln1_g[layer], ln1_b[layer])
        xs = layer_norm(ALPHA * xs + sq_relu_mlp(xs, w_up[layer], w_down[layer]), ln2_g[layer], ln2_b[layer])
    mlstm_C_p = jnp.stack(mc_p)
    mlstm_C_s = jnp.stack(mc_s)
    mlstm_n_p = jnp.stack(mn_p)
    mlstm_n_s = jnp.stack(mn_s)
    mlstm_m_p = jnp.stack(mm_p)
    mlstm_m_s = jnp.stack(mm_s)
    dil1_kv_p = jnp.stack(dil_p[0])
    dil1_kv_s = jnp.stack(dil_s[0])
    dil2_kv_p = jnp.stack(dil_p[1])
    dil2_kv_s = jnp.stack(dil_s[1])
    dil3_kv_p = jnp.stack(dil_p[2])
    dil3_kv_s = jnp.stack(dil_s[2])
    swa_kv_p = jnp.stack(swa_p)
    swa_kv_s = jnp.stack(swa_s)
    return (xp, xs, mlstm_C_p, mlstm_C_s, mlstm_n_p, mlstm_n_s, mlstm_m_p, mlstm_m_s,
            dil1_kv_p, dil1_kv_s, dil2_kv_p, dil2_kv_s, dil3_kv_p, dil3_kv_s, swa_kv_p, swa_kv_s)
```

```python
import functools

import jax
import jax.numpy as jnp
from jax import lax
from jax.experimental import pallas as pl
from jax.experimental.pallas import tpu as pltpu

LANES = 128
SUBLANES = 8
VMEM_LIMIT = 56 * 1024 * 1024

D_MODEL = 1024
DH = 64
ROT_DIM = DH // 4
ROPE_THETA = 500000.0
PAST_LEN = 8192
BLK = 128
H_A = 4
DH_A = 128
W_A = H_A * DH_A
N_DIL = 3
DIL_WINDOWS = (128, 512, 2048)
DIL_RATES = (1, 4, 16)
H_B = 8
W_B = H_B * DH
H_C = 16
KV_C = 2
WIN_C = 128
D_FF = 4 * D_MODEL
DEPTH = 2
ALPHA = (2.0 * DEPTH) ** 0.25
LN_EPS = 1e-5
N_AB = 4 * W_A + 3 * N_DIL * W_B
N_C = (H_C + 2 * KV_C) * DH
NEG_INF = float("-inf")

_F32 = jnp.float32
_BF16 = jnp.bfloat16


def _params(sem):
    return pltpu.CompilerParams(dimension_semantics=sem, vmem_limit_bytes=VMEM_LIMIT)


def _nt_dot(a, b):
    return lax.dot_general(a, b, (((1,), (1,)), ((), ())), preferred_element_type=_F32)


def _tn_dot(a, b):
    return lax.dot_general(a, b, (((0,), (0,)), ((), ())), preferred_element_type=_F32)


def _dot(a, b):
    return jnp.dot(a, b, preferred_element_type=_F32)


def _log_sigmoid(x):
    return -(jnp.maximum(-x, 0.0) + jnp.log1p(jnp.exp(-jnp.abs(x))))


def _layer_norm(y, g, b):
    mu = jnp.mean(y, axis=-1, keepdims=True)
    yc = y - mu
    var = jnp.mean(yc * yc, axis=-1, keepdims=True)
    return yc * lax.rsqrt(var + LN_EPS) * g + b


def _rope_tables(pos):
    half = ROT_DIM // 2
    inv = ROPE_THETA ** (-jnp.arange(half, dtype=_F32) / half)
    ang = pos.astype(_F32)[:, None] * inv[None, :]
    cos = jnp.cos(ang)
    sin = jnp.sin(ang)
    n = pos.shape[0]
    ones = jnp.ones((n, DH - ROT_DIM), _F32)
    zeros = jnp.zeros((n, DH - ROT_DIM), _F32)
    zh = jnp.zeros((n, half), _F32)
    cos_h = jnp.concatenate([cos, cos, ones], axis=1)
    sinm_h = jnp.concatenate([-sin, zh, zeros], axis=1)
    sinp_h = jnp.concatenate([zh, sin, zeros], axis=1)
    tile2 = lambda t: jnp.concatenate([t, t], axis=1)
    return tile2(cos_h), tile2(sinm_h), tile2(sinp_h)


def _proj_kernel(flags_ref, x_ref, w_ref, cos_ref, sinm_ref, sinp_ref, *rest, tn, with_gates):
    if with_gates:
        wg_ref, o_ref, g_ref, xb_ref = rest
    else:
        o_ref, xb_ref = rest
    j = pl.program_id(1)

    @pl.when(j == 0)
    def _():
        xb_ref[...] = x_ref[...].astype(_BF16)
        if with_gates:
            g_ref[...] = _dot(xb_ref[...], wg_ref[...])

    acc = _dot(xb_ref[...], w_ref[...])
    half = ROT_DIM // 2
    for cc in range(tn // LANES):
        sub = acc[:, cc * LANES:(cc + 1) * LANES]
        flag = flags_ref[j * (tn // LANES) + cc]

        @pl.when(flag == 1)
        def _():
            rot = (sub * cos_ref[...]
                   + pltpu.roll(sub, LANES - half, axis=1) * sinm_ref[...]
                   + pltpu.roll(sub, half, axis=1) * sinp_ref[...])
            o_ref[:, cc * LANES:(cc + 1) * LANES] = rot

        @pl.when(flag == 0)
        def _():
            o_ref[:, cc * LANES:(cc + 1) * LANES] = sub


def _project(x, w, tables, flags, wg, tm, tn):
    t, d = x.shape
    n = w.shape[1]
    cos, sinm, sinp = tables
    tab_blocks = cos.shape[0] // tm
    with_gates = wg is not None
    row_tab = pl.BlockSpec((tm, LANES), lambda i, j, f: (i % tab_blocks, 0))
    in_specs = [
        pl.BlockSpec((tm, d), lambda i, j, f: (i, 0)),
        pl.BlockSpec((d, tn), lambda i, j, f: (0, j)),
        row_tab, row_tab, row_tab,
    ]
    out_shape = [jax.ShapeDtypeStruct((t, n), _F32)]
    out_specs = [pl.BlockSpec((tm, tn), lambda i, j, f: (i, j))]
    args = [x, w, cos, sinm, sinp]
    if with_gates:
        in_specs.append(pl.BlockSpec((d, LANES), lambda i, j, f: (0, 0)))
        out_shape.append(jax.ShapeDtypeStruct((t, LANES), _F32))
        out_specs.append(pl.BlockSpec((tm, LANES), lambda i, j, f: (i, 0)))
        args.append(wg)
    res = pl.pallas_call(
        functools.partial(_proj_kernel, tn=tn, with_gates=with_gates),
        grid_spec=pltpu.PrefetchScalarGridSpec(
            num_scalar_prefetch=1, grid=(t // tm, n // tn),
            in_specs=in_specs, out_specs=out_specs,
            scratch_shapes=[pltpu.VMEM((tm, d), _BF16)]),
        out_shape=out_shape,
        compiler_params=_params(("parallel", "arbitrary")),
        name="proj_gates" if with_gates else "proj",
    )(flags, *args)
    return res if with_gates else res[0]


def _mlstm_kernel(q_ref, k_ref, v_ref, oa_ref, g_ref, bias_ref, c0_ref, n0_ref, m0_ref,
                  ga_ref, c_ref, n_ref, m_ref, *, chunk, nb):
    c_idx = pl.program_id(1)

    @pl.when(c_idx == 0)
    def _():
        c_ref[...] = c0_ref[...]
        n_ref[...] = n0_ref[...]
        m_ref[...] = m0_ref[...]

    row = lax.broadcasted_iota(jnp.int32, (chunk, chunk), 0)
    col = lax.broadcasted_iota(jnp.int32, (chunk, chunk), 1)
    eye = row == col
    tril = col <= row

    def to_row(colvec):
        return jnp.sum(jnp.where(eye, colvec, 0.0), axis=0, keepdims=True)

    for bi in range(nb):
        r0 = bi * chunk
        gates = g_ref[r0:r0 + chunk, :] + bias_ref[...]
        for h in range(H_A):
            lo = h * DH_A
            q = q_ref[r0:r0 + chunk, lo:lo + DH_A]
            k = k_ref[r0:r0 + chunk, lo:lo + DH_A] * (DH_A ** -0.5)
            v = v_ref[r0:r0 + chunk, lo:lo + DH_A]
            oa = oa_ref[r0:r0 + chunk, lo:lo + DH_A]
            c_prev = c_ref[bi, h]
            n_prev = n_ref[bi, h:h + 1, :]
            m_prev = m_ref[bi, :, h:h + 1]
            li_col = gates[:, h:h + 1]
            lf_col = _log_sigmoid(gates[:, H_A + h:H_A + h + 1])
            li_row = to_row(li_col)
            lf_row = to_row(lf_col)
            b_col = jnp.sum(jnp.where(tril, lf_row, 0.0), axis=1, keepdims=True)
            b_row = to_row(b_col)
            dmat = jnp.where(tril, b_col - b_row + li_row, NEG_INF)
            a_col = b_col + m_prev
            mt = jnp.maximum(a_col, jnp.max(dmat, axis=1, keepdims=True))
            wts = jnp.exp(dmat - mt)
            inter = jnp.exp(a_col - mt)
            qb = q.astype(_BF16)
            kb = k.astype(_BF16)
            vb = v.astype(_BF16)
            sqk = _nt_dot(qb, kb) * wts
            num = inter * _dot(qb, c_prev.astype(_BF16)) + _dot(sqk.astype(_BF16), vb)
            nq = (inter * jnp.sum(q * n_prev, axis=1, keepdims=True)
                  + jnp.sum(sqk, axis=1, keepdims=True))
            hid = num / jnp.maximum(jnp.abs(nq), jnp.exp(-mt))
            ga_ref[r0:r0 + chunk, lo:lo + DH_A] = hid * jax.nn.sigmoid(oa)
            b_last = b_col[chunk - 1:chunk, :]
            mt_last = mt[chunk - 1:chunk, :]
            w_last = jnp.exp(b_last - b_col + li_col - mt_last)
            inter_last = inter[chunk - 1:chunk, :]
            kw = k * w_last
            c_ref[bi, h] = inter_last * c_prev + _tn_dot(kw.astype(_BF16), vb)
            n_ref[bi, h:h + 1, :] = inter_last * n_prev + jnp.sum(kw, axis=0, keepdims=True)
            m_ref[bi, :, h:h + 1] = mt_last


def _mlstm(z, gates, bias, c0, n0, m0, *, n_seq, seq_len, chunk, nb):
    t = z.shape[0]
    nc = seq_len // chunk
    rows = nb * chunk
    col = lambda cb: pl.BlockSpec((rows, W_A), lambda b, c: (b * nc + c, cb))
    state_c = pl.BlockSpec((nb, H_A, DH_A, DH_A), lambda b, c: (b, 0, 0, 0))
    state_n = pl.BlockSpec((nb, H_A, DH_A), lambda b, c: (b, 0, 0))
    state_m = pl.BlockSpec((nb, 1, H_A), lambda b, c: (b, 0, 0))
    return pl.pallas_call(
        functools.partial(_mlstm_kernel, chunk=chunk, nb=nb),
        grid=(n_seq // nb, nc),
        in_specs=[col(0), col(1), col(2), col(3),
                  pl.BlockSpec((rows, LANES), lambda b, c: (b * nc + c, 0)),
                  pl.BlockSpec((1, LANES), lambda b, c: (0, 0)),
                  state_c, state_n, state_m],
        out_specs=[pl.BlockSpec((rows, W_A), lambda b, c: (b * nc + c, 0)),
                   state_c, state_n, state_m],
        out_shape=[jax.ShapeDtypeStruct((t, W_A), _F32),
                   jax.ShapeDtypeStruct((n_seq, H_A, DH_A, DH_A), _F32),
                   jax.ShapeDtypeStruct((n_seq, H_A, DH_A), _F32),
                   jax.ShapeDtypeStruct((n_seq, 1, H_A), _F32)],
        compiler_params=_params(("parallel", "arbitrary")),
        name="mlstm",
    )(z, z, z, z, gates, bias, c0, n0, m0)


def _band_kernel(*refs, max_dist, n_qtiles, kv_shared, with_sinks, with_lse):
    idx = 0
    if with_sinks:
        sink_ref = refs[0]
        idx = 1
    q_ref, k_ref, v_ref = refs[idx:idx + 3]
    idx += 3
    o_ref = refs[idx]
    idx += 1
    if with_lse:
        l_ref = refs[idx]
        idx += 1
    kprev_ref, vprev_ref = refs[idx:idx + 2]
    j = pl.program_id(2)
    n_kv = kprev_ref.shape[0]

    @pl.when(j == 0)
    def _():
        kprev_ref[...] = jnp.zeros_like(kprev_ref)
        vprev_ref[...] = jnp.zeros_like(vprev_ref)

    lane = lax.broadcasted_iota(jnp.int32, (BLK, LANES), 1)
    low = lane < DH
    if kv_shared:
        k_in, v_in = k_ref[0], v_ref[0]
        k_sw = pltpu.roll(k_in, DH, axis=1)
        v_sw = pltpu.roll(v_in, DH, axis=1)
        k_tiles = [jnp.where(low, k_in, k_sw), jnp.where(low, k_sw, k_in)]
        v_tiles = [jnp.where(low, v_in, v_sw), jnp.where(low, v_sw, v_in)]
    else:
        k_tiles = [k_ref[0, :, t * LANES:(t + 1) * LANES] for t in range(n_kv)]
        v_tiles = [v_ref[0, :, t * LANES:(t + 1) * LANES] for t in range(n_kv)]
    k_tiles = [t.astype(_BF16) for t in k_tiles]
    v_tiles = [t.astype(_BF16) for t in v_tiles]

    qi = lax.broadcasted_iota(jnp.int32, (BLK, 2 * BLK), 0) + BLK
    ki = lax.broadcasted_iota(jnp.int32, (BLK, 2 * BLK), 1)
    dist = qi - ki
    first_key = jnp.where(j > 0, 0, BLK)
    valid = (dist >= 0) & (dist <= max_dist) & (ki >= first_key)

    q_all = q_ref[0] * (DH ** -0.5)
    tiles_per_kv = n_qtiles // n_kv
    for kt in range(n_kv):
        k2 = jnp.concatenate([kprev_ref[kt], k_tiles[kt]], axis=0)
        v2 = jnp.concatenate([vprev_ref[kt], v_tiles[kt]], axis=0)
        parts = []
        for p in range(kt * tiles_per_kv, (kt + 1) * tiles_per_kv):
            q2 = q_all[:, p * LANES:(p + 1) * LANES]
            parts.append(jnp.where(low, q2, 0.0).astype(_BF16))
            parts.append(jnp.where(low, 0.0, q2).astype(_BF16))
        s_all = _nt_dot(jnp.concatenate(parts, axis=0), k2)
        probs, dens, lses = [], [], []
        for hh in range(2 * tiles_per_kv):
            s = jnp.where(valid, s_all[hh * BLK:(hh + 1) * BLK, :], NEG_INF)
            m = jnp.max(s, axis=1, keepdims=True)
            if with_sinks:
                sk = sink_ref[2 * kt * tiles_per_kv + hh]
                m = jnp.maximum(m, sk)
            p_exp = jnp.exp(s - m)
            den = jnp.sum(p_exp, axis=1, keepdims=True)
            if with_sinks:
                den = den + jnp.exp(sk - m)
            probs.append(p_exp.astype(_BF16))
            dens.append(den)
            lses.append(m + jnp.log(den))
        o_all = _dot(jnp.concatenate(probs, axis=0), v2)
        for pp in range(tiles_per_kv):
            p = kt * tiles_per_kv + pp
            e, o = 2 * pp, 2 * pp + 1
            o_even = o_all[e * BLK:(e + 1) * BLK, :] / dens[e]
            o_odd = o_all[o * BLK:(o + 1) * BLK, :] / dens[o]
            o_ref[0, :, p * LANES:(p + 1) * LANES] = jnp.where(low, o_even, o_odd)
            if with_lse:
                l_ref[0, :, p * LANES:(p + 1) * LANES] = jnp.where(low, lses[e], lses[o])

    for kt in range(n_kv):
        kprev_ref[kt] = k_tiles[kt]
        vprev_ref[kt] = v_tiles[kt]


def _band_attention(z, *, n_seq, seq_len, rate, qcol, kcol, vcol, qw, kw, max_dist,
                    sinks=None, with_lse):
    n = z.shape[1]
    length = seq_len // rate
    nblk = length // BLK
    z3 = z.reshape(n_seq, length, rate * n)
    kv_shared = kw == LANES and qw > LANES
    n_kv = 2 if kv_shared else kw // LANES

    def spec(width, coff):
        assert n % width == 0 or rate == 1
        per_row = n // width if rate > 1 else 0
        return pl.BlockSpec((1, BLK, width), lambda b, r, j: (b, j, r * per_row + coff // width))

    out_spec = pl.BlockSpec((1, BLK, qw), lambda b, r, j: (b, j, r))
    out_shape = [jax.ShapeDtypeStruct((n_seq, length, rate * qw), _F32)]
    out_specs = [out_spec]
    if with_lse:
        out_shape.append(jax.ShapeDtypeStruct((n_seq, length, rate * qw), _F32))
        out_specs.append(out_spec)
    in_specs = [spec(qw, qcol), spec(kw, kcol), spec(kw, vcol)]
    args = [z3, z3, z3]
    if sinks is not None:
        in_specs = [pl.BlockSpec(memory_space=pltpu.SMEM)] + in_specs
        args = [sinks] + args
    res = pl.pallas_call(
        functools.partial(_band_kernel, max_dist=max_dist, n_qtiles=qw // LANES,
                          kv_shared=kv_shared, with_sinks=sinks is not None, with_lse=with_lse),
        grid=(n_seq, rate, nblk),
        in_specs=in_specs, out_specs=out_specs, out_shape=out_shape,
        scratch_shapes=[pltpu.VMEM((n_kv, BLK, LANES), _BF16),
                        pltpu.VMEM((n_kv, BLK, LANES), _BF16)],
        compiler_params=_params(("parallel", "parallel", "arbitrary")),
        name="band_attn",
    )(*args)
    return [r.reshape(n_seq * seq_len, qw) for r in res]


def _outln_ab_kernel(ga_ref, o1_ref, o2_ref, o3_ref, l1_ref, l2_ref, l3_ref, x_ref, w_ref,
                     g_ref, b_ref, y_ref):
    l1, l2, l3 = l1_ref[...], l2_ref[...], l3_ref[...]
    lmax = jnp.maximum(jnp.maximum(l1, l2), l3)
    e1, e2, e3 = jnp.exp(l1 - lmax), jnp.exp(l2 - lmax), jnp.exp(l3 - lmax)
    ob = (e1 * o1_ref[...] + e2 * o2_ref[...] + e3 * o3_ref[...]) / (e1 + e2 + e3)
    mix = (_dot(ga_ref[...].astype(_BF16), w_ref[0:W_A, :])
           + _dot(ob.astype(_BF16), w_ref[W_A:W_A + W_B, :]))
    y_ref[...] = _layer_norm(ALPHA * x_ref[...] + mix, g_ref[...], b_ref[...])


def _outln_c_kernel(o_ref, x_ref, w_ref, g_ref, b_ref, y_ref):
    mix = _dot(o_ref[...].astype(_BF16), w_ref[...])
    y_ref[...] = _layer_norm(ALPHA * x_ref[...] + mix, g_ref[...], b_ref[...])


def _const_spec(shape):
    return pl.BlockSpec(shape, lambda i: (0,) * len(shape), pipeline_mode=pl.Buffered(1))


def _outln(parts, x, w, g, b, tm):
    t = x.shape[0]
    tm = min(tm, t)
    row = lambda width: pl.BlockSpec((tm, width), lambda i: (i, 0))
    kern = _outln_ab_kernel if len(parts) > 1 else _outln_c_kernel
    return pl.pallas_call(
        kern, grid=(t // tm,),
        in_specs=[row(p.shape[1]) for p in parts]
        + [row(D_MODEL), _const_spec(w.shape), _const_spec((1, D_MODEL)), _const_spec((1, D_MODEL))],
        out_specs=row(D_MODEL),
        out_shape=jax.ShapeDtypeStruct((t, D_MODEL), _F32),
        compiler_params=_params(("parallel",)),
        name="outproj_ln",
    )(*parts, x, w, g, b)


def _mlp_kernel(x_ref, wu_ref, wd_ref, g_ref, b_ref, y_ref, *, ff_chunk):
    x = x_ref[...]
    xb = x.astype(_BF16)
    acc = ALPHA * x
    for c in range(D_FF // ff_chunk):
        hid = _dot(xb, wu_ref[:, c * ff_chunk:(c + 1) * ff_chunk])
        hid = jnp.square(jnp.maximum(hid, 0.0)).astype(_BF16)
        acc = acc + _dot(hid, wd_ref[c * ff_chunk:(c + 1) * ff_chunk, :])
    y_ref[...] = _layer_norm(acc, g_ref[...], b_ref[...])


def _mlp(x, wu, wd, g, b, tm, ff_chunk):
    t = x.shape[0]
    tm = min(tm, t)
    row = pl.BlockSpec((tm, D_MODEL), lambda i: (i, 0))
    return pl.pallas_call(
        functools.partial(_mlp_kernel, ff_chunk=ff_chunk), grid=(t // tm,),
        in_specs=[row, _const_spec(wu.shape), _const_spec(wd.shape),
                  _const_spec((1, D_MODEL)), _const_spec((1, D_MODEL))],
        out_specs=row,
        out_shape=jax.ShapeDtypeStruct((t, D_MODEL), _F32),
        compiler_params=_params(("parallel",)),
        name="mlp_ln",
    )(x, wu, wd, g, b)


def _dil_sample_kernel(q_ref, new_ref, cache_ref, o_ref, l_ref, *, rate, t_len):
    row_w = 2 * W_B
    hrow = lax.broadcasted_iota(jnp.int32, (H_B, W_B), 0)
    hlane = lax.broadcasted_iota(jnp.int32, (H_B, W_B), 1) // DH
    own = hrow == hlane
    key_a = lax.broadcasted_iota(jnp.int32, (H_B, BLK), 1)
    new_j = lax.broadcasted_iota(jnp.int32, (H_B, t_len), 1)
    q_all = q_ref[0] * (DH ** -0.5)
    new_k = new_ref[0, :, 0:W_B].astype(_BF16)
    new_v = new_ref[0, :, W_B:row_w].astype(_BF16)
    outs, lses = [], []
    for t in range(t_len):
        tt = t % rate
        k_t = cache_ref[0, :, tt * row_w:tt * row_w + W_B].astype(_BF16)
        v_t = cache_ref[0, :, tt * row_w + W_B:(tt + 1) * row_w].astype(_BF16)
        q_bd = jnp.where(own, q_all[t:t + 1, :], 0.0).astype(_BF16)
        s_c = jnp.where(key_a >= t // rate, _nt_dot(q_bd, k_t), NEG_INF)
        ok_new = (new_j <= t) & ((t - new_j) % rate == 0)
        s_n = jnp.where(ok_new, _nt_dot(q_bd, new_k), NEG_INF)
        m = jnp.maximum(jnp.max(s_c, axis=1, keepdims=True), jnp.max(s_n, axis=1, keepdims=True))
        p_c = jnp.exp(s_c - m)
        p_n = jnp.exp(s_n - m)
        den = jnp.sum(p_c, axis=1, keepdims=True) + jnp.sum(p_n, axis=1, keepdims=True)
        o = (_dot(p_c.astype(_BF16), v_t) + _dot(p_n.astype(_BF16), new_v)) / den
        outs.append(jnp.sum(jnp.where(own, o, 0.0), axis=0, keepdims=True))
        lses.append(jnp.sum(jnp.where(own, m + jnp.log(den), 0.0), axis=0, keepdims=True))
    o_ref[0] = jnp.concatenate(outs, axis=0)
    l_ref[0] = jnp.concatenate(lses, axis=0)


def _dil_sample(q, new_kv, cache, *, rate, t_len):
    bsz, win, row_w = cache.shape
    span = win // rate
    cache_v = cache.reshape(bsz, span, rate * row_w)
    used = min(rate, t_len) * row_w
    out = jax.ShapeDtypeStruct((bsz, t_len, W_B), _F32)
    blk3 = lambda s1, s2: pl.BlockSpec((1, s1, s2), lambda b: (b, 0, 0))
    return pl.pallas_call(
        functools.partial(_dil_sample_kernel, rate=rate, t_len=t_len), grid=(bsz,),
        in_specs=[blk3(t_len, W_B), blk3(t_len, row_w), blk3(span, used)],
        out_specs=[blk3(t_len, W_B), blk3(t_len, W_B)],
        out_shape=[out, out],
        compiler_params=_params(("parallel",)),
        name="dil_sample",
    )(q, new_kv, cache_v)


def _swa_sample_kernel(sink_ref, z_ref, cache_ref, o_ref, *, t_len):
    grp = H_C // KV_C
    tiles = grp // 2
    rows = t_len * grp
    kw = KV_C * DH
    q_w = H_C * DH
    r_t = lax.broadcasted_iota(jnp.int32, (rows, WIN_C), 0) % t_len
    key_i = lax.broadcasted_iota(jnp.int32, (rows, WIN_C), 1)
    ok_c = key_i > r_t
    n_t = lax.broadcasted_iota(jnp.int32, (rows, t_len), 0) % t_len
    n_j = lax.broadcasted_iota(jnp.int32, (rows, t_len), 1)
    ok_n = n_j <= n_t
    r_h = lax.broadcasted_iota(jnp.int32, (rows, 1), 0) // t_len
    low_c = lax.broadcasted_iota(jnp.int32, (WIN_C, LANES), 1) < DH
    low_n = lax.broadcasted_iota(jnp.int32, (t_len, LANES), 1) < DH

    def dup(x, low, j):
        sw = pltpu.roll(x, DH, axis=1)
        return (jnp.where(low, x, sw) if j == 0 else jnp.where(low, sw, x)).astype(_BF16)

    k_c, v_c = cache_ref[0, :, 0:kw], cache_ref[0, :, kw:2 * kw]
    k_n, v_n = z_ref[0, :, q_w:q_w + kw], z_ref[0, :, q_w + kw:q_w + 2 * kw]
    for j in range(KV_C):
        parts = []
        for a in range(tiles):
            c = (j * tiles + a) * LANES
            q2 = z_ref[0, :, c:c + LANES] * (DH ** -0.5)
            parts.append(jnp.where(low_n, q2, 0.0).astype(_BF16))
            parts.append(jnp.where(low_n, 0.0, q2).astype(_BF16))
        q = jnp.concatenate(parts, axis=0)
        sk = jnp.zeros((rows, 1), _F32)
        for g in range(grp):
            sk = jnp.where(r_h == g, sink_ref[j * grp + g], sk)
        s_c = jnp.where(ok_c, _nt_dot(q, dup(k_c, low_c, j)), NEG_INF)
        s_n = jnp.where(ok_n, _nt_dot(q, dup(k_n, low_n, j)), NEG_INF)
        m = jnp.maximum(jnp.maximum(jnp.max(s_c, axis=1, keepdims=True),
                                    jnp.max(s_n, axis=1, keepdims=True)), sk)
        p_c = jnp.exp(s_c - m)
        p_n = jnp.exp(s_n - m)
        den = (jnp.sum(p_c, axis=1, keepdims=True) + jnp.sum(p_n, axis=1, keepdims=True)
               + jnp.exp(sk - m))
        o = (_dot(p_c.astype(_BF16), dup(v_c, low_c, j))
             + _dot(p_n.astype(_BF16), dup(v_n, low_n, j))) / den
        for a in range(tiles):
            c = (j * tiles + a) * LANES
            even = o[(2 * a) * t_len:(2 * a + 1) * t_len, :]
            odd = o[(2 * a + 1) * t_len:(2 * a + 2) * t_len, :]
            o_ref[0, :, c:c + LANES] = jnp.where(low_n, even, odd)


def _swa_sample_tokens(zc_s, cache_swa, sinks, *, dbs, t_len):
    kw = KV_C * DH
    return pl.pallas_call(
        functools.partial(_swa_sample_kernel, t_len=t_len), grid=(dbs,),
        in_specs=[pl.BlockSpec(memory_space=pltpu.SMEM),
                  pl.BlockSpec((1, t_len, N_C), lambda b: (b, 0, 0)),
                  pl.BlockSpec((1, WIN_C, 2 * kw), lambda b: (b, 0, 0))],
        out_specs=pl.BlockSpec((1, t_len, H_C * DH), lambda b: (b, 0, 0)),
        out_shape=jax.ShapeDtypeStruct((dbs, t_len, H_C * DH), _F32),
        compiler_params=_params(("parallel",)),
        name="swa_sample",
    )(sinks, zc_s.reshape(dbs, t_len, N_C), cache_swa).reshape(dbs * t_len, H_C * DH)


def _shift_kernel(*refs, n_arrays, t_len, n_split):
    caches = refs[0:n_arrays]
    news = refs[n_arrays:2 * n_arrays]
    outs = refs[2 * n_arrays:3 * n_arrays]
    sem = refs[3 * n_arrays]
    copies = []
    for a in range(n_arrays):
        bsz, win, _ = caches[a].shape
        step = bsz // n_split
        for s in range(n_split):
            bs = pl.ds(s * step, step)
            copies.append(pltpu.make_async_copy(
                caches[a].at[bs, pl.ds(t_len, win - t_len), :],
                outs[a].at[bs, pl.ds(0, win - t_len), :],
                sem.at[a * (n_split + 1) + s]))
        copies.append(pltpu.make_async_copy(
            news[a], outs[a].at[:, pl.ds(win - t_len, t_len), :], sem.at[a * (n_split + 1) + n_split]))
    for cp in copies:
        cp.start()
    for cp in copies:
        cp.wait()


def _shift_caches(caches, news, *, t_len, n_split):
    n_arrays = len(caches)
    any_spec = pl.BlockSpec(memory_space=pl.ANY)
    return pl.pallas_call(
        functools.partial(_shift_kernel, n_arrays=n_arrays, t_len=t_len, n_split=n_split),
        in_specs=[any_spec] * (2 * n_arrays),
        out_specs=[any_spec] * n_arrays,
        out_shape=[jax.ShapeDtypeStruct(c.shape, c.dtype) for c in caches],
        scratch_shapes=[pltpu.SemaphoreType.DMA((n_arrays * (n_split + 1),))],
        compiler_params=pltpu.CompilerParams(has_side_effects=True),
        name="cache_shift",
    )(*caches, *news)


def _rope_flags(n, lo, hi):
    c = jnp.arange(n // LANES, dtype=jnp.int32) * LANES
    return ((c >= lo) & (c < hi)).astype(jnp.int32)


def kernel(x_prompt, x_sample, state_mlstm_C, state_mlstm_n, state_mlstm_m, cache_dil1_kv, cache_dil2_kv, cache_dil3_kv, cache_swa_kv, w_in_ab, b_gate_ab, w_out_ab, w_in_c, sinks_c, w_out_c, ln1_g, ln1_b, ln2_g, ln2_b, w_up, w_down):
    bsz, s_len, _ = x_prompt.shape
    dbs, t_len, _ = x_sample.shape
    tp = bsz * s_len
    ts = dbs * t_len
    xp = x_prompt.reshape(tp, D_MODEL)
    xs = x_sample.reshape(ts, D_MODEL)
    pos_p = jnp.arange(s_len, dtype=jnp.int32)
    pos_s = PAST_LEN + jnp.arange(t_len, dtype=jnp.int32)
    tab_p = _rope_tables(pos_p)
    tab_s = tuple(jnp.tile(t, (ts // t_len, 1)) for t in _rope_tables(pos_s))
    row2 = lambda v: v.reshape(1, -1)

    w_in = w_in_ab[0]
    gate_lo = 4 * W_A
    w_main = jnp.concatenate([w_in[:, :gate_lo], w_in[:, gate_lo + 2 * H_A:]], axis=1).astype(_BF16)
    w_gate = jnp.pad(w_in[:, gate_lo:gate_lo + 2 * H_A], ((0, 0), (0, LANES - 2 * H_A))).astype(_BF16)
    bias = jnp.pad(b_gate_ab[0], (0, LANES - 2 * H_A)).reshape(1, LANES)
    q_lo = 4 * W_A
    k_lo = q_lo + N_DIL * W_B
    v_lo = k_lo + N_DIL * W_B
    flags_ab = _rope_flags(N_AB, q_lo, v_lo)
    w_out0 = w_out_ab[0].astype(_BF16)

    z_p, gates_p = _project(xp, w_main, tab_p, flags_ab, w_gate, tm=1024, tn=512)
    z_s, gates_s = _project(xs, w_main, tab_s, flags_ab, w_gate, tm=ts, tn=512)

    zeros_c = jnp.zeros((bsz, H_A, DH_A, DH_A), _F32)
    zeros_n = jnp.zeros((bsz, H_A, DH_A), _F32)
    zeros_m = jnp.zeros((bsz, 1, H_A), _F32)
    ga_p, mc_p, mn_p, mm_p = _mlstm(z_p, gates_p, bias, zeros_c, zeros_n, zeros_m,
                                    n_seq=bsz, seq_len=s_len, chunk=256, nb=1)
    ga_s, mc_s, mn_s, mm_s = _mlstm(z_s, gates_s, bias, state_mlstm_C[0], state_mlstm_n[0],
                                    state_mlstm_m[0].reshape(dbs, 1, H_A),
                                    n_seq=dbs, seq_len=t_len, chunk=t_len, nb=8)

    dil_caches = (cache_dil1_kv, cache_dil2_kv, cache_dil3_kv)
    outs_p, lses_p, outs_s, lses_s, dil_kv_p, new_rows = [], [], [], [], [], []
    z_s3 = z_s.reshape(dbs, t_len, N_AB)
    z_p3 = z_p.reshape(bsz, s_len, N_AB)
    for g in range(N_DIL):
        rate, win = DIL_RATES[g], DIL_WINDOWS[g]
        qc, kc, vc = q_lo + g * W_B, k_lo + g * W_B, v_lo + g * W_B
        o, l = _band_attention(z_p, n_seq=bsz, seq_len=s_len, rate=rate, qcol=qc, kcol=kc, vcol=vc,
                               qw=W_B, kw=W_B, max_dist=win // rate, with_lse=True)
        outs_p.append(o)
        lses_p.append(l)
        keep = min(win, s_len)
        dil_kv_p.append(jnp.concatenate([z_p3[:, s_len - keep:, kc:kc + W_B],
                                         z_p3[:, s_len - keep:, vc:vc + W_B]], axis=-1)
                        .reshape(1, bsz, keep, 2, H_B, DH))
        new_kv = jnp.concatenate([z_s3[:, :, kc:kc + W_B], z_s3[:, :, vc:vc + W_B]], axis=-1)
        new_rows.append(new_kv)
        cache = dil_caches[g][0].reshape(dbs, win, 2 * W_B)
        o, l = _dil_sample(z_s3[:, :, qc:qc + W_B], new_kv, cache, rate=rate, t_len=t_len)
        outs_s.append(o.reshape(ts, W_B))
        lses_s.append(l.reshape(ts, W_B))

    xp = _outln([ga_p] + outs_p + lses_p, xp, w_out0, row2(ln1_g[0]), row2(ln1_b[0]), tm=512)
    xs = _outln([ga_s] + outs_s + lses_s, xs, w_out0, row2(ln1_g[0]), row2(ln1_b[0]), tm=512)
    wu0, wd0 = w_up[0].astype(_BF16), w_down[0].astype(_BF16)
    xp = _mlp(xp, wu0, wd0, row2(ln2_g[0]), row2(ln2_b[0]), tm=512, ff_chunk=1024)
    xs = _mlp(xs, wu0, wd0, row2(ln2_g[0]), row2(ln2_b[0]), tm=512, ff_chunk=1024)

    w_c = w_in_c[0].astype(_BF16)
    kc1 = H_C * DH
    vc1 = kc1 + KV_C * DH
    flags_c = _rope_flags(N_C, 0, vc1)
    zc_p = _project(xp, w_c, tab_p, flags_c, None, tm=1024, tn=640)
    zc_s = _project(xs, w_c, tab_s, flags_c, None, tm=ts, tn=640)
    (o_p,) = _band_attention(zc_p, n_seq=bsz, seq_len=s_len, rate=1, qcol=0, kcol=kc1, vcol=vc1,
                             qw=H_C * DH, kw=KV_C * DH, max_dist=WIN_C - 1, sinks=sinks_c[0],
                             with_lse=False)
    keep = min(WIN_C, s_len)
    swa_kv_p = zc_p.reshape(bsz, s_len, N_C)[:, s_len - keep:, kc1:].reshape(1, bsz, keep, 2, KV_C, DH)

    cache_swa = cache_swa_kv[0].reshape(dbs, WIN_C, 2 * KV_C * DH)
    new_swa = zc_s.reshape(dbs, t_len, N_C)[:, :, kc1:]
    o_s = _swa_sample_tokens(zc_s, cache_swa, sinks_c[0], dbs=dbs, t_len=t_len)

    w_out1 = w_out_c[0].astype(_BF16)
    xp = _outln([o_p], xp, w_out1, row2(ln1_g[1]), row2(ln1_b[1]), tm=512)
    xs = _outln([o_s], xs, w_out1, row2(ln1_g[1]), row2(ln1_b[1]), tm=512)
    wu1, wd1 = w_up[1].astype(_BF16), w_down[1].astype(_BF16)
    xp = _mlp(xp, wu1, wd1, row2(ln2_g[1]), row2(ln2_b[1]), tm=512, ff_chunk=1024)
    xs = _mlp(xs, wu1, wd1, row2(ln2_g[1]), row2(ln2_b[1]), tm=512, ff_chunk=1024)

    caches = [dil_caches[g][0].reshape(dbs, DIL_WINDOWS[g], 2 * W_B) for g in range(N_DIL)] + [cache_swa]
    shifted = _shift_caches(caches, new_rows + [new_swa], t_len=t_len, n_split=4)
    dil_kv_s = [shifted[g].reshape(1, dbs, DIL_WINDOWS[g], 2, H_B, DH) for g in range(N_DIL)]
    swa_kv_s = shifted[N_DIL].reshape(1, dbs, WIN_C, 2, KV_C, DH)

    return (xp.reshape(bsz, s_len, D_MODEL), xs.reshape(dbs, t_len, D_MODEL),
            mc_p[None], mc_s[None], mn_p[None], mn_s[None],
            mm_p.reshape(1, bsz, H_A), mm_s.reshape(1, dbs, H_A),
            dil_kv_p[0], dil_kv_s[0], dil_kv_p[1], dil_kv_s[1], dil_kv_p[2], dil_kv_s[2],
            swa_kv_p, swa_kv_s)
```

```python
import functools

import jax
import jax.numpy as jnp
from jax import lax
from jax.experimental import pallas as pl
from jax.experimental.pallas import tpu as pltpu

LANES = 128
SUBLANES = 8
VMEM_LIMIT = 56 * 1024 * 1024

D_MODEL = 1024
DH = 64
ROT_DIM = DH // 4
ROPE_THETA = 500000.0
PAST_LEN = 8192
BLK = 128
H_A = 4
DH_A = 128
W_A = H_A * DH_A
N_DIL = 3
DIL_WINDOWS = (128, 512, 2048)
DIL_RATES = (1, 4, 16)
H_B = 8
W_B = H_B * DH
H_C = 16
KV_C = 2
WIN_C = 128
D_FF = 4 * D_MODEL
DEPTH = 2
ALPHA = (2.0 * DEPTH) ** 0.25
LN_EPS = 1e-5
N_AB = 4 * W_A + 3 * N_DIL * W_B
N_C = (H_C + 2 * KV_C) * DH
NEG_INF = float("-inf")

_F32 = jnp.float32
_BF16 = jnp.bfloat16


def _params(sem):
    return pltpu.CompilerParams(dimension_semantics=sem, vmem_limit_bytes=VMEM_LIMIT)


def _nt_dot(a, b):
    return lax.dot_general(a, b, (((1,), (1,)), ((), ())), preferred_element_type=_F32)


def _tn_dot(a, b):
    return lax.dot_general(a, b, (((0,), (0,)), ((), ())), preferred_element_type=_F32)


def _dot(a, b):
    return jnp.dot(a, b, preferred_element_type=_F32)


def _log_sigmoid(x):
    return -(jnp.maximum(-x, 0.0) + jnp.log1p(jnp.exp(-jnp.abs(x))))


def _layer_norm(y, g, b):
    mu = jnp.mean(y, axis=-1, keepdims=True)
    yc = y - mu
    var = jnp.mean(yc * yc, axis=-1, keepdims=True)
    return yc * lax.rsqrt(var + LN_EPS) * g + b


def _rope_tables(pos):
    half = ROT_DIM // 2
    inv = ROPE_THETA ** (-jnp.arange(half, dtype=_F32) / half)
    ang = pos.astype(_F32)[:, None] * inv[None, :]
    cos = jnp.cos(ang)
    sin = jnp.sin(ang)
    n = pos.shape[0]
    ones = jnp.ones((n, DH - ROT_DIM), _F32)
    zeros = jnp.zeros((n, DH - ROT_DIM), _F32)
    zh = jnp.zeros((n, half), _F32)
    cos_h = jnp.concatenate([cos, cos, ones], axis=1)
    sinm_h = jnp.concatenate([-sin, zh, zeros], axis=1)
    sinp_h = jnp.concatenate([zh, sin, zeros], axis=1)
    tile2 = lambda t: jnp.concatenate([t, t], axis=1)
    return tile2(cos_h), tile2(sinm_h), tile2(sinp_h)


def _proj_kernel(flags_ref, x_ref, w_ref, cos_ref, sinm_ref, sinp_ref, *rest, tn, with_gates):
    if with_gates:
        wg_ref, o_ref, g_ref, xb_ref = rest
    else:
        o_ref, xb_ref = rest
    j = pl.program_id(1)

    @pl.when(j == 0)
    def _():
        xb_ref[...] = x_ref[...].astype(_BF16)
        if with_gates:
            g_ref[...] = _dot(xb_ref[...], wg_ref[...])

    acc = _dot(xb_ref[...], w_ref[...])
    half = ROT_DIM // 2
    for cc in range(tn // LANES):
        sub = acc[:, cc * LANES:(cc + 1) * LANES]
        flag = flags_ref[j * (tn // LANES) + cc]

        @pl.when(flag == 1)
        def _():
            rot = (sub * cos_ref[...]
                   + pltpu.roll(sub, LANES - half, axis=1) * sinm_ref[...]
                   + pltpu.roll(sub, half, axis=1) * sinp_ref[...])
            o_ref[:, cc * LANES:(cc + 1) * LANES] = rot

        @pl.when(flag == 0)
        def _():
            o_ref[:, cc * LANES:(cc + 1) * LANES] = sub


def _project(x, w, tables, flags, wg, tm, tn):
    t, d = x.shape
    n = w.shape[1]
    cos, sinm, sinp = tables
    tab_blocks = cos.shape[0] // tm
    with_gates = wg is not None
    row_tab = pl.BlockSpec((tm, LANES), lambda i, j, f: (i % tab_blocks, 0))
    in_specs = [
        pl.BlockSpec((tm, d), lambda i, j, f: (i, 0)),
        pl.BlockSpec((d, tn), lambda i, j, f: (0, j)),
        row_tab, row_tab, row_tab,
    ]
    out_shape = [jax.ShapeDtypeStruct((t, n), _F32)]
    out_specs = [pl.BlockSpec((tm, tn), lambda i, j, f: (i, j))]
    args = [x, w, cos, sinm, sinp]
    if with_gates:
        in_specs.append(pl.BlockSpec((d, LANES), lambda i, j, f: (0, 0)))
        out_shape.append(jax.ShapeDtypeStruct((t, LANES), _F32))
        out_specs.append(pl.BlockSpec((tm, LANES), lambda i, j, f: (i, 0)))
        args.append(wg)
    res = pl.pallas_call(
        functools.partial(_proj_kernel, tn=tn, with_gates=with_gates),
        grid_spec=pltpu.PrefetchScalarGridSpec(
            num_scalar_prefetch=1, grid=(t // tm, n // tn),
            in_specs=in_specs, out_specs=out_specs,
            scratch_shapes=[pltpu.VMEM((tm, d), _BF16)]),
        out_shape=out_shape,
        compiler_params=_params(("parallel", "arbitrary")),
        name="proj_gates" if with_gates else "proj",
    )(flags, *args)
    return res if with_gates else res[0]


def _mlstm_kernel(q_ref, k_ref, v_ref, oa_ref, g_ref, bias_ref, c0_ref, n0_ref, m0_ref,
                  ga_ref, c_ref, n_ref, m_ref, *, chunk, nb):
    c_idx = pl.program_id(1)

    @pl.when(c_idx == 0)
    def _():
        c_ref[...] = c0_ref[...]
        n_ref[...] = n0_ref[...]
        m_ref[...] = m0_ref[...]

    row = lax.broadcasted_iota(jnp.int32, (chunk, chunk), 0)
    col = lax.broadcasted_iota(jnp.int32, (chunk, chunk), 1)
    eye = row == col
    tril = col <= row

    def to_row(colvec):
        return jnp.sum(jnp.where(eye, colvec, 0.0), axis=0, keepdims=True)

    for bi in range(nb):
        r0 = bi * chunk
        gates = g_ref[r0:r0 + chunk, :] + bias_ref[...]
        for h in range(H_A):
            lo = h * DH_A
            q = q_ref[r0:r0 + chunk, lo:lo + DH_A]
            k = k_ref[r0:r0 + chunk, lo:lo + DH_A] * (DH_A ** -0.5)
            v = v_ref[r0:r0 + chunk, lo:lo + DH_A]
            oa = oa_ref[r0:r0 + chunk, lo:lo + DH_A]
            c_prev = c_ref[bi, h]
            n_prev = n_ref[bi, h:h + 1, :]
            m_prev = m_ref[bi, :, h:h + 1]
            li_col = gates[:, h:h + 1]
            lf_col = _log_sigmoid(gates[:, H_A + h:H_A + h + 1])
            li_row = to_row(li_col)
            lf_row = to_row(lf_col)
            b_col = jnp.sum(jnp.where(tril, lf_row, 0.0), axis=1, keepdims=True)
            b_row = to_row(b_col)
            dmat = jnp.where(tril, b_col - b_row + li_row, NEG_INF)
            a_col = b_col + m_prev
            mt = jnp.maximum(a_col, jnp.max(dmat, axis=1, keepdims=True))
            wts = jnp.exp(dmat - mt)
            inter = jnp.exp(a_col - mt)
            qb = q.astype(_BF16)
            kb = k.astype(_BF16)
            vb = v.astype(_BF16)
            sqk = _nt_dot(qb, kb) * wts
            num = inter * _dot(qb, c_prev.astype(_BF16)) + _dot(sqk.astype(_BF16), vb)
            nq = (inter * jnp.sum(q * n_prev, axis=1, keepdims=True)
                  + jnp.sum(sqk, axis=1, keepdims=True))
            hid = num / jnp.maximum(jnp.abs(nq), jnp.exp(-mt))
            ga_ref[r0:r0 + chunk, lo:lo + DH_A] = hid * jax.nn.sigmoid(oa)
            b_last = b_col[chunk - 1:chunk, :]
            mt_last = mt[chunk - 1:chunk, :]
            w_last = jnp.exp(b_last - b_col + li_col - mt_last)
            inter_last = inter[chunk - 1:chunk, :]
            kw = k * w_last
            c_ref[bi, h] = inter_last * c_prev + _tn_dot(kw.astype(_BF16), vb)
            n_ref[bi, h:h + 1, :] = inter_last * n_prev + jnp.sum(kw, axis=0, keepdims=True)
            m_ref[bi, :, h:h + 1] = mt_last


def _mlstm(z, gates, bias, c0, n0, m0, *, n_seq, seq_len, chunk, nb):
    t = z.shape[0]
    nc = seq_len // chunk
    rows = nb * chunk
    col = lambda cb: pl.BlockSpec((rows, W_A), lambda b, c: (b * nc + c, cb))
    state_c = pl.BlockSpec((nb, H_A, DH_A, DH_A), lambda b, c: (b, 0, 0, 0))
    state_n = pl.BlockSpec((nb, H_A, DH_A), lambda b, c: (b, 0, 0))
    state_m = pl.BlockSpec((nb, 1, H_A), lambda b, c: (b, 0, 0))
    return pl.pallas_call(
        functools.partial(_mlstm_kernel, chunk=chunk, nb=nb),
        grid=(n_seq // nb, nc),
        in_specs=[col(0), col(1), col(2), col(3),
                  pl.BlockSpec((rows, LANES), lambda b, c: (b * nc + c, 0)),
                  pl.BlockSpec((1, LANES), lambda b, c: (0, 0)),
                  state_c, state_n, state_m],
        out_specs=[pl.BlockSpec((rows, W_A), lambda b, c: (b * nc + c, 0)),
                   state_c, state_n, state_m],
        out_shape=[jax.ShapeDtypeStruct((t, W_A), _F32),
                   jax.ShapeDtypeStruct((n_seq, H_A, DH_A, DH_A), _F32),
                   jax.ShapeDtypeStruct((n_seq, H_A, DH_A), _F32),
                   jax.ShapeDtypeStruct((n_seq, 1, H_A), _F32)],
        compiler_params=_params(("parallel", "arbitrary")),
        name="mlstm",
    )(z, z, z, z, gates, bias, c0, n0, m0)


def _band_kernel(*refs, max_dist, n_qtiles, kv_shared, with_sinks, with_lse):
    idx = 0
    if with_sinks:
        sink_ref = refs[0]
        idx = 1
    q_ref, k_ref, v_ref = refs[idx:idx + 3]
    idx += 3
    o_ref = refs[idx]
    idx += 1
    if with_lse:
        l_ref = refs[idx]
        idx += 1
    kprev_ref, vprev_ref = refs[idx:idx + 2]
    j = pl.program_id(2)
    n_kv = kprev_ref.shape[0]

    @pl.when(j == 0)
    def _():
        kprev_ref[...] = jnp.zeros_like(kprev_ref)
        vprev_ref[...] = jnp.zeros_like(vprev_ref)

    lane = lax.broadcasted_iota(jnp.int32, (BLK, LANES), 1)
    low = lane < DH
    if kv_shared:
        k_in, v_in = k_ref[0], v_ref[0]
        k_sw = pltpu.roll(k_in, DH, axis=1)
        v_sw = pltpu.roll(v_in, DH, axis=1)
        k_tiles = [jnp.where(low, k_in, k_sw), jnp.where(low, k_sw, k_in)]
        v_tiles = [jnp.where(low, v_in, v_sw), jnp.where(low, v_sw, v_in)]
    else:
        k_tiles = [k_ref[0, :, t * LANES:(t + 1) * LANES] for t in range(n_kv)]
        v_tiles = [v_ref[0, :, t * LANES:(t + 1) * LANES] for t in range(n_kv)]
    k_tiles = [t.astype(_BF16) for t in k_tiles]
    v_tiles = [t.astype(_BF16) for t in v_tiles]

    qi = lax.broadcasted_iota(jnp.int32, (BLK, 2 * BLK), 0) + BLK
    ki = lax.broadcasted_iota(jnp.int32, (BLK, 2 * BLK), 1)
    dist = qi - ki
    first_key = jnp.where(j > 0, 0, BLK)
    valid = (dist >= 0) & (dist <= max_dist) & (ki >= first_key)

    q_all = q_ref[0] * (DH ** -0.5)
    tiles_per_kv = n_qtiles // n_kv
    for kt in range(n_kv):
        k2 = jnp.concatenate([kprev_ref[kt], k_tiles[kt]], axis=0)
        v2 = jnp.concatenate([vprev_ref[kt], v_tiles[kt]], axis=0)
        parts = []
        for p in range(kt * tiles_per_kv, (kt + 1) * tiles_per_kv):
            q2 = q_all[:, p * LANES:(p + 1) * LANES]
            parts.append(jnp.where(low, q2, 0.0).astype(_BF16))
            parts.append(jnp.where(low, 0.0, q2).astype(_BF16))
        s_all = _nt_dot(jnp.concatenate(parts, axis=0), k2)
        probs, dens, lses = [], [], []
        for hh in range(2 * tiles_per_kv):
            s = jnp.where(valid, s_all[hh * BLK:(hh + 1) * BLK, :], NEG_INF)
            m = jnp.max(s, axis=1, keepdims=True)
            if with_sinks:
                sk = sink_ref[2 * kt * tiles_per_kv + hh]
                m = jnp.maximum(m, sk)
            p_exp = jnp.exp(s - m)
            den = jnp.sum(p_exp, axis=1, keepdims=True)
            if with_sinks:
                den = den + jnp.exp(sk - m)
            probs.append(p_exp.astype(_BF16))
            dens.append(den)
            lses.append(m + jnp.log(den))
        o_all = _dot(jnp.concatenate(probs, axis=0), v2)
        for pp in range(tiles_per_kv):
            p = kt * tiles_per_kv + pp
            e, o = 2 * pp, 2 * pp + 1
            o_even = o_all[e * BLK:(e + 1) * BLK, :] / dens[e]
            o_odd = o_all[o * BLK:(o + 1) * BLK, :] / dens[o]
            o_ref[0, :, p * LANES:(p + 1) * LANES] = jnp.where(low, o_even, o_odd)
            if with_lse:
                l_ref[0, :, p * LANES:(p + 1) * LANES] = jnp.where(low, lses[e], lses[o])

    for kt in range(n_kv):
        kprev_ref[kt] = k_tiles[kt]
        vprev_ref[kt] = v_tiles[kt]


def _band_attention(z, *, n_seq, seq_len, rate, qcol, kcol, vcol, qw, kw, max_dist,
                    sinks=None, with_lse):
    n = z.shape[1]
    length = seq_len // rate
    nblk = length // BLK
    z3 = z.reshape(n_seq, length, rate * n)
    kv_shared = kw == LANES and qw > LANES
    n_kv = 2 if kv_shared else kw // LANES

    def spec(width, coff):
        assert n % width == 0 or rate == 1
        per_row = n // width if rate > 1 else 0
        return pl.BlockSpec((1, BLK, width), lambda b, r, j: (b, j, r * per_row + coff // width))

    out_spec = pl.BlockSpec((1, BLK, qw), lambda b, r, j: (b, j, r))
    out_shape = [jax.ShapeDtypeStruct((n_seq, length, rate * qw), _F32)]
    out_specs = [out_spec]
    if with_lse:
        out_shape.append(jax.ShapeDtypeStruct((n_seq, length, rate * qw), _F32))
        out_specs.append(out_spec)
    in_specs = [spec(qw, qcol), spec(kw, kcol), spec(kw, vcol)]
    args = [z3, z3, z3]
    if sinks is not None:
        in_specs = [pl.BlockSpec(memory_space=pltpu.SMEM)] + in_specs
        args = [sinks] + args
    res = pl.pallas_call(
        functools.partial(_band_kernel, max_dist=max_dist, n_qtiles=qw // LANES,
                          kv_shared=kv_shared, with_sinks=sinks is not None, with_lse=with_lse),
        grid=(n_seq, rate, nblk),
        in_specs=in_specs, out_specs=out_specs, out_shape=out_shape,
        scratch_shapes=[pltpu.VMEM((n_kv, BLK, LANES), _BF16),
                        pltpu.VMEM((n_kv, BLK, LANES), _BF16)],
        compiler_params=_params(("parallel", "parallel", "arbitrary")),
        name="band_attn",
    )(*args)
    return [r.reshape(n_seq * seq_len, qw) for r in res]


def _outln_ab_kernel(ga_ref, o1_ref, o2_ref, o3_ref, l1_ref, l2_ref, l3_ref, x_ref, w_ref,
                     g_ref, b_ref, y_ref):
    l1, l2, l3 = l1_ref[...], l2_ref[...], l3_ref[...]
    lmax = jnp.maximum(jnp.maximum(l1, l2), l3)
    e1, e2, e3 = jnp.exp(l1 - lmax), jnp.exp(l2 - lmax), jnp.exp(l3 - lmax)
    ob = (e1 * o1_ref[...] + e2 * o2_ref[...] + e3 * o3_ref[...]) / (e1 + e2 + e3)
    mix = (_dot(ga_ref[...].astype(_BF16), w_ref[0:W_A, :])
           + _dot(ob.astype(_BF16), w_ref[W_A:W_A + W_B, :]))
    y_ref[...] = _layer_norm(ALPHA * x_ref[...] + mix, g_ref[...], b_ref[...])


def _outln_c_kernel(o_ref, x_ref, w_ref, g_ref, b_ref, y_ref):
    mix = _dot(o_ref[...].astype(_BF16), w_ref[...])
    y_ref[...] = _layer_norm(ALPHA * x_ref[...] + mix, g_ref[...], b_ref[...])


def _const_spec(shape):
    return pl.BlockSpec(shape, lambda i: (0,) * len(shape), pipeline_mode=pl.Buffered(1))


def _outln(parts, x, w, g, b, tm):
    t = x.shape[0]
    tm = min(tm, t)
    row = lambda width: pl.BlockSpec((tm, width), lambda i: (i, 0))
    kern = _outln_ab_kernel if len(parts) > 1 else _outln_c_kernel
    return pl.pallas_call(
        kern, grid=(t // tm,),
        in_specs=[row(p.shape[1]) for p in parts]
        + [row(D_MODEL), _const_spec(w.shape), _const_spec((1, D_MODEL)), _const_spec((1, D_MODEL))],
        out_specs=row(D_MODEL),
        out_shape=jax.ShapeDtypeStruct((t, D_MODEL), _F32),
        compiler_params=_params(("parallel",)),
        name="outproj_ln",
    )(*parts, x, w, g, b)


def _mlp_kernel(x_ref, wu_ref, wd_ref, g_ref, b_ref, y_ref, *, ff_chunk):
    x = x_ref[...]
    xb = x.astype(_BF16)
    acc = ALPHA * x
    for c in range(D_FF // ff_chunk):
        hid = _dot(xb, wu_ref[:, c * ff_chunk:(c + 1) * ff_chunk])
        hid = jnp.square(jnp.maximum(hid, 0.0)).astype(_BF16)
        acc = acc + _dot(hid, wd_ref[c * ff_chunk:(c + 1) * ff_chunk, :])
    y_ref[...] = _layer_norm(acc, g_ref[...], b_ref[...])


def _mlp(x, wu, wd, g, b, tm, ff_chunk):
    t = x.shape[0]
    tm = min(tm, t)
    row = pl.BlockSpec((tm, D_MODEL), lambda i: (i, 0))
    return pl.pallas_call(
        functools.partial(_mlp_kernel, ff_chunk=ff_chunk), grid=(t // tm,),
        in_specs=[row, _const_spec(wu.shape), _const_spec(wd.shape),
                  _const_spec((1, D_MODEL)), _const_spec((1, D_MODEL))],
        out_specs=row,
        out_shape=jax.ShapeDtypeStruct((t, D_MODEL), _F32),
        compiler_params=_params(("parallel",)),
        name="mlp_ln",
    )(x, wu, wd, g, b)


def _roll_window(old, new_tail, t_len):
    win = old.shape[1]
    rolled = pltpu.roll(old, win - t_len, axis=1)
    tail_lane = lax.broadcasted_iota(jnp.int32, new_tail.shape, 1) >= LANES - t_len
    tail = jnp.where(tail_lane, new_tail, rolled[:, win - LANES:])
    if win == LANES:
        return tail
    return jnp.concatenate([rolled[:, :win - LANES], tail], axis=1)


def _dil_cache_kernel(q_ref, k_ref, v_ref, newt_ref, cache_ref, o_ref, l_ref, out_ref, *,
                      rate, t_len):
    win = cache_ref.shape[-1]
    rows = 2 * t_len
    key_i = lax.broadcasted_iota(jnp.int32, (rows, win), 1)
    row_t = lax.broadcasted_iota(jnp.int32, (rows, win), 0) % t_len
    ok_c = (key_i >= row_t) & (((key_i - row_t) & (rate - 1)) == 0)
    n_j = lax.broadcasted_iota(jnp.int32, (rows, t_len), 1)
    n_t = lax.broadcasted_iota(jnp.int32, (rows, t_len), 0) % t_len
    ok_n = (n_j <= n_t) & (((n_t - n_j) & (rate - 1)) == 0)
    low = lax.broadcasted_iota(jnp.int32, (t_len, LANES), 1) < DH
    for a in range(H_B // 2):
        c = a * LANES
        q2 = q_ref[0, :, c:c + LANES] * (DH ** -0.5)
        lhs = jnp.concatenate([jnp.where(low, q2, 0.0), jnp.where(low, 0.0, q2)], axis=0).astype(_BF16)
        k_old = cache_ref[0, 0, 2 * a:2 * a + 2].reshape(2 * DH, win)
        v_old = cache_ref[0, 1, 2 * a:2 * a + 2].reshape(2 * DH, win)
        k_new = k_ref[0, :, c:c + LANES]
        v_new = v_ref[0, :, c:c + LANES]
        s_c = jnp.where(ok_c, _dot(lhs, k_old.astype(_BF16)), NEG_INF)
        s_n = jnp.where(ok_n, _nt_dot(lhs, k_new.astype(_BF16)), NEG_INF)
        m = jnp.maximum(jnp.max(s_c, axis=1, keepdims=True), jnp.max(s_n, axis=1, keepdims=True))
        p_c = jnp.exp(s_c - m)
        p_n = jnp.exp(s_n - m)
        den = jnp.sum(p_c, axis=1, keepdims=True) + jnp.sum(p_n, axis=1, keepdims=True)
        o = (_nt_dot(p_c.astype(_BF16), v_old.astype(_BF16))
             + _dot(p_n.astype(_BF16), v_new.astype(_BF16))) / den
        lse = m + jnp.log(den)
        o_ref[0, :, c:c + LANES] = jnp.where(low, o[0:t_len], o[t_len:rows])
        l_ref[0, :, c:c + LANES] = jnp.where(low, lse[0:t_len], lse[t_len:rows])
        for kv, old in ((0, k_old), (1, v_old)):
            new_tail = newt_ref[0, kv, 2 * a:2 * a + 2].reshape(2 * DH, LANES)
            out_ref[0, kv, 2 * a:2 * a + 2] = _roll_window(old, new_tail, t_len).reshape(2, DH, win)


def _time_minor_tail(new, t_len):
    bsz = new.shape[0]
    heads = new.shape[2] // DH
    t = jnp.transpose(new.reshape(bsz, t_len, heads, DH), (0, 2, 3, 1))
    return jnp.pad(t, ((0, 0), (0, 0), (0, 0), (LANES - t_len, 0)))


def _dil_cache(z_s3, cache_t, *, g, rate, t_len):
    bsz, _, _, _, win = cache_t.shape
    n_col = N_AB // W_B
    q_blk, k_blk, v_blk = 4 * W_A // W_B + g, 4 * W_A // W_B + N_DIL + g, 4 * W_A // W_B + 2 * N_DIL + g
    assert n_col * W_B == N_AB
    tail = jnp.stack([_time_minor_tail(z_s3[:, :, k_blk * W_B:(k_blk + 1) * W_B], t_len),
                      _time_minor_tail(z_s3[:, :, v_blk * W_B:(v_blk + 1) * W_B], t_len)], axis=1)
    col = lambda cb: pl.BlockSpec((1, t_len, W_B), lambda b: (b, 0, cb))
    tok = pl.BlockSpec((1, t_len, W_B), lambda b: (b, 0, 0))
    blk5 = lambda last: pl.BlockSpec((1, 2, H_B, DH, last), lambda b: (b, 0, 0, 0, 0))
    out = jax.ShapeDtypeStruct((bsz, t_len, W_B), _F32)
    return pl.pallas_call(
        functools.partial(_dil_cache_kernel, rate=rate, t_len=t_len), grid=(bsz,),
        in_specs=[col(q_blk), col(k_blk), col(v_blk), blk5(LANES), blk5(win)],
        out_specs=[tok, tok, blk5(win)],
        out_shape=[out, out, jax.ShapeDtypeStruct(cache_t.shape, _F32)],
        compiler_params=_params(("parallel",)),
        name="dil_cache",
    )(z_s3, z_s3, z_s3, tail, cache_t)


def _swa_sample_kernel(sink_ref, z_ref, newt_ref, cache_ref, o_ref, out_ref, *, t_len):
    grp = H_C // KV_C
    tiles = grp // 2
    rows = t_len * grp
    kw = KV_C * DH
    q_w = H_C * DH
    r_t = lax.broadcasted_iota(jnp.int32, (rows, WIN_C), 0) % t_len
    key_i = lax.broadcasted_iota(jnp.int32, (rows, WIN_C), 1)
    ok_c = key_i > r_t
    n_t = lax.broadcasted_iota(jnp.int32, (rows, t_len), 0) % t_len
    n_j = lax.broadcasted_iota(jnp.int32, (rows, t_len), 1)
    ok_n = n_j <= n_t
    r_h = lax.broadcasted_iota(jnp.int32, (rows, 1), 0) // t_len
    low_n = lax.broadcasted_iota(jnp.int32, (t_len, LANES), 1) < DH

    def dup(x, j):
        sw = pltpu.roll(x, DH, axis=1)
        return (jnp.where(low_n, x, sw) if j == 0 else jnp.where(low_n, sw, x)).astype(_BF16)

    k_n, v_n = z_ref[0, :, q_w:q_w + kw], z_ref[0, :, q_w + kw:q_w + 2 * kw]
    for kv in range(2):
        old = cache_ref[0, kv].reshape(kw, WIN_C)
        new_tail = newt_ref[0, kv].reshape(kw, LANES)
        out_ref[0, kv] = _roll_window(old, new_tail, t_len).reshape(KV_C, DH, WIN_C)
    for j in range(KV_C):
        k_old = cache_ref[0, 0, j].astype(_BF16)
        v_old = cache_ref[0, 1, j].astype(_BF16)
        k_dup = jnp.concatenate([k_old, k_old], axis=0)
        v_dup = jnp.concatenate([v_old, v_old], axis=0)
        parts = []
        for a in range(tiles):
            c = (j * tiles + a) * LANES
            q2 = z_ref[0, :, c:c + LANES] * (DH ** -0.5)
            parts.append(jnp.where(low_n, q2, 0.0).astype(_BF16))
            parts.append(jnp.where(low_n, 0.0, q2).astype(_BF16))
        q = jnp.concatenate(parts, axis=0)
        sk = jnp.zeros((rows, 1), _F32)
        for g in range(grp):
            sk = jnp.where(r_h == g, sink_ref[j * grp + g], sk)
        s_c = jnp.where(ok_c, _dot(q, k_dup), NEG_INF)
        s_n = jnp.where(ok_n, _nt_dot(q, dup(k_n, j)), NEG_INF)
        m = jnp.maximum(jnp.maximum(jnp.max(s_c, axis=1, keepdims=True),
                                    jnp.max(s_n, axis=1, keepdims=True)), sk)
        p_c = jnp.exp(s_c - m)
        p_n = jnp.exp(s_n - m)
        den = (jnp.sum(p_c, axis=1, keepdims=True) + jnp.sum(p_n, axis=1, keepdims=True)
               + jnp.exp(sk - m))
        o = (_nt_dot(p_c.astype(_BF16), v_dup)
             + _dot(p_n.astype(_BF16), dup(v_n, j))) / den
        for a in range(tiles):
            c = (j * tiles + a) * LANES
            even = o[(2 * a) * t_len:(2 * a + 1) * t_len, :]
            odd = o[(2 * a + 1) * t_len:(2 * a + 2) * t_len, :]
            o_ref[0, :, c:c + LANES] = jnp.where(low_n, even, odd)


def _swa_cache(zc_s3, cache_t, sinks, *, t_len):
    dbs = cache_t.shape[0]
    kw = KV_C * DH
    q_w = H_C * DH
    tail = jnp.stack([_time_minor_tail(zc_s3[:, :, q_w:q_w + kw], t_len),
                      _time_minor_tail(zc_s3[:, :, q_w + kw:q_w + 2 * kw], t_len)], axis=1)
    blk5 = lambda last: pl.BlockSpec((1, 2, KV_C, DH, last), lambda b: (b, 0, 0, 0, 0))
    return pl.pallas_call(
        functools.partial(_swa_sample_kernel, t_len=t_len), grid=(dbs,),
        in_specs=[pl.BlockSpec(memory_space=pltpu.SMEM),
                  pl.BlockSpec((1, t_len, N_C), lambda b: (b, 0, 0)),
                  blk5(LANES), blk5(WIN_C)],
        out_specs=[pl.BlockSpec((1, t_len, q_w), lambda b: (b, 0, 0)), blk5(WIN_C)],
        out_shape=[jax.ShapeDtypeStruct((dbs, t_len, q_w), _F32),
                   jax.ShapeDtypeStruct(cache_t.shape, _F32)],
        compiler_params=_params(("parallel",)),
        name="swa_cache",
    )(sinks, zc_s3, tail, cache_t)


def _rope_flags(n, lo, hi):
    c = jnp.arange(n // LANES, dtype=jnp.int32) * LANES
    return ((c >= lo) & (c < hi)).astype(jnp.int32)


def kernel(x_prompt, x_sample, state_mlstm_C, state_mlstm_n, state_mlstm_m, cache_dil1_kv, cache_dil2_kv, cache_dil3_kv, cache_swa_kv, w_in_ab, b_gate_ab, w_out_ab, w_in_c, sinks_c, w_out_c, ln1_g, ln1_b, ln2_g, ln2_b, w_up, w_down):
    bsz, s_len, _ = x_prompt.shape
    dbs, t_len, _ = x_sample.shape
    tp = bsz * s_len
    ts = dbs * t_len
    xp = x_prompt.reshape(tp, D_MODEL)
    xs = x_sample.reshape(ts, D_MODEL)
    pos_p = jnp.arange(s_len, dtype=jnp.int32)
    pos_s = PAST_LEN + jnp.arange(t_len, dtype=jnp.int32)
    tab_p = _rope_tables(pos_p)
    tab_s = tuple(jnp.tile(t, (ts // t_len, 1)) for t in _rope_tables(pos_s))
    row2 = lambda v: v.reshape(1, -1)

    w_in = w_in_ab[0]
    gate_lo = 4 * W_A
    w_main = jnp.concatenate([w_in[:, :gate_lo], w_in[:, gate_lo + 2 * H_A:]], axis=1).astype(_BF16)
    w_gate = jnp.pad(w_in[:, gate_lo:gate_lo + 2 * H_A], ((0, 0), (0, LANES - 2 * H_A))).astype(_BF16)
    bias = jnp.pad(b_gate_ab[0], (0, LANES - 2 * H_A)).reshape(1, LANES)
    q_lo = 4 * W_A
    k_lo = q_lo + N_DIL * W_B
    v_lo = k_lo + N_DIL * W_B
    flags_ab = _rope_flags(N_AB, q_lo, v_lo)
    w_out0 = w_out_ab[0].astype(_BF16)

    z_p, gates_p = _project(xp, w_main, tab_p, flags_ab, w_gate, tm=1024, tn=512)
    z_s, gates_s = _project(xs, w_main, tab_s, flags_ab, w_gate, tm=ts, tn=512)

    zeros_c = jnp.zeros((bsz, H_A, DH_A, DH_A), _F32)
    zeros_n = jnp.zeros((bsz, H_A, DH_A), _F32)
    zeros_m = jnp.zeros((bsz, 1, H_A), _F32)
    ga_p, mc_p, mn_p, mm_p = _mlstm(z_p, gates_p, bias, zeros_c, zeros_n, zeros_m,
                                    n_seq=bsz, seq_len=s_len, chunk=256, nb=1)
    ga_s, mc_s, mn_s, mm_s = _mlstm(z_s, gates_s, bias, state_mlstm_C[0], state_mlstm_n[0],
                                    state_mlstm_m[0].reshape(dbs, 1, H_A),
                                    n_seq=dbs, seq_len=t_len, chunk=t_len, nb=8)

    dil_caches = (cache_dil1_kv, cache_dil2_kv, cache_dil3_kv)
    outs_p, lses_p, outs_s, lses_s, dil_kv_p, dil_kv_s = [], [], [], [], [], []
    to_time_minor = lambda c: jnp.transpose(c, (0, 2, 3, 4, 1))
    to_time_major = lambda c: jnp.transpose(c, (0, 4, 1, 2, 3))[None]
    z_s3 = z_s.reshape(dbs, t_len, N_AB)
    z_p3 = z_p.reshape(bsz, s_len, N_AB)
    for g in range(N_DIL):
        rate, win = DIL_RATES[g], DIL_WINDOWS[g]
        qc, kc, vc = q_lo + g * W_B, k_lo + g * W_B, v_lo + g * W_B
        o, l = _band_attention(z_p, n_seq=bsz, seq_len=s_len, rate=rate, qcol=qc, kcol=kc, vcol=vc,
                               qw=W_B, kw=W_B, max_dist=win // rate, with_lse=True)
        outs_p.append(o)
        lses_p.append(l)
        keep = min(win, s_len)
        dil_kv_p.append(jnp.concatenate([z_p3[:, s_len - keep:, kc:kc + W_B],
                                         z_p3[:, s_len - keep:, vc:vc + W_B]], axis=-1)
                        .reshape(1, bsz, keep, 2, H_B, DH))
        o, l, rolled = _dil_cache(z_s3, to_time_minor(dil_caches[g][0]), g=g, rate=rate, t_len=t_len)
        outs_s.append(o.reshape(ts, W_B))
        lses_s.append(l.reshape(ts, W_B))
        dil_kv_s.append(to_time_major(rolled))

    xp = _outln([ga_p] + outs_p + lses_p, xp, w_out0, row2(ln1_g[0]), row2(ln1_b[0]), tm=512)
    xs = _outln([ga_s] + outs_s + lses_s, xs, w_out0, row2(ln1_g[0]), row2(ln1_b[0]), tm=512)
    wu0, wd0 = w_up[0].astype(_BF16), w_down[0].astype(_BF16)
    xp = _mlp(xp, wu0, wd0, row2(ln2_g[0]), row2(ln2_b[0]), tm=512, ff_chunk=1024)
    xs = _mlp(xs, wu0, wd0, row2(ln2_g[0]), row2(ln2_b[0]), tm=512, ff_chunk=1024)

    w_c = w_in_c[0].astype(_BF16)
    kc1 = H_C * DH
    vc1 = kc1 + KV_C * DH
    flags_c = _rope_flags(N_C, 0, vc1)
    zc_p = _project(xp, w_c, tab_p, flags_c, None, tm=1024, tn=640)
    zc_s = _project(xs, w_c, tab_s, flags_c, None, tm=ts, tn=640)
    (o_p,) = _band_attention(zc_p, n_seq=bsz, seq_len=s_len, rate=1, qcol=0, kcol=kc1, vcol=vc1,
                             qw=H_C * DH, kw=KV_C * DH, max_dist=WIN_C - 1, sinks=sinks_c[0],
                             with_lse=False)
    keep = min(WIN_C, s_len)
    swa_kv_p = zc_p.reshape(bsz, s_len, N_C)[:, s_len - keep:, kc1:].reshape(1, bsz, keep, 2, KV_C, DH)

    o_s, swa_rolled = _swa_cache(zc_s.reshape(dbs, t_len, N_C), to_time_minor(cache_swa_kv[0]),
                                 sinks_c[0], t_len=t_len)
    o_s = o_s.reshape(ts, H_C * DH)
    swa_kv_s = to_time_major(swa_rolled)

    w_out1 = w_out_c[0].astype(_BF16)
    xp = _outln([o_p], xp, w_out1, row2(ln1_g[1]), row2(ln1_b[1]), tm=512)
    xs = _outln([o_s], xs, w_out1, row2(ln1_g[1]), row2(ln1_b[1]), tm=512)
    wu1, wd1 = w_up[1].astype(_BF16), w_down[1].astype(_BF16)
    xp = _mlp(xp, wu1, wd1, row2(ln2_g[1]), row2(ln2_b[1]), tm=512, ff_chunk=1024)
    xs = _mlp(xs, wu1, wd1, row2(ln2_g[1]), row2(ln2_b[1]), tm=512, ff_chunk=1024)

    return (xp.reshape(bsz, s_len, D_MODEL), xs.reshape(dbs, t_len, D_MODEL),
            mc_p[None], mc_s[None], mn_p[None], mn_s[None],
            mm_p.reshape(1, bsz, H_A), mm_s.reshape(1, dbs, H_A),
            dil_kv_p[0], dil_kv_s[0], dil_kv_p[1], dil_kv_s[1], dil_kv_p[2], dil_kv_s[2],
            swa_kv_p, swa_kv_s)
```

```python
import functools

import jax
import jax.numpy as jnp
from jax import lax
from jax.experimental import pallas as pl
from jax.experimental.pallas import tpu as pltpu

LANES = 128
SUBLANES = 8
MXU_COLS = 256
VMEM_LIMIT = 56 * 1024 * 1024

D_MODEL = 1024
DH = 64
ROT_DIM = DH // 4
ROPE_THETA = 500000.0
PAST_LEN = 8192
BLK = 128
H_A = 4
DH_A = 128
W_A = H_A * DH_A
N_DIL = 3
DIL_WINDOWS = (128, 512, 2048)
DIL_RATES = (1, 4, 16)
H_B = 8
W_B = H_B * DH
H_C = 16
KV_C = 2
WIN_C = 128
D_FF = 4 * D_MODEL
DEPTH = 2
ALPHA = (2.0 * DEPTH) ** 0.25
LN_EPS = 1e-5
N_AB = 4 * W_A + 3 * N_DIL * W_B
N_C = (H_C + 2 * KV_C) * DH
NEG_INF = float("-inf")

_F32 = jnp.float32
_BF16 = jnp.bfloat16


def _params(sem):
    return pltpu.CompilerParams(dimension_semantics=sem, vmem_limit_bytes=VMEM_LIMIT)


def _nt_dot(a, b):
    return lax.dot_general(a, b, (((1,), (1,)), ((), ())), preferred_element_type=_F32)


def _tn_dot(a, b):
    return lax.dot_general(a, b, (((0,), (0,)), ((), ())), preferred_element_type=_F32)


def _dot(a, b):
    return jnp.dot(a, b, preferred_element_type=_F32)


def _log_sigmoid(x):
    return -(jnp.maximum(-x, 0.0) + jnp.log1p(jnp.exp(-jnp.abs(x))))


def _layer_norm(y, g, b):
    mu = jnp.mean(y, axis=-1, keepdims=True)
    yc = y - mu
    var = jnp.mean(yc * yc, axis=-1, keepdims=True)
    return yc * lax.rsqrt(var + LN_EPS) * g + b


def _rope_tables(pos):
    half = ROT_DIM // 2
    inv = ROPE_THETA ** (-jnp.arange(half, dtype=_F32) / half)
    ang = pos.astype(_F32)[:, None] * inv[None, :]
    cos = jnp.cos(ang)
    sin = jnp.sin(ang)
    n = pos.shape[0]
    ones = jnp.ones((n, DH - ROT_DIM), _F32)
    zeros = jnp.zeros((n, DH - ROT_DIM), _F32)
    zh = jnp.zeros((n, half), _F32)
    cos_h = jnp.concatenate([cos, cos, ones], axis=1)
    sinm_h = jnp.concatenate([-sin, zh, zeros], axis=1)
    sinp_h = jnp.concatenate([zh, sin, zeros], axis=1)
    tile2 = lambda t: jnp.concatenate([t, t], axis=1)
    return tile2(cos_h), tile2(sinm_h), tile2(sinp_h)


def _proj_kernel(x_ref, w_ref, cos_ref, sinm_ref, sinp_ref, o_ref, *, rope_chunks):
    out = o_ref.at[0, 0]
    xb = x_ref[...].astype(_BF16)
    n = w_ref.shape[1]
    half = ROT_DIM // 2
    for c0 in range(0, n, MXU_COLS):
        width = min(MXU_COLS, n - c0)
        acc = _dot(xb, w_ref[:, c0:c0 + width])
        for cc in range(width // LANES):
            sub = acc[:, cc * LANES:(cc + 1) * LANES]
            if rope_chunks[c0 // LANES + cc]:
                sub = (sub * cos_ref[...]
                       + pltpu.roll(sub, LANES - half, axis=1) * sinm_ref[...]
                       + pltpu.roll(sub, half, axis=1) * sinp_ref[...])
            out[:, c0 + cc * LANES:c0 + (cc + 1) * LANES] = sub


def _project(x, w, tables, rope_chunks, *, n_seq, seq_len, rate, tm):
    d = x.shape[1]
    n = w.shape[1]
    assert len(rope_chunks) * LANES == n
    length = seq_len // rate
    tm = min(tm, length)
    nj = length // tm
    xv = x.reshape(n_seq * length, rate * d)
    tabs = [t.reshape(length, rate * LANES) for t in tables]
    row_tab = pl.BlockSpec((tm, LANES), lambda b, r, j: (j, r))
    return pl.pallas_call(
        functools.partial(_proj_kernel, rope_chunks=tuple(rope_chunks)),
        grid=(n_seq, rate, nj),
        in_specs=[pl.BlockSpec((tm, d), lambda b, r, j: (b * nj + j, r)),
                  _const_spec(w.shape), row_tab, row_tab, row_tab],
        out_specs=pl.BlockSpec((1, 1, tm, n), lambda b, r, j: (b, r, j, 0)),
        out_shape=jax.ShapeDtypeStruct((n_seq, rate, length, n), _F32),
        compiler_params=_params(("parallel", "parallel", "parallel")),
        name="proj",
    )(xv, w, *tabs)


def _mlstm_kernel(q_ref, k_ref, v_ref, oa_ref, g_ref, bias_ref, c0_ref, n0_ref, m0_ref,
                  ga_ref, c_ref, n_ref, m_ref, *, chunk, nb):
    c_idx = pl.program_id(1)

    @pl.when(c_idx == 0)
    def _():
        c_ref[...] = c0_ref[...]
        n_ref[...] = n0_ref[...]
        m_ref[...] = m0_ref[...]

    row = lax.broadcasted_iota(jnp.int32, (chunk, chunk), 0)
    col = lax.broadcasted_iota(jnp.int32, (chunk, chunk), 1)
    eye = row == col
    tril = col <= row

    def to_row(colvec):
        return jnp.sum(jnp.where(eye, colvec, 0.0), axis=0, keepdims=True)

    for bi in range(nb):
        r0 = bi * chunk
        gates = g_ref[r0:r0 + chunk, :] + bias_ref[...]
        for h in range(H_A):
            lo = h * DH_A
            q = q_ref[r0:r0 + chunk, lo:lo + DH_A]
            k = k_ref[r0:r0 + chunk, lo:lo + DH_A] * (DH_A ** -0.5)
            v = v_ref[r0:r0 + chunk, lo:lo + DH_A]
            oa = oa_ref[r0:r0 + chunk, lo:lo + DH_A]
            c_prev = c_ref[bi, h]
            n_prev = n_ref[bi, h:h + 1, :]
            m_prev = m_ref[bi, :, h:h + 1]
            li_col = gates[:, h:h + 1]
            lf_col = _log_sigmoid(gates[:, H_A + h:H_A + h + 1])
            li_row = to_row(li_col)
            lf_row = to_row(lf_col)
            b_col = jnp.sum(jnp.where(tril, lf_row, 0.0), axis=1, keepdims=True)
            b_row = to_row(b_col)
            dmat = jnp.where(tril, b_col - b_row + li_row, NEG_INF)
            a_col = b_col + m_prev
            mt = jnp.maximum(a_col, jnp.max(dmat, axis=1, keepdims=True))
            wts = jnp.exp(dmat - mt)
            inter = jnp.exp(a_col - mt)
            qb = q.astype(_BF16)
            kb = k.astype(_BF16)
            vb = v.astype(_BF16)
            sqk = _nt_dot(qb, kb) * wts
            num = inter * _dot(qb, c_prev.astype(_BF16)) + _dot(sqk.astype(_BF16), vb)
            nq = (inter * jnp.sum(q * n_prev, axis=1, keepdims=True)
                  + jnp.sum(sqk, axis=1, keepdims=True))
            hid = num / jnp.maximum(jnp.abs(nq), jnp.exp(-mt))
            ga_ref[r0:r0 + chunk, lo:lo + DH_A] = hid * jax.nn.sigmoid(oa)
            b_last = b_col[chunk - 1:chunk, :]
            mt_last = mt[chunk - 1:chunk, :]
            w_last = jnp.exp(b_last - b_col + li_col - mt_last)
            inter_last = inter[chunk - 1:chunk, :]
            kw = k * w_last
            c_ref[bi, h] = inter_last * c_prev + _tn_dot(kw.astype(_BF16), vb)
            n_ref[bi, h:h + 1, :] = inter_last * n_prev + jnp.sum(kw, axis=0, keepdims=True)
            m_ref[bi, :, h:h + 1] = mt_last


def _mlstm(z, bias, c0, n0, m0, *, n_seq, seq_len, chunk, nb):
    t = z.shape[0]
    nc = seq_len // chunk
    rows = nb * chunk
    col = lambda cb: pl.BlockSpec((rows, W_A), lambda b, c: (b * nc + c, cb))
    state_c = pl.BlockSpec((nb, H_A, DH_A, DH_A), lambda b, c: (b, 0, 0, 0))
    state_n = pl.BlockSpec((nb, H_A, DH_A), lambda b, c: (b, 0, 0))
    state_m = pl.BlockSpec((nb, 1, H_A), lambda b, c: (b, 0, 0))
    return pl.pallas_call(
        functools.partial(_mlstm_kernel, chunk=chunk, nb=nb),
        grid=(n_seq // nb, nc),
        in_specs=[col(0), col(1), col(2), col(3),
                  pl.BlockSpec((rows, LANES), lambda b, c: (b * nc + c, 4 * W_A // LANES)),
                  pl.BlockSpec((1, LANES), lambda b, c: (0, 0)),
                  state_c, state_n, state_m],
        out_specs=[pl.BlockSpec((rows, W_A), lambda b, c: (b * nc + c, 0)),
                   state_c, state_n, state_m],
        out_shape=[jax.ShapeDtypeStruct((t, W_A), _F32),
                   jax.ShapeDtypeStruct((n_seq, H_A, DH_A, DH_A), _F32),
                   jax.ShapeDtypeStruct((n_seq, H_A, DH_A), _F32),
                   jax.ShapeDtypeStruct((n_seq, 1, H_A), _F32)],
        compiler_params=_params(("parallel", "arbitrary")),
        name="mlstm",
    )(z, z, z, z, z, bias, c0, n0, m0)


def _band_kernel(*refs, max_dist, n_qtiles, kv_shared, with_sinks, with_lse):
    idx = 0
    if with_sinks:
        sink_ref = refs[0]
        idx = 1
    q_ref, k_ref, v_ref = refs[idx:idx + 3]
    idx += 3
    o_ref = refs[idx]
    idx += 1
    if with_lse:
        l_ref = refs[idx]
        idx += 1
    kprev_ref, vprev_ref = refs[idx:idx + 2]
    j = pl.program_id(2)
    n_kv = kprev_ref.shape[0]

    @pl.when(j == 0)
    def _():
        kprev_ref[...] = jnp.zeros_like(kprev_ref)
        vprev_ref[...] = jnp.zeros_like(vprev_ref)

    lane = lax.broadcasted_iota(jnp.int32, (BLK, LANES), 1)
    low = lane < DH
    if kv_shared:
        k_in, v_in = k_ref[0, 0], v_ref[0, 0]
        k_sw = pltpu.roll(k_in, DH, axis=1)
        v_sw = pltpu.roll(v_in, DH, axis=1)
        k_tiles = [jnp.where(low, k_in, k_sw), jnp.where(low, k_sw, k_in)]
        v_tiles = [jnp.where(low, v_in, v_sw), jnp.where(low, v_sw, v_in)]
    else:
        k_tiles = [k_ref[0, 0, :, t * LANES:(t + 1) * LANES] for t in range(n_kv)]
        v_tiles = [v_ref[0, 0, :, t * LANES:(t + 1) * LANES] for t in range(n_kv)]
    k_tiles = [t.astype(_BF16) for t in k_tiles]
    v_tiles = [t.astype(_BF16) for t in v_tiles]

    qi = lax.broadcasted_iota(jnp.int32, (BLK, 2 * BLK), 0) + BLK
    ki = lax.broadcasted_iota(jnp.int32, (BLK, 2 * BLK), 1)
    dist = qi - ki
    first_key = jnp.where(j > 0, 0, BLK)
    valid = (dist >= 0) & (dist <= max_dist) & (ki >= first_key)

    q_all = q_ref[0, 0] * (DH ** -0.5)
    tiles_per_kv = n_qtiles // n_kv
    for kt in range(n_kv):
        k2 = jnp.concatenate([kprev_ref[kt], k_tiles[kt]], axis=0)
        v2 = jnp.concatenate([vprev_ref[kt], v_tiles[kt]], axis=0)
        parts = []
        for p in range(kt * tiles_per_kv, (kt + 1) * tiles_per_kv):
            q2 = q_all[:, p * LANES:(p + 1) * LANES]
            parts.append(jnp.where(low, q2, 0.0).astype(_BF16))
            parts.append(jnp.where(low, 0.0, q2).astype(_BF16))
        s_all = _nt_dot(jnp.concatenate(parts, axis=0), k2)
        probs, dens, lses = [], [], []
        for hh in range(2 * tiles_per_kv):
            s = jnp.where(valid, s_all[hh * BLK:(hh + 1) * BLK, :], NEG_INF)
            m = jnp.max(s, axis=1, keepdims=True)
            if with_sinks:
                sk = sink_ref[2 * kt * tiles_per_kv + hh]
                m = jnp.maximum(m, sk)
            p_exp = jnp.exp(s - m)
            den = jnp.sum(p_exp, axis=1, keepdims=True)
            if with_sinks:
                den = den + jnp.exp(sk - m)
            probs.append(p_exp.astype(_BF16))
            dens.append(den)
            lses.append(m + jnp.log(den))
        o_all = _dot(jnp.concatenate(probs, axis=0), v2)
        for pp in range(tiles_per_kv):
            p = kt * tiles_per_kv + pp
            e, o = 2 * pp, 2 * pp + 1
            o_even = o_all[e * BLK:(e + 1) * BLK, :] / dens[e]
            o_odd = o_all[o * BLK:(o + 1) * BLK, :] / dens[o]
            o_ref[0, :, p * LANES:(p + 1) * LANES] = jnp.where(low, o_even, o_odd)
            if with_lse:
                l_ref[0, :, p * LANES:(p + 1) * LANES] = jnp.where(low, lses[e], lses[o])

    for kt in range(n_kv):
        kprev_ref[kt] = k_tiles[kt]
        vprev_ref[kt] = v_tiles[kt]


def _band_attention(z4, *, qcol, kcol, vcol, qw, kw, max_dist, sinks=None, with_lse):
    n_seq, rate, length, _ = z4.shape
    seq_len = length * rate
    nblk = length // BLK
    kv_shared = kw == LANES and qw > LANES
    n_kv = 2 if kv_shared else kw // LANES

    def spec(width, coff):
        assert coff % width == 0
        return pl.BlockSpec((1, 1, BLK, width), lambda b, r, j: (b, r, j, coff // width))

    out_spec = pl.BlockSpec((1, BLK, qw), lambda b, r, j: (b, j, r))
    out_shape = [jax.ShapeDtypeStruct((n_seq, length, rate * qw), _F32)]
    out_specs = [out_spec]
    if with_lse:
        out_shape.append(jax.ShapeDtypeStruct((n_seq, length, rate * qw), _F32))
        out_specs.append(out_spec)
    in_specs = [spec(qw, qcol), spec(kw, kcol), spec(kw, vcol)]
    args = [z4, z4, z4]
    if sinks is not None:
        in_specs = [pl.BlockSpec(memory_space=pltpu.SMEM)] + in_specs
        args = [sinks] + args
    res = pl.pallas_call(
        functools.partial(_band_kernel, max_dist=max_dist, n_qtiles=qw // LANES,
                          kv_shared=kv_shared, with_sinks=sinks is not None, with_lse=with_lse),
        grid=(n_seq, rate, nblk),
        in_specs=in_specs, out_specs=out_specs, out_shape=out_shape,
        scratch_shapes=[pltpu.VMEM((n_kv, BLK, LANES), _BF16),
                        pltpu.VMEM((n_kv, BLK, LANES), _BF16)],
        compiler_params=_params(("parallel", "parallel", "arbitrary")),
        name="band_attn",
    )(*args)
    return [r.reshape(n_seq * seq_len, qw) for r in res]


def _outln_ab_kernel(ga_ref, o1_ref, o2_ref, o3_ref, l1_ref, l2_ref, l3_ref, x_ref, w_ref,
                     g_ref, b_ref, y_ref):
    l1, l2, l3 = l1_ref[...], l2_ref[...], l3_ref[...]
    lmax = jnp.maximum(jnp.maximum(l1, l2), l3)
    e1, e2, e3 = jnp.exp(l1 - lmax), jnp.exp(l2 - lmax), jnp.exp(l3 - lmax)
    ob = (e1 * o1_ref[...] + e2 * o2_ref[...] + e3 * o3_ref[...]) / (e1 + e2 + e3)
    mix = (_dot(ga_ref[...].astype(_BF16), w_ref[0:W_A, :])
           + _dot(ob.astype(_BF16), w_ref[W_A:W_A + W_B, :]))
    y_ref[...] = _layer_norm(ALPHA * x_ref[...] + mix, g_ref[...], b_ref[...])


def _outln_c_kernel(o_ref, x_ref, w_ref, g_ref, b_ref, y_ref):
    mix = _dot(o_ref[...].astype(_BF16), w_ref[...])
    y_ref[...] = _layer_norm(ALPHA * x_ref[...] + mix, g_ref[...], b_ref[...])


def _const_spec(shape):
    return pl.BlockSpec(shape, lambda *_: (0,) * len(shape), pipeline_mode=pl.Buffered(1))


def _outln(parts, x, w, g, b, tm):
    t = x.shape[0]
    tm = min(tm, t)
    row = lambda width: pl.BlockSpec((tm, width), lambda i: (i, 0))
    kern = _outln_ab_kernel if len(parts) > 1 else _outln_c_kernel
    return pl.pallas_call(
        kern, grid=(t // tm,),
        in_specs=[row(p.shape[1]) for p in parts]
        + [row(D_MODEL), _const_spec(w.shape), _const_spec((1, D_MODEL)), _const_spec((1, D_MODEL))],
        out_specs=row(D_MODEL),
        out_shape=jax.ShapeDtypeStruct((t, D_MODEL), _F32),
        compiler_params=_params(("parallel",)),
        name="outproj_ln",
    )(*parts, x, w, g, b)


def _mlp_kernel(x_ref, wu_ref, wd_ref, g_ref, b_ref, y_ref, *, ff_chunk):
    x = x_ref[...]
    xb = x.astype(_BF16)
    acc = ALPHA * x
    for c in range(D_FF // ff_chunk):
        hid = _dot(xb, wu_ref[:, c * ff_chunk:(c + 1) * ff_chunk])
        hid = jnp.square(jnp.maximum(hid, 0.0)).astype(_BF16)
        acc = acc + _dot(hid, wd_ref[c * ff_chunk:(c + 1) * ff_chunk, :])
    y_ref[...] = _layer_norm(acc, g_ref[...], b_ref[...])


def _mlp(x, wu, wd, g, b, tm, ff_chunk):
    t = x.shape[0]
    tm = min(tm, t)
    row = pl.BlockSpec((tm, D_MODEL), lambda i: (i, 0))
    return pl.pallas_call(
        functools.partial(_mlp_kernel, ff_chunk=ff_chunk), grid=(t // tm,),
        in_specs=[row, _const_spec(wu.shape), _const_spec(wd.shape),
                  _const_spec((1, D_MODEL)), _const_spec((1, D_MODEL))],
        out_specs=row,
        out_shape=jax.ShapeDtypeStruct((t, D_MODEL), _F32),
        compiler_params=_params(("parallel",)),
        name="mlp_ln",
    )(x, wu, wd, g, b)


def _roll_window(old, new_tail, t_len):
    win = old.shape[1]
    rolled = pltpu.roll(old, win - t_len, axis=1)
    tail_lane = lax.broadcasted_iota(jnp.int32, new_tail.shape, 1) >= LANES - t_len
    tail = jnp.where(tail_lane, new_tail, rolled[:, win - LANES:])
    if win == LANES:
        return tail
    return jnp.concatenate([rolled[:, :win - LANES], tail], axis=1)


def _dil_cache_kernel(q_ref, k_ref, v_ref, newt_ref, cache_ref, o_ref, l_ref, out_ref, *,
                      rate, t_len):
    win = cache_ref.shape[-1]
    rows = 2 * t_len
    key_i = lax.broadcasted_iota(jnp.int32, (rows, win), 1)
    row_t = lax.broadcasted_iota(jnp.int32, (rows, win), 0) % t_len
    ok_c = (key_i >= row_t) & (((key_i - row_t) & (rate - 1)) == 0)
    n_j = lax.broadcasted_iota(jnp.int32, (rows, t_len), 1)
    n_t = lax.broadcasted_iota(jnp.int32, (rows, t_len), 0) % t_len
    ok_n = (n_j <= n_t) & (((n_t - n_j) & (rate - 1)) == 0)
    low = lax.broadcasted_iota(jnp.int32, (t_len, LANES), 1) < DH
    for a in range(H_B // 2):
        c = a * LANES
        q2 = q_ref[0, :, c:c + LANES] * (DH ** -0.5)
        lhs = jnp.concatenate([jnp.where(low, q2, 0.0), jnp.where(low, 0.0, q2)], axis=0).astype(_BF16)
        k_old = cache_ref[0, 0, 2 * a:2 * a + 2].reshape(2 * DH, win)
        v_old = cache_ref[0, 1, 2 * a:2 * a + 2].reshape(2 * DH, win)
        k_new = k_ref[0, :, c:c + LANES]
        v_new = v_ref[0, :, c:c + LANES]
        s_c = jnp.where(ok_c, _dot(lhs, k_old.astype(_BF16)), NEG_INF)
        s_n = jnp.where(ok_n, _nt_dot(lhs, k_new.astype(_BF16)), NEG_INF)
        m = jnp.maximum(jnp.max(s_c, axis=1, keepdims=True), jnp.max(s_n, axis=1, keepdims=True))
        p_c = jnp.exp(s_c - m)
        p_n = jnp.exp(s_n - m)
        den = jnp.sum(p_c, axis=1, keepdims=True) + jnp.sum(p_n, axis=1, keepdims=True)
        o = (_nt_dot(p_c.astype(_BF16), v_old.astype(_BF16))
             + _dot(p_n.astype(_BF16), v_new.astype(_BF16))) / den
        lse = m + jnp.log(den)
        o_ref[0, :, c:c + LANES] = jnp.where(low, o[0:t_len], o[t_len:rows])
        l_ref[0, :, c:c + LANES] = jnp.where(low, lse[0:t_len], lse[t_len:rows])
        for kv, old in ((0, k_old), (1, v_old)):
            new_tail = newt_ref[0, kv, 2 * a:2 * a + 2].reshape(2 * DH, LANES)
            out_ref[0, kv, 2 * a:2 * a + 2] = _roll_window(old, new_tail, t_len).reshape(2, DH, win)


def _time_minor_tail(new, t_len):
    bsz = new.shape[0]
    heads = new.shape[2] // DH
    t = jnp.transpose(new.reshape(bsz, t_len, heads, DH), (0, 2, 3, 1))
    return jnp.pad(t, ((0, 0), (0, 0), (0, 0), (LANES - t_len, 0)))


def _dil_cache(z_s3, cache_t, *, g, rate, t_len):
    bsz, _, _, _, win = cache_t.shape
    q_blk, k_blk, v_blk = 3 * g, 3 * g + 1, 3 * g + 2
    tail = jnp.stack([_time_minor_tail(z_s3[:, :, k_blk * W_B:(k_blk + 1) * W_B], t_len),
                      _time_minor_tail(z_s3[:, :, v_blk * W_B:(v_blk + 1) * W_B], t_len)], axis=1)
    col = lambda cb: pl.BlockSpec((1, t_len, W_B), lambda b: (b, 0, cb))
    tok = pl.BlockSpec((1, t_len, W_B), lambda b: (b, 0, 0))
    blk5 = lambda last: pl.BlockSpec((1, 2, H_B, DH, last), lambda b: (b, 0, 0, 0, 0))
    out = jax.ShapeDtypeStruct((bsz, t_len, W_B), _F32)
    return pl.pallas_call(
        functools.partial(_dil_cache_kernel, rate=rate, t_len=t_len), grid=(bsz,),
        in_specs=[col(q_blk), col(k_blk), col(v_blk), blk5(LANES), blk5(win)],
        out_specs=[tok, tok, blk5(win)],
        out_shape=[out, out, jax.ShapeDtypeStruct(cache_t.shape, _F32)],
        compiler_params=_params(("parallel",)),
        name="dil_cache",
    )(z_s3, z_s3, z_s3, tail, cache_t)


def _swa_sample_kernel(sink_ref, z_ref, newt_ref, cache_ref, o_ref, out_ref, *, t_len):
    grp = H_C // KV_C
    tiles = grp // 2
    rows = t_len * grp
    kw = KV_C * DH
    q_w = H_C * DH
    r_t = lax.broadcasted_iota(jnp.int32, (rows, WIN_C), 0) % t_len
    key_i = lax.broadcasted_iota(jnp.int32, (rows, WIN_C), 1)
    ok_c = key_i > r_t
    n_t = lax.broadcasted_iota(jnp.int32, (rows, t_len), 0) % t_len
    n_j = lax.broadcasted_iota(jnp.int32, (rows, t_len), 1)
    ok_n = n_j <= n_t
    r_h = lax.broadcasted_iota(jnp.int32, (rows, 1), 0) // t_len
    low_n = lax.broadcasted_iota(jnp.int32, (t_len, LANES), 1) < DH

    def dup(x, j):
        sw = pltpu.roll(x, DH, axis=1)
        return (jnp.where(low_n, x, sw) if j == 0 else jnp.where(low_n, sw, x)).astype(_BF16)

    k_n, v_n = z_ref[0, :, q_w:q_w + kw], z_ref[0, :, q_w + kw:q_w + 2 * kw]
    for kv in range(2):
        old = cache_ref[0, kv].reshape(kw, WIN_C)
        new_tail = newt_ref[0, kv].reshape(kw, LANES)
        out_ref[0, kv] = _roll_window(old, new_tail, t_len).reshape(KV_C, DH, WIN_C)
    for j in range(KV_C):
        k_old = cache_ref[0, 0, j].astype(_BF16)
        v_old = cache_ref[0, 1, j].astype(_BF16)
        k_dup = jnp.concatenate([k_old, k_old], axis=0)
        v_dup = jnp.concatenate([v_old, v_old], axis=0)
        parts = []
        for a in range(tiles):
            c = (j * tiles + a) * LANES
            q2 = z_ref[0, :, c:c + LANES] * (DH ** -0.5)
            parts.append(jnp.where(low_n, q2, 0.0).astype(_BF16))
            parts.append(jnp.where(low_n, 0.0, q2).astype(_BF16))
        q = jnp.concatenate(parts, axis=0)
        sk = jnp.zeros((rows, 1), _F32)
        for g in range(grp):
            sk = jnp.where(r_h == g, sink_ref[j * grp + g], sk)
        s_c = jnp.where(ok_c, _dot(q, k_dup), NEG_INF)
        s_n = jnp.where(ok_n, _nt_dot(q, dup(k_n, j)), NEG_INF)
        m = jnp.maximum(jnp.maximum(jnp.max(s_c, axis=1, keepdims=True),
                                    jnp.max(s_n, axis=1, keepdims=True)), sk)
        p_c = jnp.exp(s_c - m)
        p_n = jnp.exp(s_n - m)
        den = (jnp.sum(p_c, axis=1, keepdims=True) + jnp.sum(p_n, axis=1, keepdims=True)
               + jnp.exp(sk - m))
        o = (_nt_dot(p_c.astype(_BF16), v_dup)
             + _dot(p_n.astype(_BF16), dup(v_n, j))) / den
        for a in range(tiles):
            c = (j * tiles + a) * LANES
            even = o[(2 * a) * t_len:(2 * a + 1) * t_len, :]
            odd = o[(2 * a + 1) * t_len:(2 * a + 2) * t_len, :]
            o_ref[0, :, c:c + LANES] = jnp.where(low_n, even, odd)


def _swa_cache(zc_s3, cache_t, sinks, *, t_len):
    dbs = cache_t.shape[0]
    kw = KV_C * DH
    q_w = H_C * DH
    tail = jnp.stack([_time_minor_tail(zc_s3[:, :, q_w:q_w + kw], t_len),
                      _time_minor_tail(zc_s3[:, :, q_w + kw:q_w + 2 * kw], t_len)], axis=1)
    blk5 = lambda last: pl.BlockSpec((1, 2, KV_C, DH, last), lambda b: (b, 0, 0, 0, 0))
    return pl.pallas_call(
        functools.partial(_swa_sample_kernel, t_len=t_len), grid=(dbs,),
        in_specs=[pl.BlockSpec(memory_space=pltpu.SMEM),
                  pl.BlockSpec((1, t_len, N_C), lambda b: (b, 0, 0)),
                  blk5(LANES), blk5(WIN_C)],
        out_specs=[pl.BlockSpec((1, t_len, q_w), lambda b: (b, 0, 0)), blk5(WIN_C)],
        out_shape=[jax.ShapeDtypeStruct((dbs, t_len, q_w), _F32),
                   jax.ShapeDtypeStruct(cache_t.shape, _F32)],
        compiler_params=_params(("parallel",)),
        name="swa_cache",
    )(sinks, zc_s3, tail, cache_t)


def _rope_chunks(widths_and_flags):
    out = []
    for width, flag in widths_and_flags:
        out += [flag] * (width // LANES)
    return tuple(out)


def kernel(x_prompt, x_sample, state_mlstm_C, state_mlstm_n, state_mlstm_m, cache_dil1_kv, cache_dil2_kv, cache_dil3_kv, cache_swa_kv, w_in_ab, b_gate_ab, w_out_ab, w_in_c, sinks_c, w_out_c, ln1_g, ln1_b, ln2_g, ln2_b, w_up, w_down):
    bsz, s_len, _ = x_prompt.shape
    dbs, t_len, _ = x_sample.shape
    tp = bsz * s_len
    ts = dbs * t_len
    xp = x_prompt.reshape(tp, D_MODEL)
    xs = x_sample.reshape(ts, D_MODEL)
    pos_p = jnp.arange(s_len, dtype=jnp.int32)
    pos_s = PAST_LEN + jnp.arange(t_len, dtype=jnp.int32)
    tab_p = _rope_tables(pos_p)
    tab_s = tuple(jnp.tile(t, (ts // t_len, 1)) for t in _rope_tables(pos_s))
    row2 = lambda v: v.reshape(1, -1)

    w_in = w_in_ab[0]
    gate_lo = 4 * W_A
    q_lo = gate_lo + 2 * H_A
    k_lo = q_lo + N_DIL * W_B
    v_lo = k_lo + N_DIL * W_B
    w_a = jnp.concatenate([w_in[:, :gate_lo],
                           jnp.pad(w_in[:, gate_lo:q_lo], ((0, 0), (0, LANES - 2 * H_A)))],
                          axis=1).astype(_BF16)
    w_grp = [jnp.concatenate([w_in[:, lo + g * W_B:lo + (g + 1) * W_B] for lo in (q_lo, k_lo, v_lo)],
                             axis=1).astype(_BF16) for g in range(N_DIL)]
    rope_a = _rope_chunks([(gate_lo + LANES, False)])
    rope_grp = _rope_chunks([(2 * W_B, True), (W_B, False)])
    bias = jnp.pad(b_gate_ab[0], (0, LANES - 2 * H_A)).reshape(1, LANES)
    w_out0 = w_out_ab[0].astype(_BF16)

    za_p = _project(xp, w_a, tab_p, rope_a, n_seq=bsz, seq_len=s_len, rate=1, tm=1024)
    za_s = _project(xs, w_a, tab_s, rope_a, n_seq=1, seq_len=ts, rate=1, tm=1024)
    zg_s = _project(xs, jnp.concatenate(w_grp, axis=1), tab_s, rope_grp * N_DIL,
                    n_seq=1, seq_len=ts, rate=1, tm=1024)

    zeros_c = jnp.zeros((bsz, H_A, DH_A, DH_A), _F32)
    zeros_n = jnp.zeros((bsz, H_A, DH_A), _F32)
    zeros_m = jnp.zeros((bsz, 1, H_A), _F32)
    ga_p, mc_p, mn_p, mm_p = _mlstm(za_p.reshape(tp, -1), bias, zeros_c, zeros_n, zeros_m,
                                    n_seq=bsz, seq_len=s_len, chunk=256, nb=1)
    ga_s, mc_s, mn_s, mm_s = _mlstm(za_s.reshape(ts, -1), bias, state_mlstm_C[0], state_mlstm_n[0],
                                    state_mlstm_m[0].reshape(dbs, 1, H_A),
                                    n_seq=dbs, seq_len=t_len, chunk=t_len, nb=8)

    dil_caches = (cache_dil1_kv, cache_dil2_kv, cache_dil3_kv)
    outs_p, lses_p, outs_s, lses_s, dil_kv_p, dil_kv_s = [], [], [], [], [], []
    to_time_minor = lambda c: jnp.transpose(c, (0, 2, 3, 4, 1))
    to_time_major = lambda c: jnp.transpose(c, (0, 4, 1, 2, 3))[None]
    z_s3 = zg_s.reshape(dbs, t_len, N_DIL * 3 * W_B)
    for g in range(N_DIL):
        rate, win = DIL_RATES[g], DIL_WINDOWS[g]
        zg_p = _project(xp, w_grp[g], tab_p, rope_grp, n_seq=bsz, seq_len=s_len, rate=rate, tm=1024)
        o, l = _band_attention(zg_p, qcol=0, kcol=W_B, vcol=2 * W_B, qw=W_B, kw=W_B,
                               max_dist=win // rate, with_lse=True)
        outs_p.append(o)
        lses_p.append(l)
        keep = min(win, s_len) // rate
        tail = zg_p[:, :, s_len // rate - keep:, W_B:]
        dil_kv_p.append(jnp.transpose(tail, (0, 2, 1, 3)).reshape(1, bsz, keep * rate, 2, H_B, DH))
        o, l, rolled = _dil_cache(z_s3, to_time_minor(dil_caches[g][0]), g=g, rate=rate, t_len=t_len)
        outs_s.append(o.reshape(ts, W_B))
        lses_s.append(l.reshape(ts, W_B))
        dil_kv_s.append(to_time_major(rolled))

    xp = _outln([ga_p] + outs_p + lses_p, xp, w_out0, row2(ln1_g[0]), row2(ln1_b[0]), tm=512)
    xs = _outln([ga_s] + outs_s + lses_s, xs, w_out0, row2(ln1_g[0]), row2(ln1_b[0]), tm=512)
    wu0, wd0 = w_up[0].astype(_BF16), w_down[0].astype(_BF16)
    xp = _mlp(xp, wu0, wd0, row2(ln2_g[0]), row2(ln2_b[0]), tm=512, ff_chunk=1024)
    xs = _mlp(xs, wu0, wd0, row2(ln2_g[0]), row2(ln2_b[0]), tm=512, ff_chunk=1024)

    w_c = w_in_c[0].astype(_BF16)
    kc1 = H_C * DH
    vc1 = kc1 + KV_C * DH
    rope_c = _rope_chunks([(vc1, True), (KV_C * DH, False)])
    zc_p = _project(xp, w_c, tab_p, rope_c, n_seq=bsz, seq_len=s_len, rate=1, tm=1024)
    zc_s = _project(xs, w_c, tab_s, rope_c, n_seq=1, seq_len=ts, rate=1, tm=1024)
    (o_p,) = _band_attention(zc_p, qcol=0, kcol=kc1, vcol=vc1, qw=H_C * DH, kw=KV_C * DH,
                             max_dist=WIN_C - 1, sinks=sinks_c[0], with_lse=False)
    keep = min(WIN_C, s_len)
    swa_kv_p = zc_p.reshape(bsz, s_len, N_C)[:, s_len - keep:, kc1:].reshape(1, bsz, keep, 2, KV_C, DH)

    o_s, swa_rolled = _swa_cache(zc_s.reshape(dbs, t_len, N_C), to_time_minor(cache_swa_kv[0]),
                                 sinks_c[0], t_len=t_len)
    o_s = o_s.reshape(ts, H_C * DH)
    swa_kv_s = to_time_major(swa_rolled)

    w_out1 = w_out_c[0].astype(_BF16)
    xp = _outln([o_p], xp, w_out1, row2(ln1_g[1]), row2(ln1_b[1]), tm=512)
    xs = _outln([o_s], xs, w_out1, row2(ln1_g[1]), row2(ln1_b[1]), tm=512)
    wu1, wd1 = w_up[1].astype(_BF16), w_down[1].astype(_BF16)
    xp = _mlp(xp, wu1, wd1, row2(ln2_g[1]), row2(ln2_b[1]), tm=512, ff_chunk=1024)
    xs = _mlp(xs, wu1, wd1, row2(ln2_g[1]), row2(ln2_b[1]), tm=512, ff_chunk=1024)

    return (xp.reshape(bsz, s_len, D_MODEL), xs.reshape(dbs, t_len, D_MODEL),
            mc_p[None], mc_s[None], mn_p[None], mn_s[None],
            mm_p.reshape(1, bsz, H_A), mm_s.reshape(1, dbs, H_A),
            dil_kv_p[0], dil_kv_s[0], dil_kv_p[1], dil_kv_s[1], dil_kv_p[2], dil_kv_s[2],
            swa_kv_p, swa_kv_s)
```

```python
import functools

import jax
import jax.numpy as jnp
from jax import lax
from jax.experimental import pallas as pl
from jax.experimental.pallas import tpu as pltpu

LANES = 128
SUBLANES = 8
MXU_COLS = 256
VMEM_LIMIT = 56 * 1024 * 1024
CACHE_STEP_BYTES = 4 * 1024 * 1024

D_MODEL = 1024
DH = 64
ROT_DIM = DH // 4
ROPE_THETA = 500000.0
PAST_LEN = 8192
BLK = 128
H_A = 4
DH_A = 128
W_A = H_A * DH_A
N_DIL = 3
DIL_WINDOWS = (128, 512, 2048)
DIL_RATES = (1, 4, 16)
H_B = 8
W_B = H_B * DH
H_C = 16
KV_C = 2
WIN_C = 128
D_FF = 4 * D_MODEL
DEPTH = 2
ALPHA = (2.0 * DEPTH) ** 0.25
LN_EPS = 1e-5
N_AB = 4 * W_A + 3 * N_DIL * W_B
N_C = (H_C + 2 * KV_C) * DH
NEG_INF = float("-inf")

_F32 = jnp.float32
_BF16 = jnp.bfloat16


def _params(sem):
    return pltpu.CompilerParams(dimension_semantics=sem, vmem_limit_bytes=VMEM_LIMIT)


def _nt_dot(a, b):
    return lax.dot_general(a, b, (((1,), (1,)), ((), ())), preferred_element_type=_F32)


def _tn_dot(a, b):
    return lax.dot_general(a, b, (((0,), (0,)), ((), ())), preferred_element_type=_F32)


def _dot(a, b):
    return jnp.dot(a, b, preferred_element_type=_F32)


def _log_sigmoid(x):
    return -(jnp.maximum(-x, 0.0) + jnp.log1p(jnp.exp(-jnp.abs(x))))


def _layer_norm(y, g, b):
    mu = jnp.mean(y, axis=-1, keepdims=True)
    yc = y - mu
    var = jnp.mean(yc * yc, axis=-1, keepdims=True)
    return yc * lax.rsqrt(var + LN_EPS) * g + b


def _rope_tables(pos):
    half = ROT_DIM // 2
    inv = ROPE_THETA ** (-jnp.arange(half, dtype=_F32) / half)
    ang = pos.astype(_F32)[:, None] * inv[None, :]
    cos = jnp.cos(ang)
    sin = jnp.sin(ang)
    n = pos.shape[0]
    ones = jnp.ones((n, DH - ROT_DIM), _F32)
    zeros = jnp.zeros((n, DH - ROT_DIM), _F32)
    zh = jnp.zeros((n, half), _F32)
    cos_h = jnp.concatenate([cos, cos, ones], axis=1)
    sinm_h = jnp.concatenate([-sin, zh, zeros], axis=1)
    sinp_h = jnp.concatenate([zh, sin, zeros], axis=1)
    tile2 = lambda t: jnp.concatenate([t, t], axis=1)
    return tile2(cos_h), tile2(sinm_h), tile2(sinp_h)


def _proj_kernel(x_ref, w_ref, cos_ref, sinm_ref, sinp_ref, o_ref, *, rope_chunks):
    out = o_ref.at[0, 0]
    xb = x_ref[...].astype(_BF16)
    n = w_ref.shape[1]
    half = ROT_DIM // 2
    for c0 in range(0, n, MXU_COLS):
        width = min(MXU_COLS, n - c0)
        acc = _dot(xb, w_ref[:, c0:c0 + width])
        for cc in range(width // LANES):
            sub = acc[:, cc * LANES:(cc + 1) * LANES]
            if rope_chunks[c0 // LANES + cc]:
                sub = (sub * cos_ref[...]
                       + pltpu.roll(sub, LANES - half, axis=1) * sinm_ref[...]
                       + pltpu.roll(sub, half, axis=1) * sinp_ref[...])
            out[:, c0 + cc * LANES:c0 + (cc + 1) * LANES] = sub


def _project(x, w, tables, rope_chunks, *, n_seq, seq_len, rate, tm):
    d = x.shape[1]
    n = w.shape[1]
    assert len(rope_chunks) * LANES == n
    length = seq_len // rate
    tm = min(tm, length)
    nj = length // tm
    xv = x.reshape(n_seq * length, rate * d)
    tabs = [t.reshape(length, rate * LANES) for t in tables]
    row_tab = pl.BlockSpec((tm, LANES), lambda b, r, j: (j, r))
    return pl.pallas_call(
        functools.partial(_proj_kernel, rope_chunks=tuple(rope_chunks)),
        grid=(n_seq, rate, nj),
        in_specs=[pl.BlockSpec((tm, d), lambda b, r, j: (b * nj + j, r)),
                  _const_spec(w.shape), row_tab, row_tab, row_tab],
        out_specs=pl.BlockSpec((1, 1, tm, n), lambda b, r, j: (b, r, j, 0)),
        out_shape=jax.ShapeDtypeStruct((n_seq, rate, length, n), _F32),
        compiler_params=_params(("parallel", "parallel", "parallel")),
        name="proj",
    )(xv, w, *tabs)


def _mlstm_kernel(q_ref, k_ref, v_ref, oa_ref, g_ref, bias_ref, c0_ref, n0_ref, m0_ref,
                  ga_ref, c_ref, n_ref, m_ref, *, chunk, nb):
    c_idx = pl.program_id(1)

    @pl.when(c_idx == 0)
    def _():
        c_ref[...] = c0_ref[...]
        n_ref[...] = n0_ref[...]
        m_ref[...] = m0_ref[...]

    row = lax.broadcasted_iota(jnp.int32, (chunk, chunk), 0)
    col = lax.broadcasted_iota(jnp.int32, (chunk, chunk), 1)
    eye = row == col
    tril = col <= row

    def to_row(colvec):
        return jnp.sum(jnp.where(eye, colvec, 0.0), axis=0, keepdims=True)

    for bi in range(nb):
        r0 = bi * chunk
        gates = g_ref[r0:r0 + chunk, :] + bias_ref[...]
        for h in range(H_A):
            lo = h * DH_A
            q = q_ref[r0:r0 + chunk, lo:lo + DH_A]
            k = k_ref[r0:r0 + chunk, lo:lo + DH_A] * (DH_A ** -0.5)
            v = v_ref[r0:r0 + chunk, lo:lo + DH_A]
            oa = oa_ref[r0:r0 + chunk, lo:lo + DH_A]
            c_prev = c_ref[bi, h]
            n_prev = n_ref[bi, h:h + 1, :]
            m_prev = m_ref[bi, :, h:h + 1]
            li_col = gates[:, h:h + 1]
            lf_col = _log_sigmoid(gates[:, H_A + h:H_A + h + 1])
            li_row = to_row(li_col)
            lf_row = to_row(lf_col)
            b_col = jnp.sum(jnp.where(tril, lf_row, 0.0), axis=1, keepdims=True)
            b_row = to_row(b_col)
            dmat = jnp.where(tril, b_col - b_row + li_row, NEG_INF)
            a_col = b_col + m_prev
            mt = jnp.maximum(a_col, jnp.max(dmat, axis=1, keepdims=True))
            wts = jnp.exp(dmat - mt)
            inter = jnp.exp(a_col - mt)
            qb = q.astype(_BF16)
            kb = k.astype(_BF16)
            vb = v.astype(_BF16)
            sqk = _nt_dot(qb, kb) * wts
            num = inter * _dot(qb, c_prev.astype(_BF16)) + _dot(sqk.astype(_BF16), vb)
            nq = (inter * jnp.sum(q * n_prev, axis=1, keepdims=True)
                  + jnp.sum(sqk, axis=1, keepdims=True))
            hid = num / jnp.maximum(jnp.abs(nq), jnp.exp(-mt))
            ga_ref[r0:r0 + chunk, lo:lo + DH_A] = hid * jax.nn.sigmoid(oa)
            b_last = b_col[chunk - 1:chunk, :]
            mt_last = mt[chunk - 1:chunk, :]
            w_last = jnp.exp(b_last - b_col + li_col - mt_last)
            inter_last = inter[chunk - 1:chunk, :]
            kw = k * w_last
            c_ref[bi, h] = inter_last * c_prev + _tn_dot(kw.astype(_BF16), vb)
            n_ref[bi, h:h + 1, :] = inter_last * n_prev + jnp.sum(kw, axis=0, keepdims=True)
            m_ref[bi, :, h:h + 1] = mt_last


def _mlstm(z, bias, c0, n0, m0, *, n_seq, seq_len, chunk, nb):
    t = z.shape[0]
    nc = seq_len // chunk
    rows = nb * chunk
    col = lambda cb: pl.BlockSpec((rows, W_A), lambda b, c: (b * nc + c, cb))
    state_c = pl.BlockSpec((nb, H_A, DH_A, DH_A), lambda b, c: (b, 0, 0, 0))
    state_n = pl.BlockSpec((nb, H_A, DH_A), lambda b, c: (b, 0, 0))
    state_m = pl.BlockSpec((nb, 1, H_A), lambda b, c: (b, 0, 0))
    return pl.pallas_call(
        functools.partial(_mlstm_kernel, chunk=chunk, nb=nb),
        grid=(n_seq // nb, nc),
        in_specs=[col(0), col(1), col(2), col(3),
                  pl.BlockSpec((rows, LANES), lambda b, c: (b * nc + c, 4 * W_A // LANES)),
                  pl.BlockSpec((1, LANES), lambda b, c: (0, 0)),
                  state_c, state_n, state_m],
        out_specs=[pl.BlockSpec((rows, W_A), lambda b, c: (b * nc + c, 0)),
                   state_c, state_n, state_m],
        out_shape=[jax.ShapeDtypeStruct((t, W_A), _F32),
                   jax.ShapeDtypeStruct((n_seq, H_A, DH_A, DH_A), _F32),
                   jax.ShapeDtypeStruct((n_seq, H_A, DH_A), _F32),
                   jax.ShapeDtypeStruct((n_seq, 1, H_A), _F32)],
        compiler_params=_params(("parallel", "arbitrary")),
        name="mlstm",
    )(z, z, z, z, z, bias, c0, n0, m0)


def _band_kernel(*refs, max_dist, n_qtiles, kv_shared, with_sinks, with_lse):
    idx = 0
    if with_sinks:
        sink_ref = refs[0]
        idx = 1
    q_ref, k_ref, v_ref = refs[idx:idx + 3]
    idx += 3
    o_ref = refs[idx]
    idx += 1
    if with_lse:
        l_ref = refs[idx]
        idx += 1
    kprev_ref, vprev_ref = refs[idx:idx + 2]
    j = pl.program_id(2)
    n_kv = kprev_ref.shape[0]

    @pl.when(j == 0)
    def _():
        kprev_ref[...] = jnp.zeros_like(kprev_ref)
        vprev_ref[...] = jnp.zeros_like(vprev_ref)

    lane = lax.broadcasted_iota(jnp.int32, (BLK, LANES), 1)
    low = lane < DH
    if kv_shared:
        k_in, v_in = k_ref[0, 0], v_ref[0, 0]
        k_sw = pltpu.roll(k_in, DH, axis=1)
        v_sw = pltpu.roll(v_in, DH, axis=1)
        k_tiles = [jnp.where(low, k_in, k_sw), jnp.where(low, k_sw, k_in)]
        v_tiles = [jnp.where(low, v_in, v_sw), jnp.where(low, v_sw, v_in)]
    else:
        k_tiles = [k_ref[0, 0, :, t * LANES:(t + 1) * LANES] for t in range(n_kv)]
        v_tiles = [v_ref[0, 0, :, t * LANES:(t + 1) * LANES] for t in range(n_kv)]
    k_tiles = [t.astype(_BF16) for t in k_tiles]
    v_tiles = [t.astype(_BF16) for t in v_tiles]

    qi = lax.broadcasted_iota(jnp.int32, (BLK, 2 * BLK), 0) + BLK
    ki = lax.broadcasted_iota(jnp.int32, (BLK, 2 * BLK), 1)
    dist = qi - ki
    first_key = jnp.where(j > 0, 0, BLK)
    valid = (dist >= 0) & (dist <= max_dist) & (ki >= first_key)

    q_all = q_ref[0, 0] * (DH ** -0.5)
    tiles_per_kv = n_qtiles // n_kv
    for kt in range(n_kv):
        k2 = jnp.concatenate([kprev_ref[kt], k_tiles[kt]], axis=0)
        v2 = jnp.concatenate([vprev_ref[kt], v_tiles[kt]], axis=0)
        parts = []
        for p in range(kt * tiles_per_kv, (kt + 1) * tiles_per_kv):
            q2 = q_all[:, p * LANES:(p + 1) * LANES]
            parts.append(jnp.where(low, q2, 0.0).astype(_BF16))
            parts.append(jnp.where(low, 0.0, q2).astype(_BF16))
        s_all = _nt_dot(jnp.concatenate(parts, axis=0), k2)
        probs, dens, lses = [], [], []
        for hh in range(2 * tiles_per_kv):
            s = jnp.where(valid, s_all[hh * BLK:(hh + 1) * BLK, :], NEG_INF)
            m = jnp.max(s, axis=1, keepdims=True)
            if with_sinks:
                sk = sink_ref[2 * kt * tiles_per_kv + hh]
                m = jnp.maximum(m, sk)
            p_exp = jnp.exp(s - m)
            den = jnp.sum(p_exp, axis=1, keepdims=True)
            if with_sinks:
                den = den + jnp.exp(sk - m)
            probs.append(p_exp.astype(_BF16))
            dens.append(den)
            lses.append(m + jnp.log(den))
        o_all = _dot(jnp.concatenate(probs, axis=0), v2)
        for pp in range(tiles_per_kv):
            p = kt * tiles_per_kv + pp
            e, o = 2 * pp, 2 * pp + 1
            o_even = o_all[e * BLK:(e + 1) * BLK, :] / dens[e]
            o_odd = o_all[o * BLK:(o + 1) * BLK, :] / dens[o]
            o_ref[0, :, p * LANES:(p + 1) * LANES] = jnp.where(low, o_even, o_odd)
            if with_lse:
                l_ref[0, :, p * LANES:(p + 1) * LANES] = jnp.where(low, lses[e], lses[o])

    for kt in range(n_kv):
        kprev_ref[kt] = k_tiles[kt]
        vprev_ref[kt] = v_tiles[kt]


def _band_attention(z4, *, qcol, kcol, vcol, qw, kw, max_dist, sinks=None, with_lse):
    n_seq, rate, length, _ = z4.shape
    seq_len = length * rate
    nblk = length // BLK
    kv_shared = kw == LANES and qw > LANES
    n_kv = 2 if kv_shared else kw // LANES

    def spec(width, coff):
        assert coff % width == 0
        return pl.BlockSpec((1, 1, BLK, width), lambda b, r, j: (b, r, j, coff // width))

    out_spec = pl.BlockSpec((1, BLK, qw), lambda b, r, j: (b, j, r))
    out_shape = [jax.ShapeDtypeStruct((n_seq, length, rate * qw), _F32)]
    out_specs = [out_spec]
    if with_lse:
        out_shape.append(jax.ShapeDtypeStruct((n_seq, length, rate * qw), _F32))
        out_specs.append(out_spec)
    in_specs = [spec(qw, qcol), spec(kw, kcol), spec(kw, vcol)]
    args = [z4, z4, z4]
    if sinks is not None:
        in_specs = [pl.BlockSpec(memory_space=pltpu.SMEM)] + in_specs
        args = [sinks] + args
    res = pl.pallas_call(
        functools.partial(_band_kernel, max_dist=max_dist, n_qtiles=qw // LANES,
                          kv_shared=kv_shared, with_sinks=sinks is not None, with_lse=with_lse),
        grid=(n_seq, rate, nblk),
        in_specs=in_specs, out_specs=out_specs, out_shape=out_shape,
        scratch_shapes=[pltpu.VMEM((n_kv, BLK, LANES), _BF16),
                        pltpu.VMEM((n_kv, BLK, LANES), _BF16)],
        compiler_params=_params(("parallel", "parallel", "arbitrary")),
        name="band_attn",
    )(*args)
    return [r.reshape(n_seq * seq_len, qw) for r in res]


def _outln_ab_kernel(ga_ref, o1_ref, o2_ref, o3_ref, l1_ref, l2_ref, l3_ref, x_ref, w_ref,
                     g_ref, b_ref, y_ref):
    l1, l2, l3 = l1_ref[...], l2_ref[...], l3_ref[...]
    lmax = jnp.maximum(jnp.maximum(l1, l2), l3)
    e1, e2, e3 = jnp.exp(l1 - lmax), jnp.exp(l2 - lmax), jnp.exp(l3 - lmax)
    ob = (e1 * o1_ref[...] + e2 * o2_ref[...] + e3 * o3_ref[...]) / (e1 + e2 + e3)
    mix = (_dot(ga_ref[...].astype(_BF16), w_ref[0:W_A, :])
           + _dot(ob.astype(_BF16), w_ref[W_A:W_A + W_B, :]))
    y_ref[...] = _layer_norm(ALPHA * x_ref[...] + mix, g_ref[...], b_ref[...])


def _outln_c_kernel(o_ref, x_ref, w_ref, g_ref, b_ref, y_ref):
    mix = _dot(o_ref[...].astype(_BF16), w_ref[...])
    y_ref[...] = _layer_norm(ALPHA * x_ref[...] + mix, g_ref[...], b_ref[...])


def _const_spec(shape):
    return pl.BlockSpec(shape, lambda *_: (0,) * len(shape), pipeline_mode=pl.Buffered(1))


def _outln(parts, x, w, g, b, tm):
    t = x.shape[0]
    tm = min(tm, t)
    row = lambda width: pl.BlockSpec((tm, width), lambda i: (i, 0))
    kern = _outln_ab_kernel if len(parts) > 1 else _outln_c_kernel
    return pl.pallas_call(
        kern, grid=(t // tm,),
        in_specs=[row(p.shape[1]) for p in parts]
        + [row(D_MODEL), _const_spec(w.shape), _const_spec((1, D_MODEL)), _const_spec((1, D_MODEL))],
        out_specs=row(D_MODEL),
        out_shape=jax.ShapeDtypeStruct((t, D_MODEL), _F32),
        compiler_params=_params(("parallel",)),
        name="outproj_ln",
    )(*parts, x, w, g, b)


def _mlp_kernel(x_ref, wu_ref, wd_ref, g_ref, b_ref, y_ref, *, ff_chunk):
    x = x_ref[...]
    xb = x.astype(_BF16)
    acc = ALPHA * x
    for c in range(D_FF // ff_chunk):
        hid = _dot(xb, wu_ref[:, c * ff_chunk:(c + 1) * ff_chunk])
        hid = jnp.square(jnp.maximum(hid, 0.0)).astype(_BF16)
        acc = acc + _dot(hid, wd_ref[c * ff_chunk:(c + 1) * ff_chunk, :])
    y_ref[...] = _layer_norm(acc, g_ref[...], b_ref[...])


def _mlp(x, wu, wd, g, b, tm, ff_chunk):
    t = x.shape[0]
    tm = min(tm, t)
    row = pl.BlockSpec((tm, D_MODEL), lambda i: (i, 0))
    return pl.pallas_call(
        functools.partial(_mlp_kernel, ff_chunk=ff_chunk), grid=(t // tm,),
        in_specs=[row, _const_spec(wu.shape), _const_spec(wd.shape),
                  _const_spec((1, D_MODEL)), _const_spec((1, D_MODEL))],
        out_specs=row,
        out_shape=jax.ShapeDtypeStruct((t, D_MODEL), _F32),
        compiler_params=_params(("parallel",)),
        name="mlp_ln",
    )(x, wu, wd, g, b)


def _roll_window(old, new, t_len):
    win = old.shape[1]
    rolled = pltpu.roll(old, win - t_len, axis=1)
    new_tail = jnp.concatenate([jnp.zeros((LANES - t_len, LANES), _F32), new], axis=0).T
    tail_lane = lax.broadcasted_iota(jnp.int32, new_tail.shape, 1) >= LANES - t_len
    tail = jnp.where(tail_lane, new_tail, rolled[:, win - LANES:])
    if win == LANES:
        return tail
    return jnp.concatenate([rolled[:, :win - LANES], tail], axis=1)


def _dil_cache_kernel(q_ref, k_ref, v_ref, cache_ref, o_ref, l_ref, out_ref, *, rate, t_len):
    win = cache_ref.shape[-1]
    rows = 2 * t_len
    key_i = lax.broadcasted_iota(jnp.int32, (rows, win), 1)
    row_t = lax.broadcasted_iota(jnp.int32, (rows, win), 0) % t_len
    ok_c = (key_i >= row_t) & (((key_i - row_t) & (rate - 1)) == 0)
    n_j = lax.broadcasted_iota(jnp.int32, (rows, t_len), 1)
    n_t = lax.broadcasted_iota(jnp.int32, (rows, t_len), 0) % t_len
    ok_n = (n_j <= n_t) & (((n_t - n_j) & (rate - 1)) == 0)
    low = lax.broadcasted_iota(jnp.int32, (t_len, LANES), 1) < DH
    _for_each(cache_ref.shape[0], functools.partial(
        _dil_cache_one, q_ref, k_ref, v_ref, cache_ref, o_ref, l_ref, out_ref,
        ok_c, ok_n, low, t_len))


def _for_each(n, body):
    if n == 1:
        body(0)
    else:
        lax.fori_loop(0, n, lambda i, carry: (body(i), carry)[1], 0)


def _dil_cache_one(q_ref, k_ref, v_ref, cache_ref, o_ref, l_ref, out_ref, ok_c, ok_n, low, t_len, bi):
    win = cache_ref.shape[-1]
    rows = 2 * t_len
    for a in range(H_B // 2):
        c = a * LANES
        q2 = q_ref[bi, :, c:c + LANES] * (DH ** -0.5)
        lhs = jnp.concatenate([jnp.where(low, q2, 0.0), jnp.where(low, 0.0, q2)], axis=0).astype(_BF16)
        k_old = cache_ref[bi, 0, 2 * a:2 * a + 2].reshape(2 * DH, win)
        v_old = cache_ref[bi, 1, 2 * a:2 * a + 2].reshape(2 * DH, win)
        k_new = k_ref[bi, :, c:c + LANES]
        v_new = v_ref[bi, :, c:c + LANES]
        s_c = jnp.where(ok_c, _dot(lhs, k_old.astype(_BF16)), NEG_INF)
        s_n = jnp.where(ok_n, _nt_dot(lhs, k_new.astype(_BF16)), NEG_INF)
        m = jnp.maximum(jnp.max(s_c, axis=1, keepdims=True), jnp.max(s_n, axis=1, keepdims=True))
        p_c = jnp.exp(s_c - m)
        p_n = jnp.exp(s_n - m)
        den = jnp.sum(p_c, axis=1, keepdims=True) + jnp.sum(p_n, axis=1, keepdims=True)
        o = (_nt_dot(p_c.astype(_BF16), v_old.astype(_BF16))
             + _dot(p_n.astype(_BF16), v_new.astype(_BF16))) / den
        lse = m + jnp.log(den)
        o_ref[bi, :, c:c + LANES] = jnp.where(low, o[0:t_len], o[t_len:rows])
        l_ref[bi, :, c:c + LANES] = jnp.where(low, lse[0:t_len], lse[t_len:rows])
        for kv, old, new in ((0, k_old, k_new), (1, v_old, v_new)):
            out_ref[bi, kv, 2 * a:2 * a + 2] = _roll_window(old, new, t_len).reshape(2, DH, win)


def _dil_cache(z_s3, cache_t, *, g, rate, t_len, nb):
    bsz, _, _, _, win = cache_t.shape
    q_blk, k_blk, v_blk = 3 * g, 3 * g + 1, 3 * g + 2
    col = lambda cb: pl.BlockSpec((nb, t_len, W_B), lambda b: (b, 0, cb))
    tok = pl.BlockSpec((nb, t_len, W_B), lambda b: (b, 0, 0))
    blk5 = lambda last: pl.BlockSpec((nb, 2, H_B, DH, last), lambda b: (b, 0, 0, 0, 0))
    out = jax.ShapeDtypeStruct((bsz, t_len, W_B), _F32)
    return pl.pallas_call(
        functools.partial(_dil_cache_kernel, rate=rate, t_len=t_len), grid=(bsz // nb,),
        in_specs=[col(q_blk), col(k_blk), col(v_blk), blk5(win)],
        out_specs=[tok, tok, blk5(win)],
        out_shape=[out, out, jax.ShapeDtypeStruct(cache_t.shape, _F32)],
        compiler_params=_params(("parallel",)),
        name="dil_cache",
    )(z_s3, z_s3, z_s3, cache_t)


def _swa_sample_kernel(sink_ref, z_ref, cache_ref, o_ref, out_ref, *, t_len):
    grp = H_C // KV_C
    tiles = grp // 2
    rows = t_len * grp
    kw = KV_C * DH
    q_w = H_C * DH
    r_t = lax.broadcasted_iota(jnp.int32, (rows, WIN_C), 0) % t_len
    key_i = lax.broadcasted_iota(jnp.int32, (rows, WIN_C), 1)
    ok_c = key_i > r_t
    n_t = lax.broadcasted_iota(jnp.int32, (rows, t_len), 0) % t_len
    n_j = lax.broadcasted_iota(jnp.int32, (rows, t_len), 1)
    ok_n = n_j <= n_t
    r_h = lax.broadcasted_iota(jnp.int32, (rows, 1), 0) // t_len
    low_n = lax.broadcasted_iota(jnp.int32, (t_len, LANES), 1) < DH

    def dup(x, j):
        sw = pltpu.roll(x, DH, axis=1)
        return (jnp.where(low_n, x, sw) if j == 0 else jnp.where(low_n, sw, x)).astype(_BF16)

    sinks = []
    for j in range(KV_C):
        sk = jnp.zeros((rows, 1), _F32)
        for g in range(grp):
            sk = jnp.where(r_h == g, sink_ref[j * grp + g], sk)
        sinks.append(sk)

    def one(bi):
        k_n, v_n = z_ref[bi, :, q_w:q_w + kw], z_ref[bi, :, q_w + kw:q_w + 2 * kw]
        for kv, new in ((0, k_n), (1, v_n)):
            old = cache_ref[bi, kv].reshape(kw, WIN_C)
            out_ref[bi, kv] = _roll_window(old, new, t_len).reshape(KV_C, DH, WIN_C)
        for j in range(KV_C):
            k_old = cache_ref[bi, 0, j].astype(_BF16)
            v_old = cache_ref[bi, 1, j].astype(_BF16)
            k_dup = jnp.concatenate([k_old, k_old], axis=0)
            v_dup = jnp.concatenate([v_old, v_old], axis=0)
            parts = []
            for a in range(tiles):
                c = (j * tiles + a) * LANES
                q2 = z_ref[bi, :, c:c + LANES] * (DH ** -0.5)
                parts.append(jnp.where(low_n, q2, 0.0).astype(_BF16))
                parts.append(jnp.where(low_n, 0.0, q2).astype(_BF16))
            q = jnp.concatenate(parts, axis=0)
            sk = sinks[j]
            s_c = jnp.where(ok_c, _dot(q, k_dup), NEG_INF)
            s_n = jnp.where(ok_n, _nt_dot(q, dup(k_n, j)), NEG_INF)
            m = jnp.maximum(jnp.maximum(jnp.max(s_c, axis=1, keepdims=True),
                                        jnp.max(s_n, axis=1, keepdims=True)), sk)
            p_c = jnp.exp(s_c - m)
            p_n = jnp.exp(s_n - m)
            den = (jnp.sum(p_c, axis=1, keepdims=True) + jnp.sum(p_n, axis=1, keepdims=True)
                   + jnp.exp(sk - m))
            o = (_nt_dot(p_c.astype(_BF16), v_dup)
                 + _dot(p_n.astype(_BF16), dup(v_n, j))) / den
            for a in range(tiles):
                c = (j * tiles + a) * LANES
                even = o[(2 * a) * t_len:(2 * a + 1) * t_len, :]
                odd = o[(2 * a + 1) * t_len:(2 * a + 2) * t_len, :]
                o_ref[bi, :, c:c + LANES] = jnp.where(low_n, even, odd)

    _for_each(cache_ref.shape[0], one)


def _swa_cache(zc_s3, cache_t, sinks, *, t_len, nb):
    dbs = cache_t.shape[0]
    q_w = H_C * DH
    blk5 = lambda last: pl.BlockSpec((nb, 2, KV_C, DH, last), lambda b: (b, 0, 0, 0, 0))
    return pl.pallas_call(
        functools.partial(_swa_sample_kernel, t_len=t_len), grid=(dbs // nb,),
        in_specs=[pl.BlockSpec(memory_space=pltpu.SMEM),
                  pl.BlockSpec((nb, t_len, N_C), lambda b: (b, 0, 0)),
                  blk5(WIN_C)],
        out_specs=[pl.BlockSpec((nb, t_len, q_w), lambda b: (b, 0, 0)), blk5(WIN_C)],
        out_shape=[jax.ShapeDtypeStruct((dbs, t_len, q_w), _F32),
                   jax.ShapeDtypeStruct(cache_t.shape, _F32)],
        compiler_params=_params(("parallel",)),
        name="swa_cache",
    )(sinks, zc_s3, cache_t)


def _rope_chunks(widths_and_flags):
    out = []
    for width, flag in widths_and_flags:
        out += [flag] * (width // LANES)
    return tuple(out)


def kernel(x_prompt, x_sample, state_mlstm_C, state_mlstm_n, state_mlstm_m, cache_dil1_kv, cache_dil2_kv, cache_dil3_kv, cache_swa_kv, w_in_ab, b_gate_ab, w_out_ab, w_in_c, sinks_c, w_out_c, ln1_g, ln1_b, ln2_g, ln2_b, w_up, w_down):
    bsz, s_len, _ = x_prompt.shape
    dbs, t_len, _ = x_sample.shape
    tp = bsz * s_len
    ts = dbs * t_len
    xp = x_prompt.reshape(tp, D_MODEL)
    xs = x_sample.reshape(ts, D_MODEL)
    pos_p = jnp.arange(s_len, dtype=jnp.int32)
    pos_s = PAST_LEN + jnp.arange(t_len, dtype=jnp.int32)
    tab_p = _rope_tables(pos_p)
    tab_s = tuple(jnp.tile(t, (ts // t_len, 1)) for t in _rope_tables(pos_s))
    row2 = lambda v: v.reshape(1, -1)

    w_in = w_in_ab[0]
    gate_lo = 4 * W_A
    q_lo = gate_lo + 2 * H_A
    k_lo = q_lo + N_DIL * W_B
    v_lo = k_lo + N_DIL * W_B
    w_a = jnp.concatenate([w_in[:, :gate_lo],
                           jnp.pad(w_in[:, gate_lo:q_lo], ((0, 0), (0, LANES - 2 * H_A)))],
                          axis=1).astype(_BF16)
    w_grp = [jnp.concatenate([w_in[:, lo + g * W_B:lo + (g + 1) * W_B] for lo in (q_lo, k_lo, v_lo)],
                             axis=1).astype(_BF16) for g in range(N_DIL)]
    rope_a = _rope_chunks([(gate_lo + LANES, False)])
    rope_grp = _rope_chunks([(2 * W_B, True), (W_B, False)])
    bias = jnp.pad(b_gate_ab[0], (0, LANES - 2 * H_A)).reshape(1, LANES)
    w_out0 = w_out_ab[0].astype(_BF16)

    za_p = _project(xp, w_a, tab_p, rope_a, n_seq=bsz, seq_len=s_len, rate=1, tm=1024)
    za_s = _project(xs, w_a, tab_s, rope_a, n_seq=1, seq_len=ts, rate=1, tm=1024)
    zg_s = _project(xs, jnp.concatenate(w_grp, axis=1), tab_s, rope_grp * N_DIL,
                    n_seq=1, seq_len=ts, rate=1, tm=1024)

    zeros_c = jnp.zeros((bsz, H_A, DH_A, DH_A), _F32)
    zeros_n = jnp.zeros((bsz, H_A, DH_A), _F32)
    zeros_m = jnp.zeros((bsz, 1, H_A), _F32)
    ga_p, mc_p, mn_p, mm_p = _mlstm(za_p.reshape(tp, -1), bias, zeros_c, zeros_n, zeros_m,
                                    n_seq=bsz, seq_len=s_len, chunk=256, nb=1)
    ga_s, mc_s, mn_s, mm_s = _mlstm(za_s.reshape(ts, -1), bias, state_mlstm_C[0], state_mlstm_n[0],
                                    state_mlstm_m[0].reshape(dbs, 1, H_A),
                                    n_seq=dbs, seq_len=t_len, chunk=t_len, nb=8)

    dil_caches = (cache_dil1_kv, cache_dil2_kv, cache_dil3_kv)
    outs_p, lses_p, outs_s, lses_s, dil_kv_p, dil_kv_s = [], [], [], [], [], []
    to_time_minor = lambda c: jnp.transpose(c, (0, 2, 3, 4, 1))
    to_time_major = lambda c: jnp.transpose(c, (0, 4, 1, 2, 3))[None]
    z_s3 = zg_s.reshape(dbs, t_len, N_DIL * 3 * W_B)
    for g in range(N_DIL):
        rate, win = DIL_RATES[g], DIL_WINDOWS[g]
        zg_p = _project(xp, w_grp[g], tab_p, rope_grp, n_seq=bsz, seq_len=s_len, rate=rate, tm=1024)
        o, l = _band_attention(zg_p, qcol=0, kcol=W_B, vcol=2 * W_B, qw=W_B, kw=W_B,
                               max_dist=win // rate, with_lse=True)
        outs_p.append(o)
        lses_p.append(l)
        keep = min(win, s_len) // rate
        tail = zg_p[:, :, s_len // rate - keep:, W_B:]
        dil_kv_p.append(jnp.transpose(tail, (0, 2, 1, 3)).reshape(1, bsz, keep * rate, 2, H_B, DH))
        o, l, rolled = _dil_cache(z_s3, to_time_minor(dil_caches[g][0]), g=g, rate=rate, t_len=t_len,
                                  nb=max(1, CACHE_STEP_BYTES // (win * 2 * W_B * 4)))
        outs_s.append(o.reshape(ts, W_B))
        lses_s.append(l.reshape(ts, W_B))
        dil_kv_s.append(to_time_major(rolled))

    xp = _outln([ga_p] + outs_p + lses_p, xp, w_out0, row2(ln1_g[0]), row2(ln1_b[0]), tm=512)
    xs = _outln([ga_s] + outs_s + lses_s, xs, w_out0, row2(ln1_g[0]), row2(ln1_b[0]), tm=512)
    wu0, wd0 = w_up[0].astype(_BF16), w_down[0].astype(_BF16)
    xp = _mlp(xp, wu0, wd0, row2(ln2_g[0]), row2(ln2_b[0]), tm=512, ff_chunk=1024)
    xs = _mlp(xs, wu0, wd0, row2(ln2_g[0]), row2(ln2_b[0]), tm=512, ff_chunk=1024)

    w_c = w_in_c[0].astype(_BF16)
    kc1 = H_C * DH
    vc1 = kc1 + KV_C * DH
    rope_c = _rope_chunks([(vc1, True), (KV_C * DH, False)])
    zc_p = _project(xp, w_c, tab_p, rope_c, n_seq=bsz, seq_len=s_len, rate=1, tm=1024)
    zc_s = _project(xs, w_c, tab_s, rope_c, n_seq=1, seq_len=ts, rate=1, tm=1024)
    (o_p,) = _band_attention(zc_p, qcol=0, kcol=kc1, vcol=vc1, qw=H_C * DH, kw=KV_C * DH,
                             max_dist=WIN_C - 1, sinks=sinks_c[0], with_lse=False)
    keep = min(WIN_C, s_len)
    swa_kv_p = zc_p.reshape(bsz, s_len, N_C)[:, s_len - keep:, kc1:].reshape(1, bsz, keep, 2, KV_C, DH)

    o_s, swa_rolled = _swa_cache(zc_s.reshape(dbs, t_len, N_C), to_time_minor(cache_swa_kv[0]),
                                 sinks_c[0], t_len=t_len, nb=8)
    o_s = o_s.reshape(ts, H_C * DH)
    swa_kv_s = to_time_major(swa_rolled)

    w_out1 = w_out_c[0].astype(_BF16)
    xp = _outln([o_p], xp, w_out1, row2(ln1_g[1]), row2(ln1_b[1]), tm=512)
    xs = _outln([o_s], xs, w_out1, row2(ln1_g[1]), row2(ln1_b[1]), tm=512)
    wu1, wd1 = w_up[1].astype(_BF16), w_down[1].astype(_BF16)
    xp = _mlp(xp, wu1, wd1, row2(ln2_g[1]), row2(ln2_b[1]), tm=512, ff_chunk=1024)
    xs = _mlp(xs, wu1, wd1, row2(ln2_g[1]), row2(ln2_b[1]), tm=512, ff_chunk=1024)

    return (xp.reshape(bsz, s_len, D_MODEL), xs.reshape(dbs, t_len, D_MODEL),
            mc_p[None], mc_s[None], mn_p[None], mn_s[None],
            mm_p.reshape(1, bsz, H_A), mm_s.reshape(1, dbs, H_A),
            dil_kv_p[0], dil_kv_s[0], dil_kv_p[1], dil_kv_s[1], dil_kv_p[2], dil_kv_s[2],
            swa_kv_p, swa_kv_s)
```

```python
import functools

import jax
import jax.numpy as jnp
from jax import lax
from jax.experimental import pallas as pl
from jax.experimental.pallas import tpu as pltpu

LANES = 128
SUBLANES = 8
MXU_COLS = 256
VMEM_LIMIT = 56 * 1024 * 1024
CACHE_STEP_BYTES = 4 * 1024 * 1024

D_MODEL = 1024
DH = 64
ROT_DIM = DH // 4
ROPE_THETA = 500000.0
PAST_LEN = 8192
BLK = 128
H_A = 4
DH_A = 128
W_A = H_A * DH_A
N_DIL = 3
DIL_WINDOWS = (128, 512, 2048)
DIL_RATES = (1, 4, 16)
H_B = 8
W_B = H_B * DH
H_C = 16
KV_C = 2
WIN_C = 128
D_FF = 4 * D_MODEL
DEPTH = 2
ALPHA = (2.0 * DEPTH) ** 0.25
LN_EPS = 1e-5
N_AB = 4 * W_A + 3 * N_DIL * W_B
N_C = (H_C + 2 * KV_C) * DH
NEG_INF = float("-inf")

_F32 = jnp.float32
_BF16 = jnp.bfloat16


def _params(sem):
    return pltpu.CompilerParams(dimension_semantics=sem, vmem_limit_bytes=VMEM_LIMIT)


def _nt_dot(a, b):
    return lax.dot_general(a, b, (((1,), (1,)), ((), ())), preferred_element_type=_F32)


def _tn_dot(a, b):
    return lax.dot_general(a, b, (((0,), (0,)), ((), ())), preferred_element_type=_F32)


def _dot(a, b):
    return jnp.dot(a, b, preferred_element_type=_F32)


def _log_sigmoid(x):
    return -(jnp.maximum(-x, 0.0) + jnp.log1p(jnp.exp(-jnp.abs(x))))


def _layer_norm(y, g, b):
    mu = jnp.mean(y, axis=-1, keepdims=True)
    yc = y - mu
    var = jnp.mean(yc * yc, axis=-1, keepdims=True)
    return yc * lax.rsqrt(var + LN_EPS) * g + b


def _rope_tables(pos):
    half = ROT_DIM // 2
    inv = ROPE_THETA ** (-jnp.arange(half, dtype=_F32) / half)
    ang = pos.astype(_F32)[:, None] * inv[None, :]
    cos = jnp.cos(ang)
    sin = jnp.sin(ang)
    n = pos.shape[0]
    ones = jnp.ones((n, DH - ROT_DIM), _F32)
    zeros = jnp.zeros((n, DH - ROT_DIM), _F32)
    zh = jnp.zeros((n, half), _F32)
    cos_h = jnp.concatenate([cos, cos, ones], axis=1)
    sinm_h = jnp.concatenate([-sin, zh, zeros], axis=1)
    sinp_h = jnp.concatenate([zh, sin, zeros], axis=1)
    tile2 = lambda t: jnp.concatenate([t, t], axis=1)
    return tile2(cos_h), tile2(sinm_h), tile2(sinp_h)


def _proj_kernel(x_ref, w_ref, cos_ref, sinm_ref, sinp_ref, o_ref, *tail_refs, rope_chunks, tail_lo):
    out = o_ref.at[0, 0]
    xb = x_ref[...].astype(_BF16)
    tm = xb.shape[0]
    n = w_ref.shape[1]
    half = ROT_DIM // 2
    for c0 in range(0, n, MXU_COLS):
        width = min(MXU_COLS, n - c0)
        acc = _dot(xb, w_ref[:, c0:c0 + width])
        for cc in range(width // LANES):
            lo = c0 + cc * LANES
            sub = acc[:, cc * LANES:(cc + 1) * LANES]
            if rope_chunks[lo // LANES]:
                sub = (sub * cos_ref[...]
                       + pltpu.roll(sub, LANES - half, axis=1) * sinm_ref[...]
                       + pltpu.roll(sub, half, axis=1) * sinp_ref[...])
            out[:, lo:lo + LANES] = sub.astype(o_ref.dtype)
            if tail_refs and lo >= tail_lo:
                tail = tail_refs[0]
                tail[0, 0, :, lo - tail_lo:lo - tail_lo + LANES] = sub[tm - tail.shape[2]:, :]


def _project(x, w, tables, rope_chunks, *, n_seq, seq_len, rate, tm, out_dtype=_F32,
             tail_rows=0, tail_lo=0):
    d = x.shape[1]
    n = w.shape[1]
    assert len(rope_chunks) * LANES == n
    length = seq_len // rate
    tm = min(tm, length)
    nj = length // tm
    xv = x.reshape(n_seq * length, rate * d)
    tabs = [t.reshape(length, rate * LANES) for t in tables]
    row_tab = pl.BlockSpec((tm, LANES), lambda b, r, j: (j, r))
    out_specs = [pl.BlockSpec((1, 1, tm, n), lambda b, r, j: (b, r, j, 0))]
    out_shape = [jax.ShapeDtypeStruct((n_seq, rate, length, n), out_dtype)]
    if tail_rows:
        out_specs.append(pl.BlockSpec((1, 1, tail_rows, n - tail_lo), lambda b, r, j: (b, r, 0, 0)))
        out_shape.append(jax.ShapeDtypeStruct((n_seq, rate, tail_rows, n - tail_lo), _F32))
    res = pl.pallas_call(
        functools.partial(_proj_kernel, rope_chunks=tuple(rope_chunks), tail_lo=tail_lo),
        grid=(n_seq, rate, nj),
        in_specs=[pl.BlockSpec((tm, d), lambda b, r, j: (b * nj + j, r)),
                  _const_spec(w.shape), row_tab, row_tab, row_tab],
        out_specs=out_specs, out_shape=out_shape,
        compiler_params=_params(("parallel", "parallel", "arbitrary")),
        name="proj",
    )(xv, w, *tabs)
    return res if tail_rows else res[0]


def _mlstm_kernel(q_ref, k_ref, v_ref, oa_ref, g_ref, bias_ref, c0_ref, n0_ref, m0_ref,
                  ga_ref, c_ref, n_ref, m_ref, *, chunk, nb):
    c_idx = pl.program_id(1)

    @pl.when(c_idx == 0)
    def _():
        c_ref[...] = c0_ref[...]
        n_ref[...] = n0_ref[...]
        m_ref[...] = m0_ref[...]

    row = lax.broadcasted_iota(jnp.int32, (chunk, chunk), 0)
    col = lax.broadcasted_iota(jnp.int32, (chunk, chunk), 1)
    eye = row == col
    tril = col <= row

    def to_row(colvec):
        return jnp.sum(jnp.where(eye, colvec, 0.0), axis=0, keepdims=True)

    for bi in range(nb):
        r0 = bi * chunk
        gates = g_ref[r0:r0 + chunk, :] + bias_ref[...]
        for h in range(H_A):
            lo = h * DH_A
            q = q_ref[r0:r0 + chunk, lo:lo + DH_A]
            k = k_ref[r0:r0 + chunk, lo:lo + DH_A] * (DH_A ** -0.5)
            v = v_ref[r0:r0 + chunk, lo:lo + DH_A]
            oa = oa_ref[r0:r0 + chunk, lo:lo + DH_A]
            c_prev = c_ref[bi, h]
            n_prev = n_ref[bi, h:h + 1, :]
            m_prev = m_ref[bi, :, h:h + 1]
            li_col = gates[:, h:h + 1]
            lf_col = _log_sigmoid(gates[:, H_A + h:H_A + h + 1])
            li_row = to_row(li_col)
            lf_row = to_row(lf_col)
            b_col = jnp.sum(jnp.where(tril, lf_row, 0.0), axis=1, keepdims=True)
            b_row = to_row(b_col)
            dmat = jnp.where(tril, b_col - b_row + li_row, NEG_INF)
            a_col = b_col + m_prev
            mt = jnp.maximum(a_col, jnp.max(dmat, axis=1, keepdims=True))
            wts = jnp.exp(dmat - mt)
            inter = jnp.exp(a_col - mt)
            qb = q.astype(_BF16)
            kb = k.astype(_BF16)
            vb = v.astype(_BF16)
            sqk = _nt_dot(qb, kb) * wts
            num = inter * _dot(qb, c_prev.astype(_BF16)) + _dot(sqk.astype(_BF16), vb)
            nq = (inter * jnp.sum(q * n_prev, axis=1, keepdims=True)
                  + jnp.sum(sqk, axis=1, keepdims=True))
            hid = num / jnp.maximum(jnp.abs(nq), jnp.exp(-mt))
            ga_ref[r0:r0 + chunk, lo:lo + DH_A] = hid * jax.nn.sigmoid(oa)
            b_last = b_col[chunk - 1:chunk, :]
            mt_last = mt[chunk - 1:chunk, :]
            w_last = jnp.exp(b_last - b_col + li_col - mt_last)
            inter_last = inter[chunk - 1:chunk, :]
            kw = k * w_last
            c_ref[bi, h] = inter_last * c_prev + _tn_dot(kw.astype(_BF16), vb)
            n_ref[bi, h:h + 1, :] = inter_last * n_prev + jnp.sum(kw, axis=0, keepdims=True)
            m_ref[bi, :, h:h + 1] = mt_last


def _mlstm(z, bias, c0, n0, m0, *, n_seq, seq_len, chunk, nb):
    t = z.shape[0]
    nc = seq_len // chunk
    rows = nb * chunk
    col = lambda cb: pl.BlockSpec((rows, W_A), lambda b, c: (b * nc + c, cb))
    state_c = pl.BlockSpec((nb, H_A, DH_A, DH_A), lambda b, c: (b, 0, 0, 0))
    state_n = pl.BlockSpec((nb, H_A, DH_A), lambda b, c: (b, 0, 0))
    state_m = pl.BlockSpec((nb, 1, H_A), lambda b, c: (b, 0, 0))
    return pl.pallas_call(
        functools.partial(_mlstm_kernel, chunk=chunk, nb=nb),
        grid=(n_seq // nb, nc),
        in_specs=[col(0), col(1), col(2), col(3),
                  pl.BlockSpec((rows, LANES), lambda b, c: (b * nc + c, 4 * W_A // LANES)),
                  pl.BlockSpec((1, LANES), lambda b, c: (0, 0)),
                  state_c, state_n, state_m],
        out_specs=[pl.BlockSpec((rows, W_A), lambda b, c: (b * nc + c, 0)),
                   state_c, state_n, state_m],
        out_shape=[jax.ShapeDtypeStruct((t, W_A), _F32),
                   jax.ShapeDtypeStruct((n_seq, H_A, DH_A, DH_A), _F32),
                   jax.ShapeDtypeStruct((n_seq, H_A, DH_A), _F32),
                   jax.ShapeDtypeStruct((n_seq, 1, H_A), _F32)],
        compiler_params=_params(("parallel", "arbitrary")),
        name="mlstm",
    )(z, z, z, z, z, bias, c0, n0, m0)


def _band_kernel(*refs, max_dist, n_qtiles, kv_shared, with_sinks, with_lse):
    idx = 0
    if with_sinks:
        sink_ref = refs[0]
        idx = 1
    q_ref, k_ref, v_ref = refs[idx:idx + 3]
    idx += 3
    o_ref = refs[idx]
    idx += 1
    if with_lse:
        l_ref = refs[idx]
        idx += 1
    kprev_ref, vprev_ref = refs[idx:idx + 2]
    j = pl.program_id(2)
    n_kv = kprev_ref.shape[0]

    @pl.when(j == 0)
    def _():
        kprev_ref[...] = jnp.zeros_like(kprev_ref)
        vprev_ref[...] = jnp.zeros_like(vprev_ref)

    lane = lax.broadcasted_iota(jnp.int32, (BLK, LANES), 1)
    low = lane < DH
    if kv_shared:
        k_in, v_in = k_ref[0, 0], v_ref[0, 0]
        k_sw = pltpu.roll(k_in, DH, axis=1)
        v_sw = pltpu.roll(v_in, DH, axis=1)
        k_tiles = [jnp.where(low, k_in, k_sw), jnp.where(low, k_sw, k_in)]
        v_tiles = [jnp.where(low, v_in, v_sw), jnp.where(low, v_sw, v_in)]
    else:
        k_tiles = [k_ref[0, 0, :, t * LANES:(t + 1) * LANES] for t in range(n_kv)]
        v_tiles = [v_ref[0, 0, :, t * LANES:(t + 1) * LANES] for t in range(n_kv)]
    k_tiles = [t.astype(_BF16) for t in k_tiles]
    v_tiles = [t.astype(_BF16) for t in v_tiles]

    qi = lax.broadcasted_iota(jnp.int32, (BLK, 2 * BLK), 0) + BLK
    ki = lax.broadcasted_iota(jnp.int32, (BLK, 2 * BLK), 1)
    dist = qi - ki
    first_key = jnp.where(j > 0, 0, BLK)
    valid = (dist >= 0) & (dist <= max_dist) & (ki >= first_key)

    q_all = q_ref[0, 0] * (DH ** -0.5)
    tiles_per_kv = n_qtiles // n_kv
    lse_tile = jnp.zeros((BLK, LANES), _F32)
    for kt in range(n_kv):
        k2 = jnp.concatenate([kprev_ref[kt], k_tiles[kt]], axis=0)
        v2 = jnp.concatenate([vprev_ref[kt], v_tiles[kt]], axis=0)
        parts = []
        for p in range(kt * tiles_per_kv, (kt + 1) * tiles_per_kv):
            q2 = q_all[:, p * LANES:(p + 1) * LANES]
            parts.append(jnp.where(low, q2, 0.0).astype(_BF16))
            parts.append(jnp.where(low, 0.0, q2).astype(_BF16))
        s_all = _nt_dot(jnp.concatenate(parts, axis=0), k2)
        probs, dens, lses = [], [], []
        for hh in range(2 * tiles_per_kv):
            s = jnp.where(valid, s_all[hh * BLK:(hh + 1) * BLK, :], NEG_INF)
            m = jnp.max(s, axis=1, keepdims=True)
            if with_sinks:
                sk = sink_ref[2 * kt * tiles_per_kv + hh]
                m = jnp.maximum(m, sk)
            p_exp = jnp.exp(s - m)
            den = jnp.sum(p_exp, axis=1, keepdims=True)
            if with_sinks:
                den = den + jnp.exp(sk - m)
            probs.append(p_exp.astype(_BF16))
            dens.append(den)
            lses.append(m + jnp.log(den))
        o_all = _dot(jnp.concatenate(probs, axis=0), v2)
        for pp in range(tiles_per_kv):
            p = kt * tiles_per_kv + pp
            e, o = 2 * pp, 2 * pp + 1
            o_even = o_all[e * BLK:(e + 1) * BLK, :] / dens[e]
            o_odd = o_all[o * BLK:(o + 1) * BLK, :] / dens[o]
            o_ref[0, :, p * LANES:(p + 1) * LANES] = jnp.where(low, o_even, o_odd)
            if with_lse:
                lse_tile = jnp.where(lane == 2 * p, lses[e], lse_tile)
                lse_tile = jnp.where(lane == 2 * p + 1, lses[o], lse_tile)
    if with_lse:
        l_ref[0] = lse_tile

    for kt in range(n_kv):
        kprev_ref[kt] = k_tiles[kt]
        vprev_ref[kt] = v_tiles[kt]


def _band_attention(z4, *, qcol, kcol, vcol, qw, kw, max_dist, sinks=None, with_lse):
    n_seq, rate, length, _ = z4.shape
    seq_len = length * rate
    nblk = length // BLK
    kv_shared = kw == LANES and qw > LANES
    n_kv = 2 if kv_shared else kw // LANES

    def spec(width, coff):
        assert coff % width == 0
        return pl.BlockSpec((1, 1, BLK, width), lambda b, r, j: (b, r, j, coff // width))

    out_spec = pl.BlockSpec((1, BLK, qw), lambda b, r, j: (b, j, r))
    out_shape = [jax.ShapeDtypeStruct((n_seq, length, rate * qw), _F32)]
    out_specs = [out_spec]
    if with_lse:
        out_shape.append(jax.ShapeDtypeStruct((n_seq, length, rate * LANES), _F32))
        out_specs.append(pl.BlockSpec((1, BLK, LANES), lambda b, r, j: (b, j, r)))
    in_specs = [spec(qw, qcol), spec(kw, kcol), spec(kw, vcol)]
    args = [z4, z4, z4]
    if sinks is not None:
        in_specs = [pl.BlockSpec(memory_space=pltpu.SMEM)] + in_specs
        args = [sinks] + args
    res = pl.pallas_call(
        functools.partial(_band_kernel, max_dist=max_dist, n_qtiles=qw // LANES,
                          kv_shared=kv_shared, with_sinks=sinks is not None, with_lse=with_lse),
        grid=(n_seq, rate, nblk),
        in_specs=in_specs, out_specs=out_specs, out_shape=out_shape,
        scratch_shapes=[pltpu.VMEM((n_kv, BLK, LANES), _BF16),
                        pltpu.VMEM((n_kv, BLK, LANES), _BF16)],
        compiler_params=_params(("parallel", "parallel", "arbitrary")),
        name="band_attn",
    )(*args)
    return [r.reshape(n_seq * seq_len, -1) for r in res]


def _outln_ab_kernel(ga_ref, o1_ref, o2_ref, o3_ref, l1_ref, l2_ref, l3_ref, x_ref, w_ref,
                     g_ref, b_ref, y_ref):
    l1, l2, l3 = l1_ref[...], l2_ref[...], l3_ref[...]
    lmax = jnp.maximum(jnp.maximum(l1, l2), l3)
    e1, e2, e3 = jnp.exp(l1 - lmax), jnp.exp(l2 - lmax), jnp.exp(l3 - lmax)
    inv = 1.0 / (e1 + e2 + e3)
    wts = (e1 * inv, e2 * inv, e3 * inv)
    outs = (o1_ref, o2_ref, o3_ref)
    low = lax.broadcasted_iota(jnp.int32, (l1.shape[0], LANES), 1) < DH
    tiles = []
    for p in range(H_B // 2):
        c = p * LANES
        ob = sum(jnp.where(low, wg[:, 2 * p:2 * p + 1], wg[:, 2 * p + 1:2 * p + 2]) * og[:, c:c + LANES]
                 for wg, og in zip(wts, outs))
        tiles.append(ob)
    mix = (_dot(ga_ref[...].astype(_BF16), w_ref[0:W_A, :])
           + _dot(jnp.concatenate(tiles, axis=1).astype(_BF16), w_ref[W_A:W_A + W_B, :]))
    y_ref[...] = _layer_norm(ALPHA * x_ref[...] + mix, g_ref[...], b_ref[...])


def _outln_c_kernel(o_ref, x_ref, w_ref, g_ref, b_ref, y_ref):
    mix = _dot(o_ref[...].astype(_BF16), w_ref[...])
    y_ref[...] = _layer_norm(ALPHA * x_ref[...] + mix, g_ref[...], b_ref[...])


def _const_spec(shape):
    return pl.BlockSpec(shape, lambda *_: (0,) * len(shape), pipeline_mode=pl.Buffered(1))


def _outln(parts, x, w, g, b, tm):
    t = x.shape[0]
    tm = min(tm, t)
    row = lambda width: pl.BlockSpec((tm, width), lambda i: (i, 0))
    kern = _outln_ab_kernel if len(parts) > 1 else _outln_c_kernel
    return pl.pallas_call(
        kern, grid=(t // tm,),
        in_specs=[row(p.shape[1]) for p in parts]
        + [row(D_MODEL), _const_spec(w.shape), _const_spec((1, D_MODEL)), _const_spec((1, D_MODEL))],
        out_specs=row(D_MODEL),
        out_shape=jax.ShapeDtypeStruct((t, D_MODEL), _F32),
        compiler_params=_params(("parallel",)),
        name="outproj_ln",
    )(*parts, x, w, g, b)


def _mlp_kernel(x_ref, wu_ref, wd_ref, g_ref, b_ref, y_ref, *, ff_chunk):
    x = x_ref[...]
    xb = x.astype(_BF16)
    acc = ALPHA * x
    for c in range(D_FF // ff_chunk):
        hid = _dot(xb, wu_ref[:, c * ff_chunk:(c + 1) * ff_chunk])
        hid = jnp.square(jnp.maximum(hid, 0.0)).astype(_BF16)
        acc = acc + _dot(hid, wd_ref[c * ff_chunk:(c + 1) * ff_chunk, :])
    y_ref[...] = _layer_norm(acc, g_ref[...], b_ref[...])


def _mlp(x, wu, wd, g, b, tm, ff_chunk):
    t = x.shape[0]
    tm = min(tm, t)
    row = pl.BlockSpec((tm, D_MODEL), lambda i: (i, 0))
    return pl.pallas_call(
        functools.partial(_mlp_kernel, ff_chunk=ff_chunk), grid=(t // tm,),
        in_specs=[row, _const_spec(wu.shape), _const_spec(wd.shape),
                  _const_spec((1, D_MODEL)), _const_spec((1, D_MODEL))],
        out_specs=row,
        out_shape=jax.ShapeDtypeStruct((t, D_MODEL), _F32),
        compiler_params=_params(("parallel",)),
        name="mlp_ln",
    )(x, wu, wd, g, b)


def _roll_window(old, new, t_len):
    win = old.shape[1]
    rolled = pltpu.roll(old, win - t_len, axis=1)
    new_tail = jnp.concatenate([jnp.zeros((LANES - t_len, LANES), _F32), new], axis=0).T
    tail_lane = lax.broadcasted_iota(jnp.int32, new_tail.shape, 1) >= LANES - t_len
    tail = jnp.where(tail_lane, new_tail, rolled[:, win - LANES:])
    if win == LANES:
        return tail
    return jnp.concatenate([rolled[:, :win - LANES], tail], axis=1)


def _dil_cache_kernel(q_ref, k_ref, v_ref, cache_ref, o_ref, l_ref, out_ref, *, rate, t_len):
    win = cache_ref.shape[-1]
    rows = 2 * t_len
    key_i = lax.broadcasted_iota(jnp.int32, (rows, win), 1)
    row_t = lax.broadcasted_iota(jnp.int32, (rows, win), 0) % t_len
    ok_c = (key_i >= row_t) & (((key_i - row_t) & (rate - 1)) == 0)
    n_j = lax.broadcasted_iota(jnp.int32, (rows, t_len), 1)
    n_t = lax.broadcasted_iota(jnp.int32, (rows, t_len), 0) % t_len
    ok_n = (n_j <= n_t) & (((n_t - n_j) & (rate - 1)) == 0)
    low = lax.broadcasted_iota(jnp.int32, (t_len, LANES), 1) < DH
    _for_each(cache_ref.shape[0], functools.partial(
        _dil_cache_one, q_ref, k_ref, v_ref, cache_ref, o_ref, l_ref, out_ref,
        ok_c, ok_n, low, t_len))


def _for_each(n, body):
    if n == 1:
        body(0)
    else:
        lax.fori_loop(0, n, lambda i, carry: (body(i), carry)[1], 0)


def _dil_cache_one(q_ref, k_ref, v_ref, cache_ref, o_ref, l_ref, out_ref, ok_c, ok_n, low, t_len, bi):
    win = cache_ref.shape[-1]
    rows = 2 * t_len
    lane = lax.broadcasted_iota(jnp.int32, (t_len, LANES), 1)
    lse_tile = jnp.zeros((t_len, LANES), _F32)
    for a in range(H_B // 2):
        c = a * LANES
        q2 = q_ref[bi, :, c:c + LANES] * (DH ** -0.5)
        lhs = jnp.concatenate([jnp.where(low, q2, 0.0), jnp.where(low, 0.0, q2)], axis=0).astype(_BF16)
        k_old = cache_ref[bi, 0, 2 * a:2 * a + 2].reshape(2 * DH, win)
        v_old = cache_ref[bi, 1, 2 * a:2 * a + 2].reshape(2 * DH, win)
        k_new = k_ref[bi, :, c:c + LANES]
        v_new = v_ref[bi, :, c:c + LANES]
        s_c = jnp.where(ok_c, _dot(lhs, k_old.astype(_BF16)), NEG_INF)
        s_n = jnp.where(ok_n, _nt_dot(lhs, k_new.astype(_BF16)), NEG_INF)
        m = jnp.maximum(jnp.max(s_c, axis=1, keepdims=True), jnp.max(s_n, axis=1, keepdims=True))
        p_c = jnp.exp(s_c - m)
        p_n = jnp.exp(s_n - m)
        den = jnp.sum(p_c, axis=1, keepdims=True) + jnp.sum(p_n, axis=1, keepdims=True)
        o = (_nt_dot(p_c.astype(_BF16), v_old.astype(_BF16))
             + _dot(p_n.astype(_BF16), v_new.astype(_BF16))) / den
        lse = m + jnp.log(den)
        o_ref[bi, :, c:c + LANES] = jnp.where(low, o[0:t_len], o[t_len:rows])
        lse_tile = jnp.where(lane == 2 * a, lse[0:t_len], lse_tile)
        lse_tile = jnp.where(lane == 2 * a + 1, lse[t_len:rows], lse_tile)
        for kv, old, new in ((0, k_old, k_new), (1, v_old, v_new)):
            out_ref[bi, kv, 2 * a:2 * a + 2] = _roll_window(old, new, t_len).reshape(2, DH, win)
    l_ref[bi] = lse_tile


def _dil_cache(z_s3, cache_t, *, g, rate, t_len, nb):
    bsz, _, _, _, win = cache_t.shape
    q_blk, k_blk, v_blk = 3 * g, 3 * g + 1, 3 * g + 2
    col = lambda cb: pl.BlockSpec((nb, t_len, W_B), lambda b: (b, 0, cb))
    tok = lambda width: pl.BlockSpec((nb, t_len, width), lambda b: (b, 0, 0))
    blk5 = lambda last: pl.BlockSpec((nb, 2, H_B, DH, last), lambda b: (b, 0, 0, 0, 0))
    return pl.pallas_call(
        functools.partial(_dil_cache_kernel, rate=rate, t_len=t_len), grid=(bsz // nb,),
        in_specs=[col(q_blk), col(k_blk), col(v_blk), blk5(win)],
        out_specs=[tok(W_B), tok(LANES), blk5(win)],
        out_shape=[jax.ShapeDtypeStruct((bsz, t_len, W_B), _F32),
                   jax.ShapeDtypeStruct((bsz, t_len, LANES), _F32),
                   jax.ShapeDtypeStruct(cache_t.shape, _F32)],
        compiler_params=_params(("parallel",)),
        name="dil_cache",
    )(z_s3, z_s3, z_s3, cache_t)


def _swa_sample_kernel(sink_ref, z_ref, cache_ref, o_ref, out_ref, *, t_len):
    grp = H_C // KV_C
    tiles = grp // 2
    rows = t_len * grp
    kw = KV_C * DH
    q_w = H_C * DH
    r_t = lax.broadcasted_iota(jnp.int32, (rows, WIN_C), 0) % t_len
    key_i = lax.broadcasted_iota(jnp.int32, (rows, WIN_C), 1)
    ok_c = key_i > r_t
    n_t = lax.broadcasted_iota(jnp.int32, (rows, t_len), 0) % t_len
    n_j = lax.broadcasted_iota(jnp.int32, (rows, t_len), 1)
    ok_n = n_j <= n_t
    r_h = lax.broadcasted_iota(jnp.int32, (rows, 1), 0) // t_len
    low_n = lax.broadcasted_iota(jnp.int32, (t_len, LANES), 1) < DH

    def dup(x, j):
        sw = pltpu.roll(x, DH, axis=1)
        return (jnp.where(low_n, x, sw) if j == 0 else jnp.where(low_n, sw, x)).astype(_BF16)

    sinks = []
    for j in range(KV_C):
        sk = jnp.zeros((rows, 1), _F32)
        for g in range(grp):
            sk = jnp.where(r_h == g, sink_ref[j * grp + g], sk)
        sinks.append(sk)

    def one(bi):
        k_n, v_n = z_ref[bi, :, q_w:q_w + kw], z_ref[bi, :, q_w + kw:q_w + 2 * kw]
        for kv, new in ((0, k_n), (1, v_n)):
            old = cache_ref[bi, kv].reshape(kw, WIN_C)
            out_ref[bi, kv] = _roll_window(old, new, t_len).reshape(KV_C, DH, WIN_C)
        for j in range(KV_C):
            k_old = cache_ref[bi, 0, j].astype(_BF16)
            v_old = cache_ref[bi, 1, j].astype(_BF16)
            k_dup = jnp.concatenate([k_old, k_old], axis=0)
            v_dup = jnp.concatenate([v_old, v_old], axis=0)
            parts = []
            for a in range(tiles):
                c = (j * tiles + a) * LANES
                q2 = z_ref[bi, :, c:c + LANES] * (DH ** -0.5)
                parts.append(jnp.where(low_n, q2, 0.0).astype(_BF16))
                parts.append(jnp.where(low_n, 0.0, q2).astype(_BF16))
            q = jnp.concatenate(parts, axis=0)
            sk = sinks[j]
            s_c = jnp.where(ok_c, _dot(q, k_dup), NEG_INF)
            s_n = jnp.where(ok_n, _nt_dot(q, dup(k_n, j)), NEG_INF)
            m = jnp.maximum(jnp.maximum(jnp.max(s_c, axis=1, keepdims=True),
                                        jnp.max(s_n, axis=1, keepdims=True)), sk)
            p_c = jnp.exp(s_c - m)
            p_n = jnp.exp(s_n - m)
            den = (jnp.sum(p_c, axis=1, keepdims=True) + jnp.sum(p_n, axis=1, keepdims=True)
                   + jnp.exp(sk - m))
            o = (_nt_dot(p_c.astype(_BF16), v_dup)
                 + _dot(p_n.astype(_BF16), dup(v_n, j))) / den
            for a in range(tiles):
                c = (j * tiles + a) * LANES
                even = o[(2 * a) * t_len:(2 * a + 1) * t_len, :]
                odd = o[(2 * a + 1) * t_len:(2 * a + 2) * t_len, :]
                o_ref[bi, :, c:c + LANES] = jnp.where(low_n, even, odd)

    _for_each(cache_ref.shape[0], one)


def _swa_cache(zc_s3, cache_t, sinks, *, t_len, nb):
    dbs = cache_t.shape[0]
    q_w = H_C * DH
    blk5 = lambda last: pl.BlockSpec((nb, 2, KV_C, DH, last), lambda b: (b, 0, 0, 0, 0))
    return pl.pallas_call(
        functools.partial(_swa_sample_kernel, t_len=t_len), grid=(dbs // nb,),
        in_specs=[pl.BlockSpec(memory_space=pltpu.SMEM),
                  pl.BlockSpec((nb, t_len, N_C), lambda b: (b, 0, 0)),
                  blk5(WIN_C)],
        out_specs=[pl.BlockSpec((nb, t_len, q_w), lambda b: (b, 0, 0)), blk5(WIN_C)],
        out_shape=[jax.ShapeDtypeStruct((dbs, t_len, q_w), _F32),
                   jax.ShapeDtypeStruct(cache_t.shape, _F32)],
        compiler_params=_params(("parallel",)),
        name="swa_cache",
    )(sinks, zc_s3, cache_t)


def _rope_chunks(widths_and_flags):
    out = []
    for width, flag in widths_and_flags:
        out += [flag] * (width // LANES)
    return tuple(out)


def kernel(x_prompt, x_sample, state_mlstm_C, state_mlstm_n, state_mlstm_m, cache_dil1_kv, cache_dil2_kv, cache_dil3_kv, cache_swa_kv, w_in_ab, b_gate_ab, w_out_ab, w_in_c, sinks_c, w_out_c, ln1_g, ln1_b, ln2_g, ln2_b, w_up, w_down):
    bsz, s_len, _ = x_prompt.shape
    dbs, t_len, _ = x_sample.shape
    tp = bsz * s_len
    ts = dbs * t_len
    xp = x_prompt.reshape(tp, D_MODEL)
    xs = x_sample.reshape(ts, D_MODEL)
    xp_bf = xp.astype(_BF16)
    pos_p = jnp.arange(s_len, dtype=jnp.int32)
    pos_s = PAST_LEN + jnp.arange(t_len, dtype=jnp.int32)
    tab_p = _rope_tables(pos_p)
    tab_s = tuple(jnp.tile(t, (ts // t_len, 1)) for t in _rope_tables(pos_s))
    row2 = lambda v: v.reshape(1, -1)

    w_in = w_in_ab[0]
    gate_lo = 4 * W_A
    q_lo = gate_lo + 2 * H_A
    k_lo = q_lo + N_DIL * W_B
    v_lo = k_lo + N_DIL * W_B
    w_a = jnp.concatenate([w_in[:, :gate_lo],
                           jnp.pad(w_in[:, gate_lo:q_lo], ((0, 0), (0, LANES - 2 * H_A)))],
                          axis=1).astype(_BF16)
    w_grp = [jnp.concatenate([w_in[:, lo + g * W_B:lo + (g + 1) * W_B] for lo in (q_lo, k_lo, v_lo)],
                             axis=1).astype(_BF16) for g in range(N_DIL)]
    rope_a = _rope_chunks([(gate_lo + LANES, False)])
    rope_grp = _rope_chunks([(2 * W_B, True), (W_B, False)])
    bias = jnp.pad(b_gate_ab[0], (0, LANES - 2 * H_A)).reshape(1, LANES)
    w_out0 = w_out_ab[0].astype(_BF16)

    za_p = _project(xp, w_a, tab_p, rope_a, n_seq=bsz, seq_len=s_len, rate=1, tm=1024)
    za_s = _project(xs, w_a, tab_s, rope_a, n_seq=1, seq_len=ts, rate=1, tm=1024)
    zg_s = _project(xs, jnp.concatenate(w_grp, axis=1), tab_s, rope_grp * N_DIL,
                    n_seq=1, seq_len=ts, rate=1, tm=1024)

    zeros_c = jnp.zeros((bsz, H_A, DH_A, DH_A), _F32)
    zeros_n = jnp.zeros((bsz, H_A, DH_A), _F32)
    zeros_m = jnp.zeros((bsz, 1, H_A), _F32)
    ga_p, mc_p, mn_p, mm_p = _mlstm(za_p.reshape(tp, -1), bias, zeros_c, zeros_n, zeros_m,
                                    n_seq=bsz, seq_len=s_len, chunk=256, nb=1)
    ga_s, mc_s, mn_s, mm_s = _mlstm(za_s.reshape(ts, -1), bias, state_mlstm_C[0], state_mlstm_n[0],
                                    state_mlstm_m[0].reshape(dbs, 1, H_A),
                                    n_seq=dbs, seq_len=t_len, chunk=t_len, nb=8)

    dil_caches = (cache_dil1_kv, cache_dil2_kv, cache_dil3_kv)
    outs_p, lses_p, outs_s, lses_s, dil_kv_p, dil_kv_s = [], [], [], [], [], []
    to_time_minor = lambda c: jnp.transpose(c, (0, 2, 3, 4, 1))
    to_time_major = lambda c: jnp.transpose(c, (0, 4, 1, 2, 3))[None]
    z_s3 = zg_s.reshape(dbs, t_len, N_DIL * 3 * W_B)
    for g in range(N_DIL):
        rate, win = DIL_RATES[g], DIL_WINDOWS[g]
        keep = min(win, s_len) // rate
        zg_p, tail = _project(xp if rate == 1 else xp_bf, w_grp[g], tab_p, rope_grp, n_seq=bsz,
                              seq_len=s_len, rate=rate, tm=1024, out_dtype=_BF16,
                              tail_rows=keep, tail_lo=W_B)
        o, l = _band_attention(zg_p, qcol=0, kcol=W_B, vcol=2 * W_B, qw=W_B, kw=W_B,
                               max_dist=win // rate, with_lse=True)
        outs_p.append(o)
        lses_p.append(l)
        dil_kv_p.append(jnp.transpose(tail, (0, 2, 1, 3)).reshape(1, bsz, keep * rate, 2, H_B, DH))
        o, l, rolled = _dil_cache(z_s3, to_time_minor(dil_caches[g][0]), g=g, rate=rate, t_len=t_len,
                                  nb=max(1, CACHE_STEP_BYTES // (win * 2 * W_B * 4)))
        outs_s.append(o.reshape(ts, W_B))
        lses_s.append(l.reshape(ts, LANES))
        dil_kv_s.append(to_time_major(rolled))

    xp = _outln([ga_p] + outs_p + lses_p, xp, w_out0, row2(ln1_g[0]), row2(ln1_b[0]), tm=512)
    xs = _outln([ga_s] + outs_s + lses_s, xs, w_out0, row2(ln1_g[0]), row2(ln1_b[0]), tm=512)
    wu0, wd0 = w_up[0].astype(_BF16), w_down[0].astype(_BF16)
    xp = _mlp(xp, wu0, wd0, row2(ln2_g[0]), row2(ln2_b[0]), tm=512, ff_chunk=1024)
    xs = _mlp(xs, wu0, wd0, row2(ln2_g[0]), row2(ln2_b[0]), tm=512, ff_chunk=1024)

    w_c = w_in_c[0].astype(_BF16)
    kc1 = H_C * DH
    vc1 = kc1 + KV_C * DH
    rope_c = _rope_chunks([(vc1, True), (KV_C * DH, False)])
    zc_p = _project(xp, w_c, tab_p, rope_c, n_seq=bsz, seq_len=s_len, rate=1, tm=1024)
    zc_s = _project(xs, w_c, tab_s, rope_c, n_seq=1, seq_len=ts, rate=1, tm=1024)
    (o_p,) = _band_attention(zc_p, qcol=0, kcol=kc1, vcol=vc1, qw=H_C * DH, kw=KV_C * DH,
                             max_dist=WIN_C - 1, sinks=sinks_c[0], with_lse=False)
    keep = min(WIN_C, s_len)
    swa_kv_p = zc_p.reshape(bsz, s_len, N_C)[:, s_len - keep:, kc1:].reshape(1, bsz, keep, 2, KV_C, DH)

    o_s, swa_rolled = _swa_cache(zc_s.reshape(dbs, t_len, N_C), to_time_minor(cache_swa_kv[0]),
                                 sinks_c[0], t_len=t_len, nb=8)
    o_s = o_s.reshape(ts, H_C * DH)
    swa_kv_s = to_time_major(swa_rolled)

    w_out1 = w_out_c[0].astype(_BF16)
    xp = _outln([o_p], xp, w_out1, row2(ln1_g[1]), row2(ln1_b[1]), tm=512)
    xs = _outln([o_s], xs, w_out1, row2(ln1_g[1]), row2(ln1_b[1]), tm=512)
    wu1, wd1 = w_up[1].astype(_BF16), w_down[1].astype(_BF16)
    xp = _mlp(xp, wu1, wd1, row2(ln2_g[1]), row2(ln2_b[1]), tm=512, ff_chunk=1024)
    xs = _mlp(xs, wu1, wd1, row2(ln2_g[1]), row2(ln2_b[1]), tm=512, ff_chunk=1024)

    return (xp.reshape(bsz, s_len, D_MODEL), xs.reshape(dbs, t_len, D_MODEL),
            mc_p[None], mc_s[None], mn_p[None], mn_s[None],
            mm_p.reshape(1, bsz, H_A), mm_s.reshape(1, dbs, H_A),
            dil_kv_p[0], dil_kv_s[0], dil_kv_p[1], dil_kv_s[1], dil_kv_p[2], dil_kv_s[2],
            swa_kv_p, swa_kv_s)
```

```python
import functools

import jax
import jax.numpy as jnp
from jax import lax
from jax.experimental import pallas as pl
from jax.experimental.pallas import tpu as pltpu

LANES = 128
SUBLANES = 8
MXU_COLS = 256
VMEM_LIMIT = 56 * 1024 * 1024
CACHE_STEP_BYTES = 4 * 1024 * 1024

D_MODEL = 1024
DH = 64
ROT_DIM = DH // 4
ROPE_THETA = 500000.0
PAST_LEN = 8192
BLK = 128
H_A = 4
DH_A = 128
W_A = H_A * DH_A
N_DIL = 3
DIL_WINDOWS = (128, 512, 2048)
DIL_RATES = (1, 4, 16)
H_B = 8
W_B = H_B * DH
H_C = 16
KV_C = 2
WIN_C = 128
D_FF = 4 * D_MODEL
DEPTH = 2
ALPHA = (2.0 * DEPTH) ** 0.25
LN_EPS = 1e-5
N_AB = 4 * W_A + 3 * N_DIL * W_B
N_C = (H_C + 2 * KV_C) * DH
NEG_INF = float("-inf")

_F32 = jnp.float32
_BF16 = jnp.bfloat16


def _params(sem):
    return pltpu.CompilerParams(dimension_semantics=sem, vmem_limit_bytes=VMEM_LIMIT)


def _nt_dot(a, b):
    return lax.dot_general(a, b, (((1,), (1,)), ((), ())), preferred_element_type=_F32)


def _tn_dot(a, b):
    return lax.dot_general(a, b, (((0,), (0,)), ((), ())), preferred_element_type=_F32)


def _dot(a, b):
    return jnp.dot(a, b, preferred_element_type=_F32)


def _log_sigmoid(x):
    return -(jnp.maximum(-x, 0.0) + jnp.log1p(jnp.exp(-jnp.abs(x))))


def _layer_norm(y, g, b):
    mu = jnp.mean(y, axis=-1, keepdims=True)
    yc = y - mu
    var = jnp.mean(yc * yc, axis=-1, keepdims=True)
    return yc * lax.rsqrt(var + LN_EPS) * g + b


def _rope_tables(pos):
    half = ROT_DIM // 2
    inv = ROPE_THETA ** (-jnp.arange(half, dtype=_F32) / half)
    ang = pos.astype(_F32)[:, None] * inv[None, :]
    cos = jnp.cos(ang)
    sin = jnp.sin(ang)
    n = pos.shape[0]
    ones = jnp.ones((n, DH - ROT_DIM), _F32)
    zeros = jnp.zeros((n, DH - ROT_DIM), _F32)
    zh = jnp.zeros((n, half), _F32)
    cos_h = jnp.concatenate([cos, cos, ones], axis=1)
    sinm_h = jnp.concatenate([-sin, zh, zeros], axis=1)
    sinp_h = jnp.concatenate([zh, sin, zeros], axis=1)
    tile2 = lambda t: jnp.concatenate([t, t], axis=1)
    return tile2(cos_h), tile2(sinm_h), tile2(sinp_h)


def _proj_kernel(x_ref, w_ref, cos_ref, sinm_ref, sinp_ref, o_ref, *tail_refs, rope_chunks, tail_lo):
    out = o_ref.at[0, 0]
    xb = x_ref[...].astype(_BF16)
    tm = xb.shape[0]
    n = w_ref.shape[1]
    half = ROT_DIM // 2
    for c0 in range(0, n, MXU_COLS):
        width = min(MXU_COLS, n - c0)
        acc = _dot(xb, w_ref[:, c0:c0 + width])
        for cc in range(width // LANES):
            lo = c0 + cc * LANES
            sub = acc[:, cc * LANES:(cc + 1) * LANES]
            if rope_chunks[lo // LANES]:
                sub = (sub * cos_ref[...]
                       + pltpu.roll(sub, LANES - half, axis=1) * sinm_ref[...]
                       + pltpu.roll(sub, half, axis=1) * sinp_ref[...])
            out[:, lo:lo + LANES] = sub.astype(o_ref.dtype)
            if tail_refs and lo >= tail_lo:
                tail = tail_refs[0]
                tail[0, 0, :, lo - tail_lo:lo - tail_lo + LANES] = sub[tm - tail.shape[2]:, :]


def _project(x, w, tables, rope_chunks, *, n_seq, seq_len, rate, tm, out_dtype=_F32,
             tail_rows=0, tail_lo=0):
    d = x.shape[1]
    n = w.shape[1]
    assert len(rope_chunks) * LANES == n
    length = seq_len // rate
    tm = min(tm, length)
    nj = length // tm
    xv = x.reshape(n_seq * length, rate * d)
    tabs = [t.reshape(length, rate * LANES) for t in tables]
    row_tab = pl.BlockSpec((tm, LANES), lambda b, r, j: (j, r))
    out_specs = [pl.BlockSpec((1, 1, tm, n), lambda b, r, j: (b, r, j, 0))]
    out_shape = [jax.ShapeDtypeStruct((n_seq, rate, length, n), out_dtype)]
    if tail_rows:
        out_specs.append(pl.BlockSpec((1, 1, tail_rows, n - tail_lo), lambda b, r, j: (b, r, 0, 0)))
        out_shape.append(jax.ShapeDtypeStruct((n_seq, rate, tail_rows, n - tail_lo), _F32))
    res = pl.pallas_call(
        functools.partial(_proj_kernel, rope_chunks=tuple(rope_chunks), tail_lo=tail_lo),
        grid=(n_seq, rate, nj),
        in_specs=[pl.BlockSpec((tm, d), lambda b, r, j: (b * nj + j, r)),
                  _const_spec(w.shape), row_tab, row_tab, row_tab],
        out_specs=out_specs, out_shape=out_shape,
        compiler_params=_params(("parallel", "parallel", "arbitrary")),
        name="proj",
    )(xv, w, *tabs)
    return res if tail_rows else res[0]


def _mlstm_kernel(q_ref, k_ref, v_ref, oa_ref, g_ref, bias_ref, c0_ref, n0_ref, m0_ref,
                  ga_ref, c_ref, n_ref, m_ref, *, chunk, nb):
    c_idx = pl.program_id(1)

    @pl.when(c_idx == 0)
    def _():
        c_ref[...] = c0_ref[...]
        n_ref[...] = n0_ref[...]
        m_ref[...] = m0_ref[...]

    row = lax.broadcasted_iota(jnp.int32, (chunk, chunk), 0)
    col = lax.broadcasted_iota(jnp.int32, (chunk, chunk), 1)
    eye = row == col
    tril = col <= row

    def to_row(colvec):
        return jnp.sum(jnp.where(eye, colvec, 0.0), axis=0, keepdims=True)

    for bi in range(nb):
        r0 = bi * chunk
        gates = g_ref[r0:r0 + chunk, :] + bias_ref[...]
        for h in range(H_A):
            lo = h * DH_A
            q = q_ref[r0:r0 + chunk, lo:lo + DH_A]
            k = k_ref[r0:r0 + chunk, lo:lo + DH_A] * (DH_A ** -0.5)
            v = v_ref[r0:r0 + chunk, lo:lo + DH_A]
            oa = oa_ref[r0:r0 + chunk, lo:lo + DH_A]
            c_prev = c_ref[bi, h]
            n_prev = n_ref[bi, h:h + 1, :]
            m_prev = m_ref[bi, :, h:h + 1]
            li_col = gates[:, h:h + 1]
            lf_col = _log_sigmoid(gates[:, H_A + h:H_A + h + 1])
            li_row = to_row(li_col)
            lf_row = to_row(lf_col)
            b_col = jnp.sum(jnp.where(tril, lf_row, 0.0), axis=1, keepdims=True)
            b_row = to_row(b_col)
            dmat = jnp.where(tril, b_col - b_row + li_row, NEG_INF)
            a_col = b_col + m_prev
            mt = jnp.maximum(a_col, jnp.max(dmat, axis=1, keepdims=True))
            wts = jnp.exp(dmat - mt)
            inter = jnp.exp(a_col - mt)
            qb = q.astype(_BF16)
            kb = k.astype(_BF16)
            vb = v.astype(_BF16)
            sqk = _nt_dot(qb, kb) * wts
            num = inter * _dot(qb, c_prev.astype(_BF16)) + _dot(sqk.astype(_BF16), vb)
            nq = (inter * jnp.sum(q * n_prev, axis=1, keepdims=True)
                  + jnp.sum(sqk, axis=1, keepdims=True))
            hid = num / jnp.maximum(jnp.abs(nq), jnp.exp(-mt))
            ga_ref[r0:r0 + chunk, lo:lo + DH_A] = hid * jax.nn.sigmoid(oa)
            b_last = b_col[chunk - 1:chunk, :]
            mt_last = mt[chunk - 1:chunk, :]
            w_last = jnp.exp(b_last - b_col + li_col - mt_last)
            inter_last = inter[chunk - 1:chunk, :]
            kw = k * w_last
            c_ref[bi, h] = inter_last * c_prev + _tn_dot(kw.astype(_BF16), vb)
            n_ref[bi, h:h + 1, :] = inter_last * n_prev + jnp.sum(kw, axis=0, keepdims=True)
            m_ref[bi, :, h:h + 1] = mt_last


def _mlstm(z, bias, c0, n0, m0, *, n_seq, seq_len, chunk, nb):
    t = z.shape[0]
    nc = seq_len // chunk
    rows = nb * chunk
    col = lambda cb: pl.BlockSpec((rows, W_A), lambda b, c: (b * nc + c, cb))
    state_c = pl.BlockSpec((nb, H_A, DH_A, DH_A), lambda b, c: (b, 0, 0, 0))
    state_n = pl.BlockSpec((nb, H_A, DH_A), lambda b, c: (b, 0, 0))
    state_m = pl.BlockSpec((nb, 1, H_A), lambda b, c: (b, 0, 0))
    return pl.pallas_call(
        functools.partial(_mlstm_kernel, chunk=chunk, nb=nb),
        grid=(n_seq // nb, nc),
        in_specs=[col(0), col(1), col(2), col(3),
                  pl.BlockSpec((rows, LANES), lambda b, c: (b * nc + c, 4 * W_A // LANES)),
                  pl.BlockSpec((1, LANES), lambda b, c: (0, 0)),
                  state_c, state_n, state_m],
        out_specs=[pl.BlockSpec((rows, W_A), lambda b, c: (b * nc + c, 0)),
                   state_c, state_n, state_m],
        out_shape=[jax.ShapeDtypeStruct((t, W_A), _F32),
                   jax.ShapeDtypeStruct((n_seq, H_A, DH_A, DH_A), _F32),
                   jax.ShapeDtypeStruct((n_seq, H_A, DH_A), _F32),
                   jax.ShapeDtypeStruct((n_seq, 1, H_A), _F32)],
        compiler_params=_params(("parallel", "arbitrary")),
        name="mlstm",
    )(z, z, z, z, z, bias, c0, n0, m0)


def _band_kernel(*refs, max_dist, n_qtiles, kv_shared, with_sinks, with_lse):
    idx = 0
    if with_sinks:
        sink_ref = refs[0]
        idx = 1
    q_ref, k_ref, v_ref = refs[idx:idx + 3]
    idx += 3
    o_ref = refs[idx]
    idx += 1
    if with_lse:
        l_ref = refs[idx]
        idx += 1
    kprev_ref, vprev_ref = refs[idx:idx + 2]
    j = pl.program_id(2)
    n_kv = kprev_ref.shape[0]

    @pl.when(j == 0)
    def _():
        kprev_ref[...] = jnp.zeros_like(kprev_ref)
        vprev_ref[...] = jnp.zeros_like(vprev_ref)

    lane = lax.broadcasted_iota(jnp.int32, (BLK, LANES), 1)
    low = lane < DH
    if kv_shared:
        k_in, v_in = k_ref[0, 0], v_ref[0, 0]
        low_kv = lax.broadcasted_iota(jnp.int32, k_in.shape, 1) < DH
        k_sw = pltpu.roll(k_in, DH, axis=1)
        v_sw = pltpu.roll(v_in, DH, axis=1)
        k_tiles = [jnp.where(low_kv, k_in, k_sw), jnp.where(low_kv, k_sw, k_in)]
        v_tiles = [jnp.where(low_kv, v_in, v_sw), jnp.where(low_kv, v_sw, v_in)]
    else:
        k_tiles = [k_ref[0, 0, :, t * LANES:(t + 1) * LANES] for t in range(n_kv)]
        v_tiles = [v_ref[0, 0, :, t * LANES:(t + 1) * LANES] for t in range(n_kv)]
    k_tiles = [t.astype(_BF16) for t in k_tiles]
    v_tiles = [t.astype(_BF16) for t in v_tiles]

    qi = lax.broadcasted_iota(jnp.int32, (BLK, 2 * BLK), 0) + BLK
    ki = lax.broadcasted_iota(jnp.int32, (BLK, 2 * BLK), 1)
    dist = qi - ki
    band = (dist >= 0) & (dist <= max_dist)
    first_key = jnp.where(j > 0, 0, BLK)
    n_sub = q_ref.shape[2] // BLK

    tiles_per_kv = n_qtiles // n_kv
    for sb in range(n_sub):
        r0 = sb * BLK
        valid = band & (ki >= first_key) if sb == 0 else band
        q_all = q_ref[0, 0, r0:r0 + BLK, :] * (DH ** -0.5)
        lse_tile = jnp.zeros((BLK, LANES), _F32)
        for kt in range(n_kv):
            if sb == 0:
                k2 = jnp.concatenate([kprev_ref[kt], k_tiles[kt][0:BLK]], axis=0)
                v2 = jnp.concatenate([vprev_ref[kt], v_tiles[kt][0:BLK]], axis=0)
            else:
                k2 = k_tiles[kt][r0 - BLK:r0 + BLK]
                v2 = v_tiles[kt][r0 - BLK:r0 + BLK]
            parts = []
            for p in range(kt * tiles_per_kv, (kt + 1) * tiles_per_kv):
                q2 = q_all[:, p * LANES:(p + 1) * LANES]
                parts.append(jnp.where(low, q2, 0.0).astype(_BF16))
                parts.append(jnp.where(low, 0.0, q2).astype(_BF16))
            s_all = _nt_dot(jnp.concatenate(parts, axis=0), k2)
            probs, dens, lses = [], [], []
            for hh in range(2 * tiles_per_kv):
                s = jnp.where(valid, s_all[hh * BLK:(hh + 1) * BLK, :], NEG_INF)
                m = jnp.max(s, axis=1, keepdims=True)
                if with_sinks:
                    sk = sink_ref[2 * kt * tiles_per_kv + hh]
                    m = jnp.maximum(m, sk)
                p_exp = jnp.exp(s - m)
                den = jnp.sum(p_exp, axis=1, keepdims=True)
                if with_sinks:
                    den = den + jnp.exp(sk - m)
                probs.append(p_exp.astype(_BF16))
                dens.append(den)
                lses.append(m + jnp.log(den))
            o_all = _dot(jnp.concatenate(probs, axis=0), v2)
            for pp in range(tiles_per_kv):
                p = kt * tiles_per_kv + pp
                e, o = 2 * pp, 2 * pp + 1
                o_even = o_all[e * BLK:(e + 1) * BLK, :] / dens[e]
                o_odd = o_all[o * BLK:(o + 1) * BLK, :] / dens[o]
                o_ref[0, r0:r0 + BLK, p * LANES:(p + 1) * LANES] = jnp.where(low, o_even, o_odd)
                if with_lse:
                    lse_tile = jnp.where(lane == 2 * p, lses[e], lse_tile)
                    lse_tile = jnp.where(lane == 2 * p + 1, lses[o], lse_tile)
        if with_lse:
            l_ref[0, r0:r0 + BLK, :] = lse_tile

    last = (n_sub - 1) * BLK
    for kt in range(n_kv):
        kprev_ref[kt] = k_tiles[kt][last:last + BLK]
        vprev_ref[kt] = v_tiles[kt][last:last + BLK]


def _band_attention(z4, *, qcol, kcol, vcol, qw, kw, max_dist, sinks=None, with_lse):
    n_seq, rate, length, _ = z4.shape
    seq_len = length * rate
    nblk = length // BLK
    n_sub = 2 if nblk % 2 == 0 else 1
    rows = n_sub * BLK
    kv_shared = kw == LANES and qw > LANES
    n_kv = 2 if kv_shared else kw // LANES

    def spec(width, coff):
        assert coff % width == 0
        return pl.BlockSpec((1, 1, rows, width), lambda b, r, j: (b, r, j, coff // width))

    out_spec = pl.BlockSpec((1, rows, qw), lambda b, r, j: (b, j, r))
    out_shape = [jax.ShapeDtypeStruct((n_seq, length, rate * qw), _F32)]
    out_specs = [out_spec]
    if with_lse:
        out_shape.append(jax.ShapeDtypeStruct((n_seq, length, rate * LANES), _F32))
        out_specs.append(pl.BlockSpec((1, rows, LANES), lambda b, r, j: (b, j, r)))
    in_specs = [spec(qw, qcol), spec(kw, kcol), spec(kw, vcol)]
    args = [z4, z4, z4]
    if sinks is not None:
        in_specs = [pl.BlockSpec(memory_space=pltpu.SMEM)] + in_specs
        args = [sinks] + args
    res = pl.pallas_call(
        functools.partial(_band_kernel, max_dist=max_dist, n_qtiles=qw // LANES,
                          kv_shared=kv_shared, with_sinks=sinks is not None, with_lse=with_lse),
        grid=(n_seq, rate, nblk // n_sub),
        in_specs=in_specs, out_specs=out_specs, out_shape=out_shape,
        scratch_shapes=[pltpu.VMEM((n_kv, BLK, LANES), _BF16),
                        pltpu.VMEM((n_kv, BLK, LANES), _BF16)],
        compiler_params=_params(("parallel", "parallel", "arbitrary")),
        name="band_attn",
    )(*args)
    return [r.reshape(n_seq * seq_len, -1) for r in res]


def _mix_ab(ga_ref, o1_ref, o2_ref, o3_ref, l1_ref, l2_ref, l3_ref, w_ref):
    l1, l2, l3 = l1_ref[...], l2_ref[...], l3_ref[...]
    lmax = jnp.maximum(jnp.maximum(l1, l2), l3)
    e1, e2, e3 = jnp.exp(l1 - lmax), jnp.exp(l2 - lmax), jnp.exp(l3 - lmax)
    inv = 1.0 / (e1 + e2 + e3)
    wts = (e1 * inv, e2 * inv, e3 * inv)
    outs = (o1_ref, o2_ref, o3_ref)
    low = lax.broadcasted_iota(jnp.int32, (l1.shape[0], LANES), 1) < DH
    tiles = []
    for p in range(H_B // 2):
        c = p * LANES
        ob = sum(jnp.where(low, wg[:, 2 * p:2 * p + 1], wg[:, 2 * p + 1:2 * p + 2]) * og[:, c:c + LANES]
                 for wg, og in zip(wts, outs))
        tiles.append(ob)
    return (_dot(ga_ref[...].astype(_BF16), w_ref[0:W_A, :])
            + _dot(jnp.concatenate(tiles, axis=1).astype(_BF16), w_ref[W_A:W_A + W_B, :]))


def _post_kernel(*refs, n_parts, ff_chunk):
    parts = refs[:n_parts]
    x_ref, w_ref, g1_ref, b1_ref, wu_ref, wd_ref, g2_ref, b2_ref, y_ref = refs[n_parts:]
    if n_parts == 1:
        mix = _dot(parts[0][...].astype(_BF16), w_ref[...])
    else:
        mix = _mix_ab(*parts, w_ref)
    x1 = _layer_norm(ALPHA * x_ref[...] + mix, g1_ref[...], b1_ref[...])
    xb = x1.astype(_BF16)
    acc = ALPHA * x1
    for c in range(D_FF // ff_chunk):
        hid = _dot(xb, wu_ref[:, c * ff_chunk:(c + 1) * ff_chunk])
        hid = jnp.square(jnp.maximum(hid, 0.0)).astype(_BF16)
        acc = acc + _dot(hid, wd_ref[c * ff_chunk:(c + 1) * ff_chunk, :])
    y_ref[...] = _layer_norm(acc, g2_ref[...], b2_ref[...])


def _const_spec(shape):
    return pl.BlockSpec(shape, lambda *_: (0,) * len(shape), pipeline_mode=pl.Buffered(1))


def _post_mixer(parts, x, w_out, ln1, wu, wd, ln2, *, tm, ff_chunk):
    t = x.shape[0]
    tm = min(tm, t)
    row = lambda width: pl.BlockSpec((tm, width), lambda i: (i, 0))
    vec = _const_spec((1, D_MODEL))
    return pl.pallas_call(
        functools.partial(_post_kernel, n_parts=len(parts), ff_chunk=ff_chunk), grid=(t // tm,),
        in_specs=[row(p.shape[1]) for p in parts]
        + [row(D_MODEL), _const_spec(w_out.shape), vec, vec,
           _const_spec(wu.shape), _const_spec(wd.shape), vec, vec],
        out_specs=row(D_MODEL),
        out_shape=jax.ShapeDtypeStruct((t, D_MODEL), _F32),
        compiler_params=_params(("parallel",)),
        name="post_mixer",
    )(*parts, x, w_out, *ln1, wu, wd, *ln2)


def _roll_window(old, new, t_len):
    win = old.shape[1]
    rolled = pltpu.roll(old, win - t_len, axis=1)
    new_tail = jnp.concatenate([jnp.zeros((LANES - t_len, LANES), _F32), new], axis=0).T
    tail_lane = lax.broadcasted_iota(jnp.int32, new_tail.shape, 1) >= LANES - t_len
    tail = jnp.where(tail_lane, new_tail, rolled[:, win - LANES:])
    if win == LANES:
        return tail
    return jnp.concatenate([rolled[:, :win - LANES], tail], axis=1)


def _dil_cache_kernel(q_ref, k_ref, v_ref, cache_ref, o_ref, l_ref, out_ref, *, rate, t_len):
    win = cache_ref.shape[-1]
    rows = 2 * t_len
    key_i = lax.broadcasted_iota(jnp.int32, (rows, win), 1)
    row_t = lax.broadcasted_iota(jnp.int32, (rows, win), 0) % t_len
    ok_c = (key_i >= row_t) & (((key_i - row_t) & (rate - 1)) == 0)
    n_j = lax.broadcasted_iota(jnp.int32, (rows, t_len), 1)
    n_t = lax.broadcasted_iota(jnp.int32, (rows, t_len), 0) % t_len
    ok_n = (n_j <= n_t) & (((n_t - n_j) & (rate - 1)) == 0)
    low = lax.broadcasted_iota(jnp.int32, (t_len, LANES), 1) < DH
    _for_each(cache_ref.shape[0], functools.partial(
        _dil_cache_one, q_ref, k_ref, v_ref, cache_ref, o_ref, l_ref, out_ref,
        ok_c, ok_n, low, t_len))


def _for_each(n, body):
    if n == 1:
        body(0)
    else:
        lax.fori_loop(0, n, lambda i, carry: (body(i), carry)[1], 0)


def _dil_cache_one(q_ref, k_ref, v_ref, cache_ref, o_ref, l_ref, out_ref, ok_c, ok_n, low, t_len, bi):
    win = cache_ref.shape[-1]
    rows = 2 * t_len
    lane = lax.broadcasted_iota(jnp.int32, (t_len, LANES), 1)
    lse_tile = jnp.zeros((t_len, LANES), _F32)
    for a in range(H_B // 2):
        c = a * LANES
        q2 = q_ref[bi, :, c:c + LANES] * (DH ** -0.5)
        lhs = jnp.concatenate([jnp.where(low, q2, 0.0), jnp.where(low, 0.0, q2)], axis=0).astype(_BF16)
        k_old = cache_ref[bi, 0, 2 * a:2 * a + 2].reshape(2 * DH, win)
        v_old = cache_ref[bi, 1, 2 * a:2 * a + 2].reshape(2 * DH, win)
        k_new = k_ref[bi, :, c:c + LANES]
        v_new = v_ref[bi, :, c:c + LANES]
        s_c = jnp.where(ok_c, _dot(lhs, k_old.astype(_BF16)), NEG_INF)
        s_n = jnp.where(ok_n, _nt_dot(lhs, k_new.astype(_BF16)), NEG_INF)
        m = jnp.maximum(jnp.max(s_c, axis=1, keepdims=True), jnp.max(s_n, axis=1, keepdims=True))
        p_c = jnp.exp(s_c - m)
        p_n = jnp.exp(s_n - m)
        den = jnp.sum(p_c, axis=1, keepdims=True) + jnp.sum(p_n, axis=1, keepdims=True)
        o = (_nt_dot(p_c.astype(_BF16), v_old.astype(_BF16))
             + _dot(p_n.astype(_BF16), v_new.astype(_BF16))) / den
        lse = m + jnp.log(den)
        o_ref[bi, :, c:c + LANES] = jnp.where(low, o[0:t_len], o[t_len:rows])
        lse_tile = jnp.where(lane == 2 * a, lse[0:t_len], lse_tile)
        lse_tile = jnp.where(lane == 2 * a + 1, lse[t_len:rows], lse_tile)
        for kv, old, new in ((0, k_old, k_new), (1, v_old, v_new)):
            out_ref[bi, kv, 2 * a:2 * a + 2] = _roll_window(old, new, t_len).reshape(2, DH, win)
    l_ref[bi] = lse_tile


def _dil_cache(z_s3, cache_t, *, g, rate, t_len, nb):
    bsz, _, _, _, win = cache_t.shape
    q_blk, k_blk, v_blk = 3 * g, 3 * g + 1, 3 * g + 2
    col = lambda cb: pl.BlockSpec((nb, t_len, W_B), lambda b: (b, 0, cb))
    tok = lambda width: pl.BlockSpec((nb, t_len, width), lambda b: (b, 0, 0))
    blk5 = lambda last: pl.BlockSpec((nb, 2, H_B, DH, last), lambda b: (b, 0, 0, 0, 0))
    return pl.pallas_call(
        functools.partial(_dil_cache_kernel, rate=rate, t_len=t_len), grid=(bsz // nb,),
        in_specs=[col(q_blk), col(k_blk), col(v_blk), blk5(win)],
        out_specs=[tok(W_B), tok(LANES), blk5(win)],
        out_shape=[jax.ShapeDtypeStruct((bsz, t_len, W_B), _F32),
                   jax.ShapeDtypeStruct((bsz, t_len, LANES), _F32),
                   jax.ShapeDtypeStruct(cache_t.shape, _F32)],
        compiler_params=_params(("parallel",)),
        name="dil_cache",
    )(z_s3, z_s3, z_s3, cache_t)


def _swa_sample_kernel(sink_ref, z_ref, cache_ref, o_ref, out_ref, *, t_len):
    grp = H_C // KV_C
    tiles = grp // 2
    rows = t_len * grp
    kw = KV_C * DH
    q_w = H_C * DH
    r_t = lax.broadcasted_iota(jnp.int32, (rows, WIN_C), 0) % t_len
    key_i = lax.broadcasted_iota(jnp.int32, (rows, WIN_C), 1)
    ok_c = key_i > r_t
    n_t = lax.broadcasted_iota(jnp.int32, (rows, t_len), 0) % t_len
    n_j = lax.broadcasted_iota(jnp.int32, (rows, t_len), 1)
    ok_n = n_j <= n_t
    r_h = lax.broadcasted_iota(jnp.int32, (rows, 1), 0) // t_len
    low_n = lax.broadcasted_iota(jnp.int32, (t_len, LANES), 1) < DH

    def dup(x, j):
        sw = pltpu.roll(x, DH, axis=1)
        return (jnp.where(low_n, x, sw) if j == 0 else jnp.where(low_n, sw, x)).astype(_BF16)

    sinks = []
    for j in range(KV_C):
        sk = jnp.zeros((rows, 1), _F32)
        for g in range(grp):
            sk = jnp.where(r_h == g, sink_ref[j * grp + g], sk)
        sinks.append(sk)

    def one(bi):
        k_n, v_n = z_ref[bi, :, q_w:q_w + kw], z_ref[bi, :, q_w + kw:q_w + 2 * kw]
        for kv, new in ((0, k_n), (1, v_n)):
            old = cache_ref[bi, kv].reshape(kw, WIN_C)
            out_ref[bi, kv] = _roll_window(old, new, t_len).reshape(KV_C, DH, WIN_C)
        for j in range(KV_C):
            k_old = cache_ref[bi, 0, j].astype(_BF16)
            v_old = cache_ref[bi, 1, j].astype(_BF16)
            k_dup = jnp.concatenate([k_old, k_old], axis=0)
            v_dup = jnp.concatenate([v_old, v_old], axis=0)
            parts = []
            for a in range(tiles):
                c = (j * tiles + a) * LANES
                q2 = z_ref[bi, :, c:c + LANES] * (DH ** -0.5)
                parts.append(jnp.where(low_n, q2, 0.0).astype(_BF16))
                parts.append(jnp.where(low_n, 0.0, q2).astype(_BF16))
            q = jnp.concatenate(parts, axis=0)
            sk = sinks[j]
            s_c = jnp.where(ok_c, _dot(q, k_dup), NEG_INF)
            s_n = jnp.where(ok_n, _nt_dot(q, dup(k_n, j)), NEG_INF)
            m = jnp.maximum(jnp.maximum(jnp.max(s_c, axis=1, keepdims=True),
                                        jnp.max(s_n, axis=1, keepdims=True)), sk)
            p_c = jnp.exp(s_c - m)
            p_n = jnp.exp(s_n - m)
            den = (jnp.sum(p_c, axis=1, keepdims=True) + jnp.sum(p_n, axis=1, keepdims=True)
                   + jnp.exp(sk - m))
            o = (_nt_dot(p_c.astype(_BF16), v_dup)
                 + _dot(p_n.astype(_BF16), dup(v_n, j))) / den
            for a in range(tiles):
                c = (j * tiles + a) * LANES
                even = o[(2 * a) * t_len:(2 * a + 1) * t_len, :]
                odd = o[(2 * a + 1) * t_len:(2 * a + 2) * t_len, :]
                o_ref[bi, :, c:c + LANES] = jnp.where(low_n, even, odd)

    _for_each(cache_ref.shape[0], one)


def _swa_cache(zc_s3, cache_t, sinks, *, t_len, nb):
    dbs = cache_t.shape[0]
    q_w = H_C * DH
    blk5 = lambda last: pl.BlockSpec((nb, 2, KV_C, DH, last), lambda b: (b, 0, 0, 0, 0))
    return pl.pallas_call(
        functools.partial(_swa_sample_kernel, t_len=t_len), grid=(dbs // nb,),
        in_specs=[pl.BlockSpec(memory_space=pltpu.SMEM),
                  pl.BlockSpec((nb, t_len, N_C), lambda b: (b, 0, 0)),
                  blk5(WIN_C)],
        out_specs=[pl.BlockSpec((nb, t_len, q_w), lambda b: (b, 0, 0)), blk5(WIN_C)],
        out_shape=[jax.ShapeDtypeStruct((dbs, t_len, q_w), _F32),
                   jax.ShapeDtypeStruct(cache_t.shape, _F32)],
        compiler_params=_params(("parallel",)),
        name="swa_cache",
    )(sinks, zc_s3, cache_t)


def _rope_chunks(widths_and_flags):
    out = []
    for width, flag in widths_and_flags:
        out += [flag] * (width // LANES)
    return tuple(out)


def kernel(x_prompt, x_sample, state_mlstm_C, state_mlstm_n, state_mlstm_m, cache_dil1_kv, cache_dil2_kv, cache_dil3_kv, cache_swa_kv, w_in_ab, b_gate_ab, w_out_ab, w_in_c, sinks_c, w_out_c, ln1_g, ln1_b, ln2_g, ln2_b, w_up, w_down):
    bsz, s_len, _ = x_prompt.shape
    dbs, t_len, _ = x_sample.shape
    tp = bsz * s_len
    ts = dbs * t_len
    xp = x_prompt.reshape(tp, D_MODEL)
    xs = x_sample.reshape(ts, D_MODEL)
    xp_bf = xp.astype(_BF16)
    pos_p = jnp.arange(s_len, dtype=jnp.int32)
    pos_s = PAST_LEN + jnp.arange(t_len, dtype=jnp.int32)
    tab_p = _rope_tables(pos_p)
    tab_s = tuple(jnp.tile(t, (ts // t_len, 1)) for t in _rope_tables(pos_s))
    row2 = lambda v: v.reshape(1, -1)

    w_in = w_in_ab[0]
    gate_lo = 4 * W_A
    q_lo = gate_lo + 2 * H_A
    k_lo = q_lo + N_DIL * W_B
    v_lo = k_lo + N_DIL * W_B
    w_a = jnp.concatenate([w_in[:, :gate_lo],
                           jnp.pad(w_in[:, gate_lo:q_lo], ((0, 0), (0, LANES - 2 * H_A)))],
                          axis=1).astype(_BF16)
    w_grp = [jnp.concatenate([w_in[:, lo + g * W_B:lo + (g + 1) * W_B] for lo in (q_lo, k_lo, v_lo)],
                             axis=1).astype(_BF16) for g in range(N_DIL)]
    rope_a = _rope_chunks([(gate_lo + LANES, False)])
    rope_grp = _rope_chunks([(2 * W_B, True), (W_B, False)])
    bias = jnp.pad(b_gate_ab[0], (0, LANES - 2 * H_A)).reshape(1, LANES)
    w_out0 = w_out_ab[0].astype(_BF16)

    za_p = _project(xp, w_a, tab_p, rope_a, n_seq=bsz, seq_len=s_len, rate=1, tm=1024)
    za_s = _project(xs, w_a, tab_s, rope_a, n_seq=1, seq_len=ts, rate=1, tm=1024)
    zg_s = _project(xs, jnp.concatenate(w_grp, axis=1), tab_s, rope_grp * N_DIL,
                    n_seq=1, seq_len=ts, rate=1, tm=1024)

    zeros_c = jnp.zeros((bsz, H_A, DH_A, DH_A), _F32)
    zeros_n = jnp.zeros((bsz, H_A, DH_A), _F32)
    zeros_m = jnp.zeros((bsz, 1, H_A), _F32)
    ga_p, mc_p, mn_p, mm_p = _mlstm(za_p.reshape(tp, -1), bias, zeros_c, zeros_n, zeros_m,
                                    n_seq=bsz, seq_len=s_len, chunk=256, nb=1)
    ga_s, mc_s, mn_s, mm_s = _mlstm(za_s.reshape(ts, -1), bias, state_mlstm_C[0], state_mlstm_n[0],
                                    state_mlstm_m[0].reshape(dbs, 1, H_A),
                                    n_seq=dbs, seq_len=t_len, chunk=t_len, nb=8)

    dil_caches = (cache_dil1_kv, cache_dil2_kv, cache_dil3_kv)
    outs_p, lses_p, outs_s, lses_s, dil_kv_p, dil_kv_s = [], [], [], [], [], []
    to_time_minor = lambda c: jnp.transpose(c, (0, 2, 3, 4, 1))
    to_time_major = lambda c: jnp.transpose(c, (0, 4, 1, 2, 3))[None]
    z_s3 = zg_s.reshape(dbs, t_len, N_DIL * 3 * W_B)
    for g in range(N_DIL):
        rate, win = DIL_RATES[g], DIL_WINDOWS[g]
        keep = min(win, s_len) // rate
        zg_p, tail = _project(xp if rate == 1 else xp_bf, w_grp[g], tab_p, rope_grp, n_seq=bsz,
                              seq_len=s_len, rate=rate, tm=1024, out_dtype=_BF16,
                              tail_rows=keep, tail_lo=W_B)
        o, l = _band_attention(zg_p, qcol=0, kcol=W_B, vcol=2 * W_B, qw=W_B, kw=W_B,
                               max_dist=win // rate, with_lse=True)
        outs_p.append(o)
        lses_p.append(l)
        dil_kv_p.append(jnp.transpose(tail, (0, 2, 1, 3)).reshape(1, bsz, keep * rate, 2, H_B, DH))
        o, l, rolled = _dil_cache(z_s3, to_time_minor(dil_caches[g][0]), g=g, rate=rate, t_len=t_len,
                                  nb=max(1, CACHE_STEP_BYTES // (win * 2 * W_B * 4)))
        outs_s.append(o.reshape(ts, W_B))
        lses_s.append(l.reshape(ts, LANES))
        dil_kv_s.append(to_time_major(rolled))

    post = functools.partial(_post_mixer, tm=512, ff_chunk=1024)
    wu0, wd0 = w_up[0].astype(_BF16), w_down[0].astype(_BF16)
    ln1_0, ln2_0 = (row2(ln1_g[0]), row2(ln1_b[0])), (row2(ln2_g[0]), row2(ln2_b[0]))
    xp = post([ga_p] + outs_p + lses_p, xp, w_out0, ln1_0, wu0, wd0, ln2_0)
    xs = post([ga_s] + outs_s + lses_s, xs, w_out0, ln1_0, wu0, wd0, ln2_0)

    w_c = w_in_c[0].astype(_BF16)
    kc1 = H_C * DH
    vc1 = kc1 + KV_C * DH
    rope_c = _rope_chunks([(vc1, True), (KV_C * DH, False)])
    zc_p = _project(xp, w_c, tab_p, rope_c, n_seq=bsz, seq_len=s_len, rate=1, tm=1024)
    zc_s = _project(xs, w_c, tab_s, rope_c, n_seq=1, seq_len=ts, rate=1, tm=1024)
    (o_p,) = _band_attention(zc_p, qcol=0, kcol=kc1, vcol=vc1, qw=H_C * DH, kw=KV_C * DH,
                             max_dist=WIN_C - 1, sinks=sinks_c[0], with_lse=False)
    keep = min(WIN_C, s_len)
    swa_kv_p = zc_p.reshape(bsz, s_len, N_C)[:, s_len - keep:, kc1:].reshape(1, bsz, keep, 2, KV_C, DH)

    o_s, swa_rolled = _swa_cache(zc_s.reshape(dbs, t_len, N_C), to_time_minor(cache_swa_kv[0]),
                                 sinks_c[0], t_len=t_len, nb=8)
    o_s = o_s.reshape(ts, H_C * DH)
    swa_kv_s = to_time_major(swa_rolled)

    w_out1 = w_out_c[0].astype(_BF16)
    wu1, wd1 = w_up[1].astype(_BF16), w_down[1].astype(_BF16)
    ln1_1, ln2_1 = (row2(ln1_g[1]), row2(ln1_b[1])), (row2(ln2_g[1]), row2(ln2_b[1]))
    xp = post([o_p], xp, w_out1, ln1_1, wu1, wd1, ln2_1)
    xs = post([o_s], xs, w_out1, ln1_1, wu1, wd1, ln2_1)

    return (xp.reshape(bsz, s_len, D_MODEL), xs.reshape(dbs, t_len, D_MODEL),
            mc_p[None], mc_s[None], mn_p[None], mn_s[None],
            mm_p.reshape(1, bsz, H_A), mm_s.reshape(1, dbs, H_A),
            dil_kv_p[0], dil_kv_s[0], dil_kv_p[1], dil_kv_s[1], dil_kv_p[2], dil_kv_s[2],
            swa_kv_p, swa_kv_s)
```

```python
import functools

import jax
import jax.numpy as jnp
from jax import lax
from jax.experimental import pallas as pl
from jax.experimental.pallas import tpu as pltpu

LANES = 128
SUBLANES = 8
MXU_COLS = 256
VMEM_LIMIT = 56 * 1024 * 1024
CACHE_STEP_BYTES = 4 * 1024 * 1024

D_MODEL = 1024
DH = 64
ROT_DIM = DH // 4
ROPE_THETA = 500000.0
PAST_LEN = 8192
BLK = 128
H_A = 4
DH_A = 128
W_A = H_A * DH_A
N_DIL = 3
DIL_WINDOWS = (128, 512, 2048)
DIL_RATES = (1, 4, 16)
H_B = 8
W_B = H_B * DH
H_C = 16
KV_C = 2
WIN_C = 128
D_FF = 4 * D_MODEL
DEPTH = 2
ALPHA = (2.0 * DEPTH) ** 0.25
LN_EPS = 1e-5
N_AB = 4 * W_A + 3 * N_DIL * W_B
N_C = (H_C + 2 * KV_C) * DH
NEG_INF = float("-inf")

_F32 = jnp.float32
_BF16 = jnp.bfloat16


def _params(sem):
    return pltpu.CompilerParams(dimension_semantics=sem, vmem_limit_bytes=VMEM_LIMIT)


def _nt_dot(a, b):
    return lax.dot_general(a, b, (((1,), (1,)), ((), ())), preferred_element_type=_F32)


def _tn_dot(a, b):
    return lax.dot_general(a, b, (((0,), (0,)), ((), ())), preferred_element_type=_F32)


def _dot(a, b):
    return jnp.dot(a, b, preferred_element_type=_F32)


def _log_sigmoid(x):
    return -(jnp.maximum(-x, 0.0) + jnp.log1p(jnp.exp(-jnp.abs(x))))


def _layer_norm(y, g, b):
    mu = jnp.mean(y, axis=-1, keepdims=True)
    yc = y - mu
    var = jnp.mean(yc * yc, axis=-1, keepdims=True)
    return yc * lax.rsqrt(var + LN_EPS) * g + b


def _rope_tables(pos):
    half = ROT_DIM // 2
    inv = ROPE_THETA ** (-jnp.arange(half, dtype=_F32) / half)
    ang = pos.astype(_F32)[:, None] * inv[None, :]
    cos = jnp.cos(ang)
    sin = jnp.sin(ang)
    n = pos.shape[0]
    ones = jnp.ones((n, DH - ROT_DIM), _F32)
    zeros = jnp.zeros((n, DH - ROT_DIM), _F32)
    zh = jnp.zeros((n, half), _F32)
    cos_h = jnp.concatenate([cos, cos, ones], axis=1)
    sinm_h = jnp.concatenate([-sin, zh, zeros], axis=1)
    sinp_h = jnp.concatenate([zh, sin, zeros], axis=1)
    tile2 = lambda t: jnp.concatenate([t, t], axis=1)
    return tile2(cos_h), tile2(sinm_h), tile2(sinp_h)


def _proj_kernel(x_ref, w_ref, cos_ref, sinm_ref, sinp_ref, o_ref, *tail_refs, rope_chunks, tail_lo):
    out = o_ref.at[0, 0]
    xb = x_ref[...].astype(_BF16)
    tm = xb.shape[0]
    n = w_ref.shape[1]
    half = ROT_DIM // 2
    for c0 in range(0, n, MXU_COLS):
        width = min(MXU_COLS, n - c0)
        acc = _dot(xb, w_ref[:, c0:c0 + width])
        for cc in range(width // LANES):
            lo = c0 + cc * LANES
            sub = acc[:, cc * LANES:(cc + 1) * LANES]
            if rope_chunks[lo // LANES]:
                sub = (sub * cos_ref[...]
                       + pltpu.roll(sub, LANES - half, axis=1) * sinm_ref[...]
                       + pltpu.roll(sub, half, axis=1) * sinp_ref[...])
            out[:, lo:lo + LANES] = sub.astype(o_ref.dtype)
            if tail_refs and lo >= tail_lo:
                tail = tail_refs[0]
                tail[0, 0, :, lo - tail_lo:lo - tail_lo + LANES] = sub[tm - tail.shape[2]:, :]


def _project(x, w, tables, rope_chunks, *, n_seq, seq_len, rate, tm, out_dtype=_F32,
             tail_rows=0, tail_lo=0):
    plan = _proj_plan(x, w, tables, rope_chunks, n_seq=n_seq, seq_len=seq_len, rate=rate, tm=tm,
                      out_dtype=out_dtype, tail_rows=tail_rows, tail_lo=tail_lo)
    in_specs, out_specs = plan["specs"](lambda b, r, j: (b, r, j))
    res = pl.pallas_call(
        plan["kernel"], grid=plan["grid"], in_specs=in_specs,
        out_specs=out_specs, out_shape=plan["out_shape"],
        compiler_params=_params(("parallel", "parallel", "arbitrary")),
        name="proj",
    )(*plan["args"])
    return res if tail_rows else res[0]


def _proj_plan(x, w, tables, rope_chunks, *, n_seq, seq_len, rate, tm, out_dtype, tail_rows, tail_lo):
    d = x.shape[1]
    n = w.shape[1]
    assert len(rope_chunks) * LANES == n
    length = seq_len // rate
    tm = min(tm, length)
    nj = length // tm
    xv = x.reshape(n_seq * length, rate * d)
    tabs = [t.reshape(length, rate * LANES) for t in tables]
    out_shape = [jax.ShapeDtypeStruct((n_seq, rate, length, n), out_dtype)]
    if tail_rows:
        out_shape.append(jax.ShapeDtypeStruct((n_seq, rate, tail_rows, n - tail_lo), _F32))

    def specs(dec):
        def at(fn):
            return lambda *idx: fn(*dec(*idx))
        row_tab = pl.BlockSpec((tm, LANES), at(lambda b, r, j: (j, r)))
        in_specs = [pl.BlockSpec((tm, d), at(lambda b, r, j: (b * nj + j, r))),
                    _const_spec(w.shape), row_tab, row_tab, row_tab]
        out_specs = [pl.BlockSpec((1, 1, tm, n), at(lambda b, r, j: (b, r, j, 0)))]
        if tail_rows:
            out_specs.append(pl.BlockSpec((1, 1, tail_rows, n - tail_lo),
                                          at(lambda b, r, j: (b, r, 0, 0))))
        return in_specs, out_specs

    return dict(kernel=functools.partial(_proj_kernel, rope_chunks=tuple(rope_chunks), tail_lo=tail_lo),
                grid=(n_seq, rate, nj), specs=specs, out_shape=out_shape, args=[xv, w, *tabs])


def _mlstm_kernel(q_ref, k_ref, v_ref, oa_ref, g_ref, bias_ref, c0_ref, n0_ref, m0_ref,
                  ga_ref, c_ref, n_ref, m_ref, *, chunk, nb):
    c_idx = pl.program_id(1)

    @pl.when(c_idx == 0)
    def _():
        c_ref[...] = c0_ref[...]
        n_ref[...] = n0_ref[...]
        m_ref[...] = m0_ref[...]

    row = lax.broadcasted_iota(jnp.int32, (chunk, chunk), 0)
    col = lax.broadcasted_iota(jnp.int32, (chunk, chunk), 1)
    eye = row == col
    tril = col <= row

    def to_row(colvec):
        return jnp.sum(jnp.where(eye, colvec, 0.0), axis=0, keepdims=True)

    for bi in range(nb):
        r0 = bi * chunk
        gates = g_ref[r0:r0 + chunk, :] + bias_ref[...]
        for h in range(H_A):
            lo = h * DH_A
            q = q_ref[r0:r0 + chunk, lo:lo + DH_A]
            k = k_ref[r0:r0 + chunk, lo:lo + DH_A] * (DH_A ** -0.5)
            v = v_ref[r0:r0 + chunk, lo:lo + DH_A]
            oa = oa_ref[r0:r0 + chunk, lo:lo + DH_A]
            c_prev = c_ref[bi, h]
            n_prev = n_ref[bi, h:h + 1, :]
            m_prev = m_ref[bi, :, h:h + 1]
            li_col = gates[:, h:h + 1]
            lf_col = _log_sigmoid(gates[:, H_A + h:H_A + h + 1])
            li_row = to_row(li_col)
            lf_row = to_row(lf_col)
            b_col = jnp.sum(jnp.where(tril, lf_row, 0.0), axis=1, keepdims=True)
            b_row = to_row(b_col)
            dmat = jnp.where(tril, b_col - b_row + li_row, NEG_INF)
            a_col = b_col + m_prev
            mt = jnp.maximum(a_col, jnp.max(dmat, axis=1, keepdims=True))
            wts = jnp.exp(dmat - mt)
            inter = jnp.exp(a_col - mt)
            qb = q.astype(_BF16)
            kb = k.astype(_BF16)
            vb = v.astype(_BF16)
            sqk = _nt_dot(qb, kb) * wts
            num = inter * _dot(qb, c_prev.astype(_BF16)) + _dot(sqk.astype(_BF16), vb)
            nq = (inter * jnp.sum(q * n_prev, axis=1, keepdims=True)
                  + jnp.sum(sqk, axis=1, keepdims=True))
            hid = num / jnp.maximum(jnp.abs(nq), jnp.exp(-mt))
            ga_ref[r0:r0 + chunk, lo:lo + DH_A] = hid * jax.nn.sigmoid(oa)
            b_last = b_col[chunk - 1:chunk, :]
            mt_last = mt[chunk - 1:chunk, :]
            w_last = jnp.exp(b_last - b_col + li_col - mt_last)
            inter_last = inter[chunk - 1:chunk, :]
            kw = k * w_last
            c_ref[bi, h] = inter_last * c_prev + _tn_dot(kw.astype(_BF16), vb)
            n_ref[bi, h:h + 1, :] = inter_last * n_prev + jnp.sum(kw, axis=0, keepdims=True)
            m_ref[bi, :, h:h + 1] = mt_last


def _mlstm(z, bias, c0, n0, m0, *, n_seq, seq_len, chunk, nb):
    t = z.shape[0]
    nc = seq_len // chunk
    rows = nb * chunk
    col = lambda cb: pl.BlockSpec((rows, W_A), lambda b, c: (b * nc + c, cb))
    state_c = pl.BlockSpec((nb, H_A, DH_A, DH_A), lambda b, c: (b, 0, 0, 0))
    state_n = pl.BlockSpec((nb, H_A, DH_A), lambda b, c: (b, 0, 0))
    state_m = pl.BlockSpec((nb, 1, H_A), lambda b, c: (b, 0, 0))
    return pl.pallas_call(
        functools.partial(_mlstm_kernel, chunk=chunk, nb=nb),
        grid=(n_seq // nb, nc),
        in_specs=[col(0), col(1), col(2), col(3),
                  pl.BlockSpec((rows, LANES), lambda b, c: (b * nc + c, 4 * W_A // LANES)),
                  pl.BlockSpec((1, LANES), lambda b, c: (0, 0)),
                  state_c, state_n, state_m],
        out_specs=[pl.BlockSpec((rows, W_A), lambda b, c: (b * nc + c, 0)),
                   state_c, state_n, state_m],
        out_shape=[jax.ShapeDtypeStruct((t, W_A), _F32),
                   jax.ShapeDtypeStruct((n_seq, H_A, DH_A, DH_A), _F32),
                   jax.ShapeDtypeStruct((n_seq, H_A, DH_A), _F32),
                   jax.ShapeDtypeStruct((n_seq, 1, H_A), _F32)],
        compiler_params=_params(("parallel", "arbitrary")),
        name="mlstm",
    )(z, z, z, z, z, bias, c0, n0, m0)


def _band_kernel(*refs, max_dist, n_qtiles, kv_shared, with_sinks, with_lse):
    idx = 0
    if with_sinks:
        sink_ref = refs[0]
        idx = 1
    q_ref, k_ref, v_ref = refs[idx:idx + 3]
    idx += 3
    o_ref = refs[idx]
    idx += 1
    if with_lse:
        l_ref = refs[idx]
        idx += 1
    kprev_ref, vprev_ref = refs[idx:idx + 2]
    j = pl.program_id(2)
    n_kv = kprev_ref.shape[0]

    @pl.when(j == 0)
    def _():
        kprev_ref[...] = jnp.zeros_like(kprev_ref)
        vprev_ref[...] = jnp.zeros_like(vprev_ref)

    lane = lax.broadcasted_iota(jnp.int32, (BLK, LANES), 1)
    low = lane < DH
    if kv_shared:
        k_in, v_in = k_ref[0, 0], v_ref[0, 0]
        low_kv = lax.broadcasted_iota(jnp.int32, k_in.shape, 1) < DH
        k_sw = pltpu.roll(k_in, DH, axis=1)
        v_sw = pltpu.roll(v_in, DH, axis=1)
        k_tiles = [jnp.where(low_kv, k_in, k_sw), jnp.where(low_kv, k_sw, k_in)]
        v_tiles = [jnp.where(low_kv, v_in, v_sw), jnp.where(low_kv, v_sw, v_in)]
    else:
        k_tiles = [k_ref[0, 0, :, t * LANES:(t + 1) * LANES] for t in range(n_kv)]
        v_tiles = [v_ref[0, 0, :, t * LANES:(t + 1) * LANES] for t in range(n_kv)]
    k_tiles = [t.astype(_BF16) for t in k_tiles]
    v_tiles = [t.astype(_BF16) for t in v_tiles]

    qi = lax.broadcasted_iota(jnp.int32, (BLK, 2 * BLK), 0) + BLK
    ki = lax.broadcasted_iota(jnp.int32, (BLK, 2 * BLK), 1)
    dist = qi - ki
    band = (dist >= 0) & (dist <= max_dist)
    first_key = jnp.where(j > 0, 0, BLK)
    n_sub = q_ref.shape[2] // BLK

    tiles_per_kv = n_qtiles // n_kv
    for sb in range(n_sub):
        r0 = sb * BLK
        valid = band & (ki >= first_key) if sb == 0 else band
        q_all = q_ref[0, 0, r0:r0 + BLK, :] * (DH ** -0.5)
        lse_tile = jnp.zeros((BLK, LANES), _F32)
        for kt in range(n_kv):
            if sb == 0:
                k2 = jnp.concatenate([kprev_ref[kt], k_tiles[kt][0:BLK]], axis=0)
                v2 = jnp.concatenate([vprev_ref[kt], v_tiles[kt][0:BLK]], axis=0)
            else:
                k2 = k_tiles[kt][r0 - BLK:r0 + BLK]
                v2 = v_tiles[kt][r0 - BLK:r0 + BLK]
            parts = []
            for p in range(kt * tiles_per_kv, (kt + 1) * tiles_per_kv):
                q2 = q_all[:, p * LANES:(p + 1) * LANES]
                parts.append(jnp.where(low, q2, 0.0).astype(_BF16))
                parts.append(jnp.where(low, 0.0, q2).astype(_BF16))
            s_all = _nt_dot(jnp.concatenate(parts, axis=0), k2)
            probs, dens, lses = [], [], []
            for hh in range(2 * tiles_per_kv):
                s = jnp.where(valid, s_all[hh * BLK:(hh + 1) * BLK, :], NEG_INF)
                m = jnp.max(s, axis=1, keepdims=True)
                if with_sinks:
                    sk = sink_ref[2 * kt * tiles_per_kv + hh]
                    m = jnp.maximum(m, sk)
                p_exp = jnp.exp(s - m)
                den = jnp.sum(p_exp, axis=1, keepdims=True)
                if with_sinks:
                    den = den + jnp.exp(sk - m)
                probs.append(p_exp.astype(_BF16))
                dens.append(den)
                lses.append(m + jnp.log(den))
            o_all = _dot(jnp.concatenate(probs, axis=0), v2)
            for pp in range(tiles_per_kv):
                p = kt * tiles_per_kv + pp
                e, o = 2 * pp, 2 * pp + 1
                o_even = o_all[e * BLK:(e + 1) * BLK, :] / dens[e]
                o_odd = o_all[o * BLK:(o + 1) * BLK, :] / dens[o]
                o_ref[0, r0:r0 + BLK, p * LANES:(p + 1) * LANES] = jnp.where(low, o_even, o_odd)
                if with_lse:
                    lse_tile = jnp.where(lane == 2 * p, lses[e], lse_tile)
                    lse_tile = jnp.where(lane == 2 * p + 1, lses[o], lse_tile)
        if with_lse:
            l_ref[0, r0:r0 + BLK, :] = lse_tile

    last = (n_sub - 1) * BLK
    for kt in range(n_kv):
        kprev_ref[kt] = k_tiles[kt][last:last + BLK]
        vprev_ref[kt] = v_tiles[kt][last:last + BLK]


def _band_attention(z4, *, qcol, kcol, vcol, qw, kw, max_dist, sinks=None, with_lse):
    n_seq, rate, length, _ = z4.shape
    seq_len = length * rate
    nblk = length // BLK
    n_sub = 2 if nblk % 2 == 0 else 1
    rows = n_sub * BLK
    kv_shared = kw == LANES and qw > LANES
    n_kv = 2 if kv_shared else kw // LANES

    def spec(width, coff):
        assert coff % width == 0
        return pl.BlockSpec((1, 1, rows, width), lambda b, r, j: (b, r, j, coff // width))

    out_spec = pl.BlockSpec((1, rows, qw), lambda b, r, j: (b, j, r))
    out_shape = [jax.ShapeDtypeStruct((n_seq, length, rate * qw), _F32)]
    out_specs = [out_spec]
    if with_lse:
        out_shape.append(jax.ShapeDtypeStruct((n_seq, length, rate * LANES), _F32))
        out_specs.append(pl.BlockSpec((1, rows, LANES), lambda b, r, j: (b, j, r)))
    in_specs = [spec(qw, qcol), spec(kw, kcol), spec(kw, vcol)]
    args = [z4, z4, z4]
    if sinks is not None:
        in_specs = [pl.BlockSpec(memory_space=pltpu.SMEM)] + in_specs
        args = [sinks] + args
    res = pl.pallas_call(
        functools.partial(_band_kernel, max_dist=max_dist, n_qtiles=qw // LANES,
                          kv_shared=kv_shared, with_sinks=sinks is not None, with_lse=with_lse),
        grid=(n_seq, rate, nblk // n_sub),
        in_specs=in_specs, out_specs=out_specs, out_shape=out_shape,
        scratch_shapes=[pltpu.VMEM((n_kv, BLK, LANES), _BF16),
                        pltpu.VMEM((n_kv, BLK, LANES), _BF16)],
        compiler_params=_params(("parallel", "parallel", "arbitrary")),
        name="band_attn",
    )(*args)
    return [r.reshape(n_seq * seq_len, -1) for r in res]


def _mix_ab(ga_ref, o1_ref, o2_ref, o3_ref, l1_ref, l2_ref, l3_ref, w_ref):
    l1, l2, l3 = l1_ref[...], l2_ref[...], l3_ref[...]
    lmax = jnp.maximum(jnp.maximum(l1, l2), l3)
    e1, e2, e3 = jnp.exp(l1 - lmax), jnp.exp(l2 - lmax), jnp.exp(l3 - lmax)
    inv = 1.0 / (e1 + e2 + e3)
    wts = (e1 * inv, e2 * inv, e3 * inv)
    outs = (o1_ref, o2_ref, o3_ref)
    low = lax.broadcasted_iota(jnp.int32, (l1.shape[0], LANES), 1) < DH
    tiles = []
    for p in range(H_B // 2):
        c = p * LANES
        ob = sum(jnp.where(low, wg[:, 2 * p:2 * p + 1], wg[:, 2 * p + 1:2 * p + 2]) * og[:, c:c + LANES]
                 for wg, og in zip(wts, outs))
        tiles.append(ob)
    return (_dot(ga_ref[...].astype(_BF16), w_ref[0:W_A, :])
            + _dot(jnp.concatenate(tiles, axis=1).astype(_BF16), w_ref[W_A:W_A + W_B, :]))


def _post_kernel(*refs, n_parts, ff_chunk):
    parts = refs[:n_parts]
    x_ref, w_ref, g1_ref, b1_ref, wu_ref, wd_ref, g2_ref, b2_ref, y_ref = refs[n_parts:]
    if n_parts == 1:
        mix = _dot(parts[0][...].astype(_BF16), w_ref[...])
    else:
        mix = _mix_ab(*parts, w_ref)
    x1 = _layer_norm(ALPHA * x_ref[...] + mix, g1_ref[...], b1_ref[...])
    xb = x1.astype(_BF16)
    acc = ALPHA * x1
    for c in range(D_FF // ff_chunk):
        hid = _dot(xb, wu_ref[:, c * ff_chunk:(c + 1) * ff_chunk])
        hid = jnp.square(jnp.maximum(hid, 0.0)).astype(_BF16)
        acc = acc + _dot(hid, wd_ref[c * ff_chunk:(c + 1) * ff_chunk, :])
    y_ref[...] = _layer_norm(acc, g2_ref[...], b2_ref[...])


def _const_spec(shape):
    return pl.BlockSpec(shape, lambda *_: (0,) * len(shape), pipeline_mode=pl.Buffered(1))


def _post_mixer(parts, x, w_out, ln1, wu, wd, ln2, *, tm, ff_chunk):
    t = x.shape[0]
    tm = min(tm, t)
    row = lambda width: pl.BlockSpec((tm, width), lambda i: (i, 0))
    vec = _const_spec((1, D_MODEL))
    return pl.pallas_call(
        functools.partial(_post_kernel, n_parts=len(parts), ff_chunk=ff_chunk), grid=(t // tm,),
        in_specs=[row(p.shape[1]) for p in parts]
        + [row(D_MODEL), _const_spec(w_out.shape), vec, vec,
           _const_spec(wu.shape), _const_spec(wd.shape), vec, vec],
        out_specs=row(D_MODEL),
        out_shape=jax.ShapeDtypeStruct((t, D_MODEL), _F32),
        compiler_params=_params(("parallel",)),
        name="post_mixer",
    )(*parts, x, w_out, *ln1, wu, wd, *ln2)


def _roll_window(old, new, t_len):
    win = old.shape[1]
    rolled = pltpu.roll(old, win - t_len, axis=1)
    new_tail = jnp.concatenate([jnp.zeros((LANES - t_len, LANES), _F32), new], axis=0).T
    tail_lane = lax.broadcasted_iota(jnp.int32, new_tail.shape, 1) >= LANES - t_len
    tail = jnp.where(tail_lane, new_tail, rolled[:, win - LANES:])
    if win == LANES:
        return tail
    return jnp.concatenate([rolled[:, :win - LANES], tail], axis=1)


def _dil_cache_kernel(q_ref, k_ref, v_ref, cache_ref, o_ref, l_ref, out_ref, *, rate, t_len):
    win = cache_ref.shape[-1]
    rows = 2 * t_len
    key_i = lax.broadcasted_iota(jnp.int32, (rows, win), 1)
    row_t = lax.broadcasted_iota(jnp.int32, (rows, win), 0) % t_len
    ok_c = (key_i >= row_t) & (((key_i - row_t) & (rate - 1)) == 0)
    n_j = lax.broadcasted_iota(jnp.int32, (rows, t_len), 1)
    n_t = lax.broadcasted_iota(jnp.int32, (rows, t_len), 0) % t_len
    ok_n = (n_j <= n_t) & (((n_t - n_j) & (rate - 1)) == 0)
    low = lax.broadcasted_iota(jnp.int32, (t_len, LANES), 1) < DH
    _for_each(cache_ref.shape[0], functools.partial(
        _dil_cache_one, q_ref, k_ref, v_ref, cache_ref, o_ref, l_ref, out_ref,
        ok_c, ok_n, low, t_len))


def _for_each(n, body):
    if n == 1:
        body(0)
    else:
        lax.fori_loop(0, n, lambda i, carry: (body(i), carry)[1], 0)


def _dil_cache_one(q_ref, k_ref, v_ref, cache_ref, o_ref, l_ref, out_ref, ok_c, ok_n, low, t_len, bi):
    win = cache_ref.shape[-1]
    rows = 2 * t_len
    lane = lax.broadcasted_iota(jnp.int32, (t_len, LANES), 1)
    lse_tile = jnp.zeros((t_len, LANES), _F32)
    for a in range(H_B // 2):
        c = a * LANES
        q2 = q_ref[bi, :, c:c + LANES] * (DH ** -0.5)
        lhs = jnp.concatenate([jnp.where(low, q2, 0.0), jnp.where(low, 0.0, q2)], axis=0).astype(_BF16)
        k_old = cache_ref[bi, 0, 2 * a:2 * a + 2].reshape(2 * DH, win)
        v_old = cache_ref[bi, 1, 2 * a:2 * a + 2].reshape(2 * DH, win)
        k_new = k_ref[bi, :, c:c + LANES]
        v_new = v_ref[bi, :, c:c + LANES]
        s_c = jnp.where(ok_c, _dot(lhs, k_old.astype(_BF16)), NEG_INF)
        s_n = jnp.where(ok_n, _nt_dot(lhs, k_new.astype(_BF16)), NEG_INF)
        m = jnp.maximum(jnp.max(s_c, axis=1, keepdims=True), jnp.max(s_n, axis=1, keepdims=True))
        p_c = jnp.exp(s_c - m)
        p_n = jnp.exp(s_n - m)
        den = jnp.sum(p_c, axis=1, keepdims=True) + jnp.sum(p_n, axis=1, keepdims=True)
        o = (_nt_dot(p_c.astype(_BF16), v_old.astype(_BF16))
             + _dot(p_n.astype(_BF16), v_new.astype(_BF16))) / den
        lse = m + jnp.log(den)
        o_ref[bi, :, c:c + LANES] = jnp.where(low, o[0:t_len], o[t_len:rows])
        lse_tile = jnp.where(lane == 2 * a, lse[0:t_len], lse_tile)
        lse_tile = jnp.where(lane == 2 * a + 1, lse[t_len:rows], lse_tile)
        for kv, old, new in ((0, k_old, k_new), (1, v_old, v_new)):
            out_ref[bi, kv, 2 * a:2 * a + 2] = _roll_window(old, new, t_len).reshape(2, DH, win)
    l_ref[bi] = lse_tile


def _dil_cache_hosting_kernel(*refs, rate, t_len, hosted):
    _dil_cache_kernel(*refs[:4], *refs[4 + 5 * len(hosted):7 + 5 * len(hosted)], rate=rate, t_len=t_len)
    n_out = 7 + 5 * len(hosted)
    for h, (proj_kernel, n_outs) in enumerate(hosted):
        proj_kernel(*refs[4 + 5 * h:9 + 5 * h], *refs[n_out:n_out + n_outs])
        n_out += n_outs


def _dil_cache(z_s3, cache_t, *, g, rate, t_len, nb, hosted_plans=()):
    bsz, _, _, _, win = cache_t.shape
    steps = bsz // nb
    q_blk, k_blk, v_blk = 3 * g, 3 * g + 1, 3 * g + 2
    col = lambda cb: pl.BlockSpec((nb, t_len, W_B), lambda b: (b, 0, cb))
    tok = lambda width: pl.BlockSpec((nb, t_len, width), lambda b: (b, 0, 0))
    blk5 = lambda last: pl.BlockSpec((nb, 2, H_B, DH, last), lambda b: (b, 0, 0, 0, 0))
    in_specs = [col(q_blk), col(k_blk), col(v_blk), blk5(win)]
    out_specs = [tok(W_B), tok(LANES), blk5(win)]
    out_shape = [jax.ShapeDtypeStruct((bsz, t_len, W_B), _F32),
                 jax.ShapeDtypeStruct((bsz, t_len, LANES), _F32),
                 jax.ShapeDtypeStruct(cache_t.shape, _F32)]
    args = [z_s3, z_s3, z_s3, cache_t]
    hosted = []
    for plan in hosted_plans:
        n_seq, p_rate, nj = plan["grid"]
        assert n_seq * p_rate * nj == steps
        dec = lambda i, p_rate=p_rate, nj=nj: (i // (p_rate * nj), (i // nj) % p_rate, i % nj)
        p_in, p_out = plan["specs"](dec)
        in_specs += p_in
        out_specs += p_out
        out_shape += plan["out_shape"]
        args += plan["args"]
        hosted.append((plan["kernel"], len(p_out)))
    return pl.pallas_call(
        functools.partial(_dil_cache_hosting_kernel, rate=rate, t_len=t_len, hosted=tuple(hosted)),
        grid=(steps,), in_specs=in_specs, out_specs=out_specs, out_shape=out_shape,
        compiler_params=_params(("arbitrary",)),
        name="dil_cache",
    )(*args)


def _swa_sample_kernel(sink_ref, z_ref, cache_ref, o_ref, out_ref, *, t_len):
    grp = H_C // KV_C
    tiles = grp // 2
    rows = t_len * grp
    kw = KV_C * DH
    q_w = H_C * DH
    r_t = lax.broadcasted_iota(jnp.int32, (rows, WIN_C), 0) % t_len
    key_i = lax.broadcasted_iota(jnp.int32, (rows, WIN_C), 1)
    ok_c = key_i > r_t
    n_t = lax.broadcasted_iota(jnp.int32, (rows, t_len), 0) % t_len
    n_j = lax.broadcasted_iota(jnp.int32, (rows, t_len), 1)
    ok_n = n_j <= n_t
    r_h = lax.broadcasted_iota(jnp.int32, (rows, 1), 0) // t_len
    low_n = lax.broadcasted_iota(jnp.int32, (t_len, LANES), 1) < DH

    def dup(x, j):
        sw = pltpu.roll(x, DH, axis=1)
        return (jnp.where(low_n, x, sw) if j == 0 else jnp.where(low_n, sw, x)).astype(_BF16)

    sinks = []
    for j in range(KV_C):
        sk = jnp.zeros((rows, 1), _F32)
        for g in range(grp):
            sk = jnp.where(r_h == g, sink_ref[j * grp + g], sk)
        sinks.append(sk)

    def one(bi):
        k_n, v_n = z_ref[bi, :, q_w:q_w + kw], z_ref[bi, :, q_w + kw:q_w + 2 * kw]
        for kv, new in ((0, k_n), (1, v_n)):
            old = cache_ref[bi, kv].reshape(kw, WIN_C)
            out_ref[bi, kv] = _roll_window(old, new, t_len).reshape(KV_C, DH, WIN_C)
        for j in range(KV_C):
            k_old = cache_ref[bi, 0, j].astype(_BF16)
            v_old = cache_ref[bi, 1, j].astype(_BF16)
            k_dup = jnp.concatenate([k_old, k_old], axis=0)
            v_dup = jnp.concatenate([v_old, v_old], axis=0)
            parts = []
            for a in range(tiles):
                c = (j * tiles + a) * LANES
                q2 = z_ref[bi, :, c:c + LANES] * (DH ** -0.5)
                parts.append(jnp.where(low_n, q2, 0.0).astype(_BF16))
                parts.append(jnp.where(low_n, 0.0, q2).astype(_BF16))
            q = jnp.concatenate(parts, axis=0)
            sk = sinks[j]
            s_c = jnp.where(ok_c, _dot(q, k_dup), NEG_INF)
            s_n = jnp.where(ok_n, _nt_dot(q, dup(k_n, j)), NEG_INF)
            m = jnp.maximum(jnp.maximum(jnp.max(s_c, axis=1, keepdims=True),
                                        jnp.max(s_n, axis=1, keepdims=True)), sk)
            p_c = jnp.exp(s_c - m)
            p_n = jnp.exp(s_n - m)
            den = (jnp.sum(p_c, axis=1, keepdims=True) + jnp.sum(p_n, axis=1, keepdims=True)
                   + jnp.exp(sk - m))
            o = (_nt_dot(p_c.astype(_BF16), v_dup)
                 + _dot(p_n.astype(_BF16), dup(v_n, j))) / den
            for a in range(tiles):
                c = (j * tiles + a) * LANES
                even = o[(2 * a) * t_len:(2 * a + 1) * t_len, :]
                odd = o[(2 * a + 1) * t_len:(2 * a + 2) * t_len, :]
                o_ref[bi, :, c:c + LANES] = jnp.where(low_n, even, odd)

    _for_each(cache_ref.shape[0], one)


def _swa_cache(zc_s3, cache_t, sinks, *, t_len, nb):
    dbs = cache_t.shape[0]
    q_w = H_C * DH
    blk5 = lambda last: pl.BlockSpec((nb, 2, KV_C, DH, last), lambda b: (b, 0, 0, 0, 0))
    return pl.pallas_call(
        functools.partial(_swa_sample_kernel, t_len=t_len), grid=(dbs // nb,),
        in_specs=[pl.BlockSpec(memory_space=pltpu.SMEM),
                  pl.BlockSpec((nb, t_len, N_C), lambda b: (b, 0, 0)),
                  blk5(WIN_C)],
        out_specs=[pl.BlockSpec((nb, t_len, q_w), lambda b: (b, 0, 0)), blk5(WIN_C)],
        out_shape=[jax.ShapeDtypeStruct((dbs, t_len, q_w), _F32),
                   jax.ShapeDtypeStruct(cache_t.shape, _F32)],
        compiler_params=_params(("parallel",)),
        name="swa_cache",
    )(sinks, zc_s3, cache_t)


def _rope_chunks(widths_and_flags):
    out = []
    for width, flag in widths_and_flags:
        out += [flag] * (width // LANES)
    return tuple(out)


def kernel(x_prompt, x_sample, state_mlstm_C, state_mlstm_n, state_mlstm_m, cache_dil1_kv, cache_dil2_kv, cache_dil3_kv, cache_swa_kv, w_in_ab, b_gate_ab, w_out_ab, w_in_c, sinks_c, w_out_c, ln1_g, ln1_b, ln2_g, ln2_b, w_up, w_down):
    bsz, s_len, _ = x_prompt.shape
    dbs, t_len, _ = x_sample.shape
    tp = bsz * s_len
    ts = dbs * t_len
    xp = x_prompt.reshape(tp, D_MODEL)
    xs = x_sample.reshape(ts, D_MODEL)
    xp_bf = xp.astype(_BF16)
    pos_p = jnp.arange(s_len, dtype=jnp.int32)
    pos_s = PAST_LEN + jnp.arange(t_len, dtype=jnp.int32)
    tab_p = _rope_tables(pos_p)
    tab_s = tuple(jnp.tile(t, (ts // t_len, 1)) for t in _rope_tables(pos_s))
    row2 = lambda v: v.reshape(1, -1)

    w_in = w_in_ab[0]
    gate_lo = 4 * W_A
    q_lo = gate_lo + 2 * H_A
    k_lo = q_lo + N_DIL * W_B
    v_lo = k_lo + N_DIL * W_B
    w_a = jnp.concatenate([w_in[:, :gate_lo],
                           jnp.pad(w_in[:, gate_lo:q_lo], ((0, 0), (0, LANES - 2 * H_A)))],
                          axis=1).astype(_BF16)
    w_grp = [jnp.concatenate([w_in[:, lo + g * W_B:lo + (g + 1) * W_B] for lo in (q_lo, k_lo, v_lo)],
                             axis=1).astype(_BF16) for g in range(N_DIL)]
    rope_a = _rope_chunks([(gate_lo + LANES, False)])
    rope_grp = _rope_chunks([(2 * W_B, True), (W_B, False)])
    bias = jnp.pad(b_gate_ab[0], (0, LANES - 2 * H_A)).reshape(1, LANES)
    w_out0 = w_out_ab[0].astype(_BF16)

    za_s = _project(xs, w_a, tab_s, rope_a, n_seq=1, seq_len=ts, rate=1, tm=1024)
    zg_s = _project(xs, jnp.concatenate(w_grp, axis=1), tab_s, rope_grp * N_DIL,
                    n_seq=1, seq_len=ts, rate=1, tm=1024)

    ga_s, mc_s, mn_s, mm_s = _mlstm(za_s.reshape(ts, -1), bias, state_mlstm_C[0], state_mlstm_n[0],
                                    state_mlstm_m[0].reshape(dbs, 1, H_A),
                                    n_seq=dbs, seq_len=t_len, chunk=t_len, nb=8)

    dil_caches = (cache_dil1_kv, cache_dil2_kv, cache_dil3_kv)
    outs_p, lses_p, outs_s, lses_s, dil_kv_p, dil_kv_s = [], [], [], [], [], []
    to_time_minor = lambda c: jnp.transpose(c, (0, 2, 3, 4, 1))
    to_time_major = lambda c: jnp.transpose(c, (0, 4, 1, 2, 3))[None]
    z_s3 = zg_s.reshape(dbs, t_len, N_DIL * 3 * W_B)
    host_g = N_DIL - 1
    host_nb = max(1, CACHE_STEP_BYTES // (DIL_WINDOWS[host_g] * 2 * W_B * 4))
    can_host = tp // BLK == dbs // host_nb
    hosted_groups = [g for g in range(N_DIL) if DIL_RATES[g] > 1 and can_host]
    proj_kw = dict(n_seq=bsz, seq_len=s_len, out_dtype=_BF16, tail_lo=W_B)
    keeps = [min(DIL_WINDOWS[g], s_len) // DIL_RATES[g] for g in range(N_DIL)]
    plans = [_proj_plan(xp_bf, w_grp[g], tab_p, rope_grp, rate=DIL_RATES[g], tm=BLK,
                        tail_rows=keeps[g], **proj_kw) for g in hosted_groups]
    a_kw = dict(n_seq=bsz, seq_len=s_len, rate=1, out_dtype=_F32, tail_rows=0, tail_lo=0)
    if can_host:
        plans.append(_proj_plan(xp_bf, w_a, tab_p, rope_a, tm=BLK, **a_kw))
    zg_tails = {}
    for g in range(N_DIL):
        rate, win = DIL_RATES[g], DIL_WINDOWS[g]
        res = _dil_cache(z_s3, to_time_minor(dil_caches[g][0]), g=g, rate=rate, t_len=t_len,
                         nb=max(1, CACHE_STEP_BYTES // (win * 2 * W_B * 4)),
                         hosted_plans=plans if g == host_g else ())
        o, l, rolled = res[:3]
        if g == host_g:
            for i, hg in enumerate(hosted_groups):
                zg_tails[hg] = res[3 + 2 * i:5 + 2 * i]
            if can_host:
                za_p = res[-1]
        outs_s.append(o.reshape(ts, W_B))
        lses_s.append(l.reshape(ts, LANES))
        dil_kv_s.append(to_time_major(rolled))
    if not can_host:
        za_p = _project(xp, w_a, tab_p, rope_a, tm=1024, **a_kw)
    zeros_c = jnp.zeros((bsz, H_A, DH_A, DH_A), _F32)
    zeros_n = jnp.zeros((bsz, H_A, DH_A), _F32)
    zeros_m = jnp.zeros((bsz, 1, H_A), _F32)
    ga_p, mc_p, mn_p, mm_p = _mlstm(za_p.reshape(tp, -1), bias, zeros_c, zeros_n, zeros_m,
                                    n_seq=bsz, seq_len=s_len, chunk=256, nb=1)
    for g in range(N_DIL):
        rate, win = DIL_RATES[g], DIL_WINDOWS[g]
        if g in zg_tails:
            zg_p, tail = zg_tails[g]
        else:
            zg_p, tail = _project(xp if rate == 1 else xp_bf, w_grp[g], tab_p, rope_grp, rate=rate,
                                  tm=1024, tail_rows=keeps[g], **proj_kw)
        o, l = _band_attention(zg_p, qcol=0, kcol=W_B, vcol=2 * W_B, qw=W_B, kw=W_B,
                               max_dist=win // rate, with_lse=True)
        outs_p.append(o)
        lses_p.append(l)
        dil_kv_p.append(jnp.transpose(tail, (0, 2, 1, 3))
                        .reshape(1, bsz, keeps[g] * rate, 2, H_B, DH))

    post = functools.partial(_post_mixer, tm=512, ff_chunk=1024)
    wu0, wd0 = w_up[0].astype(_BF16), w_down[0].astype(_BF16)
    ln1_0, ln2_0 = (row2(ln1_g[0]), row2(ln1_b[0])), (row2(ln2_g[0]), row2(ln2_b[0]))
    xp = post([ga_p] + outs_p + lses_p, xp, w_out0, ln1_0, wu0, wd0, ln2_0)
    xs = post([ga_s] + outs_s + lses_s, xs, w_out0, ln1_0, wu0, wd0, ln2_0)

    w_c = w_in_c[0].astype(_BF16)
    kc1 = H_C * DH
    vc1 = kc1 + KV_C * DH
    rope_c = _rope_chunks([(vc1, True), (KV_C * DH, False)])
    zc_p = _project(xp, w_c, tab_p, rope_c, n_seq=bsz, seq_len=s_len, rate=1, tm=1024)
    zc_s = _project(xs, w_c, tab_s, rope_c, n_seq=1, seq_len=ts, rate=1, tm=1024)
    (o_p,) = _band_attention(zc_p, qcol=0, kcol=kc1, vcol=vc1, qw=H_C * DH, kw=KV_C * DH,
                             max_dist=WIN_C - 1, sinks=sinks_c[0], with_lse=False)
    keep = min(WIN_C, s_len)
    swa_kv_p = zc_p.reshape(bsz, s_len, N_C)[:, s_len - keep:, kc1:].reshape(1, bsz, keep, 2, KV_C, DH)

    o_s, swa_rolled = _swa_cache(zc_s.reshape(dbs, t_len, N_C), to_time_minor(cache_swa_kv[0]),
                                 sinks_c[0], t_len=t_len, nb=8)
    o_s = o_s.reshape(ts, H_C * DH)
    swa_kv_s = to_time_major(swa_rolled)

    w_out1 = w_out_c[0].astype(_BF16)
    wu1, wd1 = w_up[1].astype(_BF16), w_down[1].astype(_BF16)
    ln1_1, ln2_1 = (row2(ln1_g[1]), row2(ln1_b[1])), (row2(ln2_g[1]), row2(ln2_b[1]))
    xp = post([o_p], xp, w_out1, ln1_1, wu1, wd1, ln2_1)
    xs = post([o_s], xs, w_out1, ln1_1, wu1, wd1, ln2_1)

    return (xp.reshape(bsz, s_len, D_MODEL), xs.reshape(dbs, t_len, D_MODEL),
            mc_p[None], mc_s[None], mn_p[None], mn_s[None],
            mm_p.reshape(1, bsz, H_A), mm_s.reshape(1, dbs, H_A),
            dil_kv_p[0], dil_kv_s[0], dil_kv_p[1], dil_kv_s[1], dil_kv_p[2], dil_kv_s[2],
            swa_kv_p, swa_kv_s)
```

```python
import functools

import jax
import jax.numpy as jnp
from jax import lax
from jax.experimental import pallas as pl
from jax.experimental.pallas import tpu as pltpu

LANES = 128
SUBLANES = 8
MXU_COLS = 256
VMEM_LIMIT = 56 * 1024 * 1024
CACHE_STEP_BYTES = 4 * 1024 * 1024

D_MODEL = 1024
DH = 64
ROT_DIM = DH // 4
ROPE_THETA = 500000.0
PAST_LEN = 8192
BLK = 128
H_A = 4
DH_A = 128
W_A = H_A * DH_A
N_DIL = 3
DIL_WINDOWS = (128, 512, 2048)
DIL_RATES = (1, 4, 16)
H_B = 8
W_B = H_B * DH
H_C = 16
KV_C = 2
WIN_C = 128
D_FF = 4 * D_MODEL
DEPTH = 2
ALPHA = (2.0 * DEPTH) ** 0.25
LN_EPS = 1e-5
N_AB = 4 * W_A + 3 * N_DIL * W_B
N_C = (H_C + 2 * KV_C) * DH
NEG_INF = float("-inf")

_F32 = jnp.float32
_BF16 = jnp.bfloat16


def _params(sem):
    return pltpu.CompilerParams(dimension_semantics=sem, vmem_limit_bytes=VMEM_LIMIT)


def _nt_dot(a, b):
    return lax.dot_general(a, b, (((1,), (1,)), ((), ())), preferred_element_type=_F32)


def _tn_dot(a, b):
    return lax.dot_general(a, b, (((0,), (0,)), ((), ())), preferred_element_type=_F32)


def _dot(a, b):
    return jnp.dot(a, b, preferred_element_type=_F32)


def _log_sigmoid(x):
    return -(jnp.maximum(-x, 0.0) + jnp.log1p(jnp.exp(-jnp.abs(x))))


def _layer_norm(y, g, b):
    mu = jnp.mean(y, axis=-1, keepdims=True)
    yc = y - mu
    var = jnp.mean(yc * yc, axis=-1, keepdims=True)
    return yc * lax.rsqrt(var + LN_EPS) * g + b


def _rope_tables(pos):
    half = ROT_DIM // 2
    inv = ROPE_THETA ** (-jnp.arange(half, dtype=_F32) / half)
    ang = pos.astype(_F32)[:, None] * inv[None, :]
    cos = jnp.cos(ang)
    sin = jnp.sin(ang)
    n = pos.shape[0]
    ones = jnp.ones((n, DH - ROT_DIM), _F32)
    zeros = jnp.zeros((n, DH - ROT_DIM), _F32)
    zh = jnp.zeros((n, half), _F32)
    cos_h = jnp.concatenate([cos, cos, ones], axis=1)
    sinm_h = jnp.concatenate([-sin, zh, zeros], axis=1)
    sinp_h = jnp.concatenate([zh, sin, zeros], axis=1)
    tile2 = lambda t: jnp.concatenate([t, t], axis=1)
    return tile2(cos_h), tile2(sinm_h), tile2(sinp_h)


def _proj_kernel(x_ref, w_ref, cos_ref, sinm_ref, sinp_ref, o_ref, *tail_refs, rope_chunks, tail_lo):
    out = o_ref.at[0, 0]
    xb = x_ref[...].astype(_BF16)
    tm = xb.shape[0]
    n = w_ref.shape[1]
    half = ROT_DIM // 2
    for c0 in range(0, n, MXU_COLS):
        width = min(MXU_COLS, n - c0)
        acc = _dot(xb, w_ref[:, c0:c0 + width])
        for cc in range(width // LANES):
            lo = c0 + cc * LANES
            sub = acc[:, cc * LANES:(cc + 1) * LANES]
            if rope_chunks[lo // LANES]:
                sub = (sub * cos_ref[...]
                       + pltpu.roll(sub, LANES - half, axis=1) * sinm_ref[...]
                       + pltpu.roll(sub, half, axis=1) * sinp_ref[...])
            out[:, lo:lo + LANES] = sub.astype(o_ref.dtype)
            if tail_refs and lo >= tail_lo:
                tail = tail_refs[0]
                tail[0, 0, :, lo - tail_lo:lo - tail_lo + LANES] = sub[tm - tail.shape[2]:, :]


def _project(x, w, tables, rope_chunks, *, n_seq, seq_len, rate, tm, out_dtype=_F32,
             tail_rows=0, tail_lo=0):
    plan = _proj_plan(x, w, tables, rope_chunks, n_seq=n_seq, seq_len=seq_len, rate=rate, tm=tm,
                      out_dtype=out_dtype, tail_rows=tail_rows, tail_lo=tail_lo)
    in_specs, out_specs = plan["specs"](lambda b, r, j: (b, r, j))
    res = pl.pallas_call(
        plan["kernel"], grid=plan["grid"], in_specs=in_specs,
        out_specs=out_specs, out_shape=plan["out_shape"],
        compiler_params=_params(("parallel", "parallel", "arbitrary")),
        name="proj",
    )(*plan["args"])
    return res if tail_rows else res[0]


def _proj_plan(x, w, tables, rope_chunks, *, n_seq, seq_len, rate, tm, out_dtype, tail_rows, tail_lo):
    d = x.shape[1]
    n = w.shape[1]
    assert len(rope_chunks) * LANES == n
    length = seq_len // rate
    tm = min(tm, length)
    nj = length // tm
    xv = x.reshape(n_seq * length, rate * d)
    tabs = [t.reshape(length, rate * LANES) for t in tables]
    out_shape = [jax.ShapeDtypeStruct((n_seq, rate, length, n), out_dtype)]
    if tail_rows:
        out_shape.append(jax.ShapeDtypeStruct((n_seq, rate, tail_rows, n - tail_lo), _F32))

    def specs(dec):
        def at(fn):
            return lambda *idx: fn(*dec(*idx))
        row_tab = pl.BlockSpec((tm, LANES), at(lambda b, r, j: (j, r)))
        in_specs = [pl.BlockSpec((tm, d), at(lambda b, r, j: (b * nj + j, r))),
                    _const_spec(w.shape), row_tab, row_tab, row_tab]
        out_specs = [pl.BlockSpec((1, 1, tm, n), at(lambda b, r, j: (b, r, j, 0)))]
        if tail_rows:
            out_specs.append(pl.BlockSpec((1, 1, tail_rows, n - tail_lo),
                                          at(lambda b, r, j: (b, r, 0, 0))))
        return in_specs, out_specs

    return dict(kernel=functools.partial(_proj_kernel, rope_chunks=tuple(rope_chunks), tail_lo=tail_lo),
                grid=(n_seq, rate, nj), specs=specs, out_shape=out_shape, args=[xv, w, *tabs])


def _mlstm_kernel(q_ref, k_ref, v_ref, oa_ref, g_ref, bias_ref, c0_ref, n0_ref, m0_ref,
                  ga_ref, c_ref, n_ref, m_ref, *, chunk, nb):
    c_idx = pl.program_id(1)

    @pl.when(c_idx == 0)
    def _():
        c_ref[...] = c0_ref[...]
        n_ref[...] = n0_ref[...]
        m_ref[...] = m0_ref[...]

    row = lax.broadcasted_iota(jnp.int32, (chunk, chunk), 0)
    col = lax.broadcasted_iota(jnp.int32, (chunk, chunk), 1)
    eye = row == col
    tril = col <= row

    def to_row(colvec):
        return jnp.sum(jnp.where(eye, colvec, 0.0), axis=0, keepdims=True)

    for bi in range(nb):
        r0 = bi * chunk
        gates = g_ref[r0:r0 + chunk, :] + bias_ref[...]
        for h in range(H_A):
            lo = h * DH_A
            q = q_ref[r0:r0 + chunk, lo:lo + DH_A]
            k = k_ref[r0:r0 + chunk, lo:lo + DH_A] * (DH_A ** -0.5)
            v = v_ref[r0:r0 + chunk, lo:lo + DH_A]
            oa = oa_ref[r0:r0 + chunk, lo:lo + DH_A]
            c_prev = c_ref[bi, h]
            n_prev = n_ref[bi, h:h + 1, :]
            m_prev = m_ref[bi, :, h:h + 1]
            li_col = gates[:, h:h + 1]
            lf_col = _log_sigmoid(gates[:, H_A + h:H_A + h + 1])
            li_row = to_row(li_col)
            lf_row = to_row(lf_col)
            b_col = jnp.sum(jnp.where(tril, lf_row, 0.0), axis=1, keepdims=True)
            b_row = to_row(b_col)
            dmat = jnp.where(tril, b_col - b_row + li_row, NEG_INF)
            a_col = b_col + m_prev
            mt = jnp.maximum(a_col, jnp.max(dmat, axis=1, keepdims=True))
            wts = jnp.exp(dmat - mt)
            inter = jnp.exp(a_col - mt)
            qb = q.astype(_BF16)
            kb = k.astype(_BF16)
            vb = v.astype(_BF16)
            sqk = _nt_dot(qb, kb) * wts
            num = inter * _dot(qb, c_prev.astype(_BF16)) + _dot(sqk.astype(_BF16), vb)
            nq = (inter * jnp.sum(q * n_prev, axis=1, keepdims=True)
                  + jnp.sum(sqk, axis=1, keepdims=True))
            hid = num / jnp.maximum(jnp.abs(nq), jnp.exp(-mt))
            ga_ref[r0:r0 + chunk, lo:lo + DH_A] = hid * jax.nn.sigmoid(oa)
            b_last = b_col[chunk - 1:chunk, :]
            mt_last = mt[chunk - 1:chunk, :]
            w_last = jnp.exp(b_last - b_col + li_col - mt_last)
            inter_last = inter[chunk - 1:chunk, :]
            kw = k * w_last
            c_ref[bi, h] = inter_last * c_prev + _tn_dot(kw.astype(_BF16), vb)
            n_ref[bi, h:h + 1, :] = inter_last * n_prev + jnp.sum(kw, axis=0, keepdims=True)
            m_ref[bi, :, h:h + 1] = mt_last


def _mlstm(z, bias, c0, n0, m0, *, n_seq, seq_len, chunk, nb):
    t = z.shape[0]
    nc = seq_len // chunk
    rows = nb * chunk
    col = lambda cb: pl.BlockSpec((rows, W_A), lambda b, c: (b * nc + c, cb))
    state_c = pl.BlockSpec((nb, H_A, DH_A, DH_A), lambda b, c: (b, 0, 0, 0))
    state_n = pl.BlockSpec((nb, H_A, DH_A), lambda b, c: (b, 0, 0))
    state_m = pl.BlockSpec((nb, 1, H_A), lambda b, c: (b, 0, 0))
    return pl.pallas_call(
        functools.partial(_mlstm_kernel, chunk=chunk, nb=nb),
        grid=(n_seq // nb, nc),
        in_specs=[col(0), col(1), col(2), col(3),
                  pl.BlockSpec((rows, LANES), lambda b, c: (b * nc + c, 4 * W_A // LANES)),
                  pl.BlockSpec((1, LANES), lambda b, c: (0, 0)),
                  state_c, state_n, state_m],
        out_specs=[pl.BlockSpec((rows, W_A), lambda b, c: (b * nc + c, 0)),
                   state_c, state_n, state_m],
        out_shape=[jax.ShapeDtypeStruct((t, W_A), _F32),
                   jax.ShapeDtypeStruct((n_seq, H_A, DH_A, DH_A), _F32),
                   jax.ShapeDtypeStruct((n_seq, H_A, DH_A), _F32),
                   jax.ShapeDtypeStruct((n_seq, 1, H_A), _F32)],
        compiler_params=_params(("parallel", "arbitrary")),
        name="mlstm",
    )(z, z, z, z, z, bias, c0, n0, m0)


def _band_kernel(*refs, max_dist, n_qtiles, kv_shared, with_sinks, with_lse):
    idx = 0
    if with_sinks:
        sink_ref = refs[0]
        idx = 1
    q_ref, k_ref, v_ref = refs[idx:idx + 3]
    idx += 3
    o_ref = refs[idx]
    idx += 1
    if with_lse:
        l_ref = refs[idx]
        idx += 1
    kprev_ref, vprev_ref = refs[idx:idx + 2]
    j = pl.program_id(2)
    n_kv = kprev_ref.shape[0]

    @pl.when(j == 0)
    def _():
        kprev_ref[...] = jnp.zeros_like(kprev_ref)
        vprev_ref[...] = jnp.zeros_like(vprev_ref)

    lane = lax.broadcasted_iota(jnp.int32, (BLK, LANES), 1)
    low = lane < DH
    if kv_shared:
        k_in, v_in = k_ref[0, 0], v_ref[0, 0]
        low_kv = lax.broadcasted_iota(jnp.int32, k_in.shape, 1) < DH
        k_sw = pltpu.roll(k_in, DH, axis=1)
        v_sw = pltpu.roll(v_in, DH, axis=1)
        k_tiles = [jnp.where(low_kv, k_in, k_sw), jnp.where(low_kv, k_sw, k_in)]
        v_tiles = [jnp.where(low_kv, v_in, v_sw), jnp.where(low_kv, v_sw, v_in)]
    else:
        k_tiles = [k_ref[0, 0, :, t * LANES:(t + 1) * LANES] for t in range(n_kv)]
        v_tiles = [v_ref[0, 0, :, t * LANES:(t + 1) * LANES] for t in range(n_kv)]
    k_tiles = [t.astype(_BF16) for t in k_tiles]
    v_tiles = [t.astype(_BF16) for t in v_tiles]

    qi = lax.broadcasted_iota(jnp.int32, (BLK, 2 * BLK), 0) + BLK
    ki = lax.broadcasted_iota(jnp.int32, (BLK, 2 * BLK), 1)
    dist = qi - ki
    band = (dist >= 0) & (dist <= max_dist)
    first_key = jnp.where(j > 0, 0, BLK)
    n_sub = q_ref.shape[2] // BLK

    tiles_per_kv = n_qtiles // n_kv
    for sb in range(n_sub):
        r0 = sb * BLK
        valid = band & (ki >= first_key) if sb == 0 else band
        q_all = q_ref[0, 0, r0:r0 + BLK, :] * (DH ** -0.5)
        lse_tile = jnp.zeros((BLK, LANES), _F32)
        for kt in range(n_kv):
            if sb == 0:
                k2 = jnp.concatenate([kprev_ref[kt], k_tiles[kt][0:BLK]], axis=0)
                v2 = jnp.concatenate([vprev_ref[kt], v_tiles[kt][0:BLK]], axis=0)
            else:
                k2 = k_tiles[kt][r0 - BLK:r0 + BLK]
                v2 = v_tiles[kt][r0 - BLK:r0 + BLK]
            parts = []
            for p in range(kt * tiles_per_kv, (kt + 1) * tiles_per_kv):
                q2 = q_all[:, p * LANES:(p + 1) * LANES]
                parts.append(jnp.where(low, q2, 0.0).astype(_BF16))
                parts.append(jnp.where(low, 0.0, q2).astype(_BF16))
            s_all = _nt_dot(jnp.concatenate(parts, axis=0), k2)
            probs, dens, lses = [], [], []
            for hh in range(2 * tiles_per_kv):
                s = jnp.where(valid, s_all[hh * BLK:(hh + 1) * BLK, :], NEG_INF)
                m = jnp.max(s, axis=1, keepdims=True)
                if with_sinks:
                    sk = sink_ref[2 * kt * tiles_per_kv + hh]
                    m = jnp.maximum(m, sk)
                p_exp = jnp.exp(s - m)
                den = jnp.sum(p_exp, axis=1, keepdims=True)
                if with_sinks:
                    den = den + jnp.exp(sk - m)
                probs.append(p_exp.astype(_BF16))
                dens.append(den)
                lses.append(m + jnp.log(den))
            o_all = _dot(jnp.concatenate(probs, axis=0), v2)
            for pp in range(tiles_per_kv):
                p = kt * tiles_per_kv + pp
                e, o = 2 * pp, 2 * pp + 1
                o_even = o_all[e * BLK:(e + 1) * BLK, :] / dens[e]
                o_odd = o_all[o * BLK:(o + 1) * BLK, :] / dens[o]
                o_ref[0, r0:r0 + BLK, p * LANES:(p + 1) * LANES] = jnp.where(low, o_even, o_odd)
                if with_lse:
                    lse_tile = jnp.where(lane == 2 * p, lses[e], lse_tile)
                    lse_tile = jnp.where(lane == 2 * p + 1, lses[o], lse_tile)
        if with_lse:
            l_ref[0, r0:r0 + BLK, :] = lse_tile

    last = (n_sub - 1) * BLK
    for kt in range(n_kv):
        kprev_ref[kt] = k_tiles[kt][last:last + BLK]
        vprev_ref[kt] = v_tiles[kt][last:last + BLK]


def _band_attention(z4, *, qcol, kcol, vcol, qw, kw, max_dist, sinks=None, with_lse):
    n_seq, rate, length, _ = z4.shape
    seq_len = length * rate
    nblk = length // BLK
    n_sub = 2 if nblk % 2 == 0 else 1
    rows = n_sub * BLK
    kv_shared = kw == LANES and qw > LANES
    n_kv = 2 if kv_shared else kw // LANES

    def spec(width, coff):
        assert coff % width == 0
        return pl.BlockSpec((1, 1, rows, width), lambda b, r, j: (b, r, j, coff // width))

    out_spec = pl.BlockSpec((1, rows, qw), lambda b, r, j: (b, j, r))
    out_shape = [jax.ShapeDtypeStruct((n_seq, length, rate * qw), _F32)]
    out_specs = [out_spec]
    if with_lse:
        out_shape.append(jax.ShapeDtypeStruct((n_seq, length, rate * LANES), _F32))
        out_specs.append(pl.BlockSpec((1, rows, LANES), lambda b, r, j: (b, j, r)))
    in_specs = [spec(qw, qcol), spec(kw, kcol), spec(kw, vcol)]
    args = [z4, z4, z4]
    if sinks is not None:
        in_specs = [pl.BlockSpec(memory_space=pltpu.SMEM)] + in_specs
        args = [sinks] + args
    res = pl.pallas_call(
        functools.partial(_band_kernel, max_dist=max_dist, n_qtiles=qw // LANES,
                          kv_shared=kv_shared, with_sinks=sinks is not None, with_lse=with_lse),
        grid=(n_seq, rate, nblk // n_sub),
        in_specs=in_specs, out_specs=out_specs, out_shape=out_shape,
        scratch_shapes=[pltpu.VMEM((n_kv, BLK, LANES), _BF16),
                        pltpu.VMEM((n_kv, BLK, LANES), _BF16)],
        compiler_params=_params(("parallel", "parallel", "arbitrary")),
        name="band_attn",
    )(*args)
    return [r.reshape(n_seq * seq_len, -1) for r in res]


def _mix_ab(ga_ref, o1_ref, o2_ref, o3_ref, l1_ref, l2_ref, l3_ref, w_ref):
    l1, l2, l3 = l1_ref[...], l2_ref[...], l3_ref[...]
    lmax = jnp.maximum(jnp.maximum(l1, l2), l3)
    e1, e2, e3 = jnp.exp(l1 - lmax), jnp.exp(l2 - lmax), jnp.exp(l3 - lmax)
    inv = 1.0 / (e1 + e2 + e3)
    wts = (e1 * inv, e2 * inv, e3 * inv)
    outs = (o1_ref, o2_ref, o3_ref)
    low = lax.broadcasted_iota(jnp.int32, (l1.shape[0], LANES), 1) < DH
    tiles = []
    for p in range(H_B // 2):
        c = p * LANES
        ob = sum(jnp.where(low, wg[:, 2 * p:2 * p + 1], wg[:, 2 * p + 1:2 * p + 2]) * og[:, c:c + LANES]
                 for wg, og in zip(wts, outs))
        tiles.append(ob)
    return (_dot(ga_ref[...].astype(_BF16), w_ref[0:W_A, :])
            + _dot(jnp.concatenate(tiles, axis=1).astype(_BF16), w_ref[W_A:W_A + W_B, :]))


def _post_kernel(*refs, n_parts, ff_chunk):
    parts = refs[:n_parts]
    x_ref, w_ref, g1_ref, b1_ref, wu_ref, wd_ref, g2_ref, b2_ref, y_ref = refs[n_parts:]
    if n_parts == 1:
        mix = _dot(parts[0][...].astype(_BF16), w_ref[...])
    else:
        mix = _mix_ab(*parts, w_ref)
    x1 = _layer_norm(ALPHA * x_ref[...] + mix, g1_ref[...], b1_ref[...])
    xb = x1.astype(_BF16)
    acc = ALPHA * x1
    for c in range(D_FF // ff_chunk):
        hid = _dot(xb, wu_ref[:, c * ff_chunk:(c + 1) * ff_chunk])
        hid = jnp.square(jnp.maximum(hid, 0.0)).astype(_BF16)
        acc = acc + _dot(hid, wd_ref[c * ff_chunk:(c + 1) * ff_chunk, :])
    y_ref[...] = _layer_norm(acc, g2_ref[...], b2_ref[...])


def _const_spec(shape):
    return pl.BlockSpec(shape, lambda *_: (0,) * len(shape), pipeline_mode=pl.Buffered(1))


def _post_mixer(parts, x, w_out, ln1, wu, wd, ln2, *, tm, ff_chunk):
    plan = _post_plan(parts, x, w_out, ln1, wu, wd, ln2, tm=tm, ff_chunk=ff_chunk)
    in_specs, out_specs = plan["specs"](lambda i: (i,))
    return pl.pallas_call(
        plan["kernel"], grid=plan["grid"], in_specs=in_specs, out_specs=out_specs,
        out_shape=plan["out_shape"], compiler_params=_params(("parallel",)),
        name="post_mixer",
    )(*plan["args"])[0]


def _post_plan(parts, x, w_out, ln1, wu, wd, ln2, *, tm, ff_chunk):
    t = x.shape[0]
    tm = min(tm, t)

    def specs(dec):
        row = lambda width: pl.BlockSpec((tm, width), lambda *idx: (dec(*idx)[0], 0))
        vec = _const_spec((1, D_MODEL))
        in_specs = ([row(p.shape[1]) for p in parts]
                    + [row(D_MODEL), _const_spec(w_out.shape), vec, vec,
                       _const_spec(wu.shape), _const_spec(wd.shape), vec, vec])
        return in_specs, [row(D_MODEL)]

    return dict(kernel=functools.partial(_post_kernel, n_parts=len(parts), ff_chunk=ff_chunk),
                grid=(t // tm,), specs=specs, out_shape=[jax.ShapeDtypeStruct((t, D_MODEL), _F32)],
                args=[*parts, x, w_out, *ln1, wu, wd, *ln2])


def _roll_window(old, new, t_len):
    win = old.shape[1]
    rolled = pltpu.roll(old, win - t_len, axis=1)
    new_tail = jnp.concatenate([jnp.zeros((LANES - t_len, LANES), _F32), new], axis=0).T
    tail_lane = lax.broadcasted_iota(jnp.int32, new_tail.shape, 1) >= LANES - t_len
    tail = jnp.where(tail_lane, new_tail, rolled[:, win - LANES:])
    if win == LANES:
        return tail
    return jnp.concatenate([rolled[:, :win - LANES], tail], axis=1)


def _dil_cache_kernel(q_ref, k_ref, v_ref, cache_ref, o_ref, l_ref, out_ref, *, rate, t_len):
    win = cache_ref.shape[-1]
    rows = 2 * t_len
    key_i = lax.broadcasted_iota(jnp.int32, (rows, win), 1)
    row_t = lax.broadcasted_iota(jnp.int32, (rows, win), 0) % t_len
    ok_c = (key_i >= row_t) & (((key_i - row_t) & (rate - 1)) == 0)
    n_j = lax.broadcasted_iota(jnp.int32, (rows, t_len), 1)
    n_t = lax.broadcasted_iota(jnp.int32, (rows, t_len), 0) % t_len
    ok_n = (n_j <= n_t) & (((n_t - n_j) & (rate - 1)) == 0)
    low = lax.broadcasted_iota(jnp.int32, (t_len, LANES), 1) < DH
    _for_each(cache_ref.shape[0], functools.partial(
        _dil_cache_one, q_ref, k_ref, v_ref, cache_ref, o_ref, l_ref, out_ref,
        ok_c, ok_n, low, t_len))


def _for_each(n, body):
    if n == 1:
        body(0)
    else:
        lax.fori_loop(0, n, lambda i, carry: (body(i), carry)[1], 0)


def _dil_cache_one(q_ref, k_ref, v_ref, cache_ref, o_ref, l_ref, out_ref, ok_c, ok_n, low, t_len, bi):
    win = cache_ref.shape[-1]
    rows = 2 * t_len
    lane = lax.broadcasted_iota(jnp.int32, (t_len, LANES), 1)
    lse_tile = jnp.zeros((t_len, LANES), _F32)
    for a in range(H_B // 2):
        c = a * LANES
        q2 = q_ref[bi, :, c:c + LANES] * (DH ** -0.5)
        lhs = jnp.concatenate([jnp.where(low, q2, 0.0), jnp.where(low, 0.0, q2)], axis=0).astype(_BF16)
        k_old = cache_ref[bi, 0, 2 * a:2 * a + 2].reshape(2 * DH, win)
        v_old = cache_ref[bi, 1, 2 * a:2 * a + 2].reshape(2 * DH, win)
        k_new = k_ref[bi, :, c:c + LANES]
        v_new = v_ref[bi, :, c:c + LANES]
        s_c = jnp.where(ok_c, _dot(lhs, k_old.astype(_BF16)), NEG_INF)
        s_n = jnp.where(ok_n, _nt_dot(lhs, k_new.astype(_BF16)), NEG_INF)
        m = jnp.maximum(jnp.max(s_c, axis=1, keepdims=True), jnp.max(s_n, axis=1, keepdims=True))
        p_c = jnp.exp(s_c - m)
        p_n = jnp.exp(s_n - m)
        den = jnp.sum(p_c, axis=1, keepdims=True) + jnp.sum(p_n, axis=1, keepdims=True)
        o = (_nt_dot(p_c.astype(_BF16), v_old.astype(_BF16))
             + _dot(p_n.astype(_BF16), v_new.astype(_BF16))) / den
        lse = m + jnp.log(den)
        o_ref[bi, :, c:c + LANES] = jnp.where(low, o[0:t_len], o[t_len:rows])
        lse_tile = jnp.where(lane == 2 * a, lse[0:t_len], lse_tile)
        lse_tile = jnp.where(lane == 2 * a + 1, lse[t_len:rows], lse_tile)
        for kv, old, new in ((0, k_old, k_new), (1, v_old, v_new)):
            out_ref[bi, kv, 2 * a:2 * a + 2] = _roll_window(old, new, t_len).reshape(2, DH, win)
    l_ref[bi] = lse_tile


def _dil_cache_hosting_kernel(*refs, rate, t_len, hosted):
    n_in = 4 + sum(n for _, n, _ in hosted)
    _dil_cache_kernel(*refs[:4], *refs[n_in:n_in + 3], rate=rate, t_len=t_len)
    i_in, i_out = 4, n_in + 3
    for kern, n_ins, n_outs in hosted:
        kern(*refs[i_in:i_in + n_ins], *refs[i_out:i_out + n_outs])
        i_in += n_ins
        i_out += n_outs


def _dil_cache(z_s3, cache_t, *, g, rate, t_len, nb, hosted_plans=()):
    bsz, _, _, _, win = cache_t.shape
    steps = bsz // nb
    q_blk, k_blk, v_blk = 3 * g, 3 * g + 1, 3 * g + 2
    col = lambda cb: pl.BlockSpec((nb, t_len, W_B), lambda b: (b, 0, cb))
    tok = lambda width: pl.BlockSpec((nb, t_len, width), lambda b: (b, 0, 0))
    blk5 = lambda last: pl.BlockSpec((nb, 2, H_B, DH, last), lambda b: (b, 0, 0, 0, 0))
    in_specs = [col(q_blk), col(k_blk), col(v_blk), blk5(win)]
    out_specs = [tok(W_B), tok(LANES), blk5(win)]
    out_shape = [jax.ShapeDtypeStruct((bsz, t_len, W_B), _F32),
                 jax.ShapeDtypeStruct((bsz, t_len, LANES), _F32),
                 jax.ShapeDtypeStruct(cache_t.shape, _F32)]
    args = [z_s3, z_s3, z_s3, cache_t]
    hosted = []
    for plan in hosted_plans:
        grid = plan["grid"]
        assert functools.reduce(lambda a, b: a * b, grid) == steps

        def dec(i, grid=grid):
            idx = []
            for size in reversed(grid):
                idx.append(i % size)
                i = i // size
            return tuple(reversed(idx))

        p_in, p_out = plan["specs"](dec)
        in_specs += p_in
        out_specs += p_out
        out_shape += plan["out_shape"]
        args += plan["args"]
        hosted.append((plan["kernel"], len(p_in), len(p_out)))
    return pl.pallas_call(
        functools.partial(_dil_cache_hosting_kernel, rate=rate, t_len=t_len, hosted=tuple(hosted)),
        grid=(steps,), in_specs=in_specs, out_specs=out_specs, out_shape=out_shape,
        compiler_params=_params(("arbitrary",)),
        name="dil_cache",
    )(*args)


def _swa_sample_kernel(sink_ref, z_ref, cache_ref, o_ref, out_ref, *, t_len):
    grp = H_C // KV_C
    tiles = grp // 2
    rows = t_len * grp
    kw = KV_C * DH
    q_w = H_C * DH
    r_t = lax.broadcasted_iota(jnp.int32, (rows, WIN_C), 0) % t_len
    key_i = lax.broadcasted_iota(jnp.int32, (rows, WIN_C), 1)
    ok_c = key_i > r_t
    n_t = lax.broadcasted_iota(jnp.int32, (rows, t_len), 0) % t_len
    n_j = lax.broadcasted_iota(jnp.int32, (rows, t_len), 1)
    ok_n = n_j <= n_t
    r_h = lax.broadcasted_iota(jnp.int32, (rows, 1), 0) // t_len
    low_n = lax.broadcasted_iota(jnp.int32, (t_len, LANES), 1) < DH

    def dup(x, j):
        sw = pltpu.roll(x, DH, axis=1)
        return (jnp.where(low_n, x, sw) if j == 0 else jnp.where(low_n, sw, x)).astype(_BF16)

    sinks = []
    for j in range(KV_C):
        sk = jnp.zeros((rows, 1), _F32)
        for g in range(grp):
            sk = jnp.where(r_h == g, sink_ref[j * grp + g], sk)
        sinks.append(sk)

    def one(bi):
        k_n, v_n = z_ref[bi, :, q_w:q_w + kw], z_ref[bi, :, q_w + kw:q_w + 2 * kw]
        for kv, new in ((0, k_n), (1, v_n)):
            old = cache_ref[bi, kv].reshape(kw, WIN_C)
            out_ref[bi, kv] = _roll_window(old, new, t_len).reshape(KV_C, DH, WIN_C)
        for j in range(KV_C):
            k_old = cache_ref[bi, 0, j].astype(_BF16)
            v_old = cache_ref[bi, 1, j].astype(_BF16)
            k_dup = jnp.concatenate([k_old, k_old], axis=0)
            v_dup = jnp.concatenate([v_old, v_old], axis=0)
            parts = []
            for a in range(tiles):
                c = (j * tiles + a) * LANES
                q2 = z_ref[bi, :, c:c + LANES] * (DH ** -0.5)
                parts.append(jnp.where(low_n, q2, 0.0).astype(_BF16))
                parts.append(jnp.where(low_n, 0.0, q2).astype(_BF16))
            q = jnp.concatenate(parts, axis=0)
            sk = sinks[j]
            s_c = jnp.where(ok_c, _dot(q, k_dup), NEG_INF)
            s_n = jnp.where(ok_n, _nt_dot(q, dup(k_n, j)), NEG_INF)
            m = jnp.maximum(jnp.maximum(jnp.max(s_c, axis=1, keepdims=True),
                                        jnp.max(s_n, axis=1, keepdims=True)), sk)
            p_c = jnp.exp(s_c - m)
            p_n = jnp.exp(s_n - m)
            den = (jnp.sum(p_c, axis=1, keepdims=True) + jnp.sum(p_n, axis=1, keepdims=True)
                   + jnp.exp(sk - m))
            o = (_nt_dot(p_c.astype(_BF16), v_dup)
                 + _dot(p_n.astype(_BF16), dup(v_n, j))) / den
            for a in range(tiles):
                c = (j * tiles + a) * LANES
                even = o[(2 * a) * t_len:(2 * a + 1) * t_len, :]
                odd = o[(2 * a + 1) * t_len:(2 * a + 2) * t_len, :]
                o_ref[bi, :, c:c + LANES] = jnp.where(low_n, even, odd)

    _for_each(cache_ref.shape[0], one)


def _swa_cache(zc_s3, cache_t, sinks, *, t_len, nb):
    dbs = cache_t.shape[0]
    q_w = H_C * DH
    blk5 = lambda last: pl.BlockSpec((nb, 2, KV_C, DH, last), lambda b: (b, 0, 0, 0, 0))
    return pl.pallas_call(
        functools.partial(_swa_sample_kernel, t_len=t_len), grid=(dbs // nb,),
        in_specs=[pl.BlockSpec(memory_space=pltpu.SMEM),
                  pl.BlockSpec((nb, t_len, N_C), lambda b: (b, 0, 0)),
                  blk5(WIN_C)],
        out_specs=[pl.BlockSpec((nb, t_len, q_w), lambda b: (b, 0, 0)), blk5(WIN_C)],
        out_shape=[jax.ShapeDtypeStruct((dbs, t_len, q_w), _F32),
                   jax.ShapeDtypeStruct(cache_t.shape, _F32)],
        compiler_params=_params(("parallel",)),
        name="swa_cache",
    )(sinks, zc_s3, cache_t)


def _rope_chunks(widths_and_flags):
    out = []
    for width, flag in widths_and_flags:
        out += [flag] * (width // LANES)
    return tuple(out)


def kernel(x_prompt, x_sample, state_mlstm_C, state_mlstm_n, state_mlstm_m, cache_dil1_kv, cache_dil2_kv, cache_dil3_kv, cache_swa_kv, w_in_ab, b_gate_ab, w_out_ab, w_in_c, sinks_c, w_out_c, ln1_g, ln1_b, ln2_g, ln2_b, w_up, w_down):
    bsz, s_len, _ = x_prompt.shape
    dbs, t_len, _ = x_sample.shape
    tp = bsz * s_len
    ts = dbs * t_len
    xp = x_prompt.reshape(tp, D_MODEL)
    xs = x_sample.reshape(ts, D_MODEL)
    xp_bf = xp.astype(_BF16)
    pos_p = jnp.arange(s_len, dtype=jnp.int32)
    pos_s = PAST_LEN + jnp.arange(t_len, dtype=jnp.int32)
    tab_p = _rope_tables(pos_p)
    tab_s = tuple(jnp.tile(t, (ts // t_len, 1)) for t in _rope_tables(pos_s))
    row2 = lambda v: v.reshape(1, -1)

    w_in = w_in_ab[0]
    gate_lo = 4 * W_A
    q_lo = gate_lo + 2 * H_A
    k_lo = q_lo + N_DIL * W_B
    v_lo = k_lo + N_DIL * W_B
    w_a = jnp.concatenate([w_in[:, :gate_lo],
                           jnp.pad(w_in[:, gate_lo:q_lo], ((0, 0), (0, LANES - 2 * H_A)))],
                          axis=1).astype(_BF16)
    w_grp = [jnp.concatenate([w_in[:, lo + g * W_B:lo + (g + 1) * W_B] for lo in (q_lo, k_lo, v_lo)],
                             axis=1).astype(_BF16) for g in range(N_DIL)]
    rope_a = _rope_chunks([(gate_lo + LANES, False)])
    rope_grp = _rope_chunks([(2 * W_B, True), (W_B, False)])
    bias = jnp.pad(b_gate_ab[0], (0, LANES - 2 * H_A)).reshape(1, LANES)
    w_out0 = w_out_ab[0].astype(_BF16)

    za_s = _project(xs, w_a, tab_s, rope_a, n_seq=1, seq_len=ts, rate=1, tm=1024)
    zg_s = _project(xs, jnp.concatenate(w_grp, axis=1), tab_s, rope_grp * N_DIL,
                    n_seq=1, seq_len=ts, rate=1, tm=1024)

    ga_s, mc_s, mn_s, mm_s = _mlstm(za_s.reshape(ts, -1), bias, state_mlstm_C[0], state_mlstm_n[0],
                                    state_mlstm_m[0].reshape(dbs, 1, H_A),
                                    n_seq=dbs, seq_len=t_len, chunk=t_len, nb=8)

    dil_caches = (cache_dil1_kv, cache_dil2_kv, cache_dil3_kv)
    outs_p, lses_p, outs_s, lses_s, dil_kv_p, dil_kv_s = [], [], [], [], [], []
    to_time_minor = lambda c: jnp.transpose(c, (0, 2, 3, 4, 1))
    to_time_major = lambda c: jnp.transpose(c, (0, 4, 1, 2, 3))[None]
    z_s3 = zg_s.reshape(dbs, t_len, N_DIL * 3 * W_B)
    proj_kw = dict(n_seq=bsz, seq_len=s_len, tm=1024, out_dtype=_BF16, tail_lo=W_B)
    keeps = [min(DIL_WINDOWS[g], s_len) // DIL_RATES[g] for g in range(N_DIL)]
    za_p = _project(xp_bf, w_a, tab_p, rope_a, n_seq=bsz, seq_len=s_len, rate=1, tm=1024)
    zeros_c = jnp.zeros((bsz, H_A, DH_A, DH_A), _F32)
    zeros_n = jnp.zeros((bsz, H_A, DH_A), _F32)
    zeros_m = jnp.zeros((bsz, 1, H_A), _F32)
    ga_p, mc_p, mn_p, mm_p = _mlstm(za_p.reshape(tp, -1), bias, zeros_c, zeros_n, zeros_m,
                                    n_seq=bsz, seq_len=s_len, chunk=256, nb=1)
    for g in range(N_DIL):
        rate, win = DIL_RATES[g], DIL_WINDOWS[g]
        zg_p, tail = _project(xp_bf, w_grp[g], tab_p, rope_grp, rate=rate, tail_rows=keeps[g], **proj_kw)
        o, l = _band_attention(zg_p, qcol=0, kcol=W_B, vcol=2 * W_B, qw=W_B, kw=W_B,
                               max_dist=win // rate, with_lse=True)
        outs_p.append(o)
        lses_p.append(l)
        dil_kv_p.append(jnp.transpose(tail, (0, 2, 1, 3))
                        .reshape(1, bsz, keeps[g] * rate, 2, H_B, DH))

    post = functools.partial(_post_mixer, tm=512, ff_chunk=1024)
    wu0, wd0 = w_up[0].astype(_BF16), w_down[0].astype(_BF16)
    ln1_0, ln2_0 = (row2(ln1_g[0]), row2(ln1_b[0])), (row2(ln2_g[0]), row2(ln2_b[0]))

    xp = post([ga_p] + outs_p + lses_p, xp, w_out0, ln1_0, wu0, wd0, ln2_0)

    w_c = w_in_c[0].astype(_BF16)
    kc1 = H_C * DH
    vc1 = kc1 + KV_C * DH
    rope_c = _rope_chunks([(vc1, True), (KV_C * DH, False)])
    zc_p = _project(xp, w_c, tab_p, rope_c, n_seq=bsz, seq_len=s_len, rate=1, tm=1024)
    (o_p,) = _band_attention(zc_p, qcol=0, kcol=kc1, vcol=vc1, qw=H_C * DH, kw=KV_C * DH,
                             max_dist=WIN_C - 1, sinks=sinks_c[0], with_lse=False)
    keep = min(WIN_C, s_len)
    swa_kv_p = zc_p.reshape(bsz, s_len, N_C)[:, s_len - keep:, kc1:].reshape(1, bsz, keep, 2, KV_C, DH)
    w_out1 = w_out_c[0].astype(_BF16)
    wu1, wd1 = w_up[1].astype(_BF16), w_down[1].astype(_BF16)
    ln1_1, ln2_1 = (row2(ln1_g[1]), row2(ln1_b[1])), (row2(ln2_g[1]), row2(ln2_b[1]))

    nbs = [max(1, CACHE_STEP_BYTES // (DIL_WINDOWS[g] * 2 * W_B * 4)) for g in range(N_DIL)]
    host_g = N_DIL - 1
    can_host = tp // BLK == dbs // nbs[host_g]
    layer1 = ([o_p], xp, w_out1, ln1_1, wu1, wd1, ln2_1)
    for g in range(N_DIL):
        plans = [_post_plan(*layer1, tm=BLK, ff_chunk=1024)] if (g == host_g and can_host) else ()
        res = _dil_cache(z_s3, to_time_minor(dil_caches[g][0]), g=g, rate=DIL_RATES[g], t_len=t_len,
                         nb=nbs[g], hosted_plans=plans)
        o, l, rolled = res[:3]
        if plans:
            xp = res[3]
        outs_s.append(o.reshape(ts, W_B))
        lses_s.append(l.reshape(ts, LANES))
        dil_kv_s.append(to_time_major(rolled))
    if not can_host:
        xp = post(*layer1)

    xs = post([ga_s] + outs_s + lses_s, xs, w_out0, ln1_0, wu0, wd0, ln2_0)
    zc_s = _project(xs, w_c, tab_s, rope_c, n_seq=1, seq_len=ts, rate=1, tm=1024)
    o_s, swa_rolled = _swa_cache(zc_s.reshape(dbs, t_len, N_C), to_time_minor(cache_swa_kv[0]),
                                 sinks_c[0], t_len=t_len, nb=8)
    swa_kv_s = to_time_major(swa_rolled)
    xs = post([o_s.reshape(ts, H_C * DH)], xs, w_out1, ln1_1, wu1, wd1, ln2_1)

    return (xp.reshape(bsz, s_len, D_MODEL), xs.reshape(dbs, t_len, D_MODEL),
            mc_p[None], mc_s[None], mn_p[None], mn_s[None],
            mm_p.reshape(1, bsz, H_A), mm_s.reshape(1, dbs, H_A),
            dil_kv_p[0], dil_kv_s[0], dil_kv_p[1], dil_kv_s[1], dil_kv_p[2], dil_kv_s[2],
            swa_kv_p, swa_kv_s)
```

```python
import functools

import jax
import jax.numpy as jnp
from jax import lax
from jax.experimental import pallas as pl
from jax.experimental.pallas import tpu as pltpu

LANES = 128
SUBLANES = 8
MXU_COLS = 256
VMEM_LIMIT = 56 * 1024 * 1024
CACHE_STEP_BYTES = 4 * 1024 * 1024

D_MODEL = 1024
DH = 64
ROT_DIM = DH // 4
ROPE_THETA = 500000.0
PAST_LEN = 8192
BLK = 128
H_A = 4
DH_A = 128
W_A = H_A * DH_A
N_DIL = 3
DIL_WINDOWS = (128, 512, 2048)
DIL_RATES = (1, 4, 16)
H_B = 8
W_B = H_B * DH
H_C = 16
KV_C = 2
WIN_C = 128
D_FF = 4 * D_MODEL
DEPTH = 2
ALPHA = (2.0 * DEPTH) ** 0.25
LN_EPS = 1e-5
N_AB = 4 * W_A + 3 * N_DIL * W_B
N_C = (H_C + 2 * KV_C) * DH
NEG_INF = float("-inf")

_F32 = jnp.float32
_BF16 = jnp.bfloat16


def _params(sem):
    return pltpu.CompilerParams(dimension_semantics=sem, vmem_limit_bytes=VMEM_LIMIT)


def _nt_dot(a, b):
    return lax.dot_general(a, b, (((1,), (1,)), ((), ())), preferred_element_type=_F32)


def _tn_dot(a, b):
    return lax.dot_general(a, b, (((0,), (0,)), ((), ())), preferred_element_type=_F32)


def _dot(a, b):
    return jnp.dot(a, b, preferred_element_type=_F32)


def _log_sigmoid(x):
    return -(jnp.maximum(-x, 0.0) + jnp.log1p(jnp.exp(-jnp.abs(x))))


def _layer_norm(y, g, b):
    mu = jnp.mean(y, axis=-1, keepdims=True)
    yc = y - mu
    var = jnp.mean(yc * yc, axis=-1, keepdims=True)
    return yc * lax.rsqrt(var + LN_EPS) * g + b


def _rope_tables(pos):
    half = ROT_DIM // 2
    inv = ROPE_THETA ** (-jnp.arange(half, dtype=_F32) / half)
    ang = pos.astype(_F32)[:, None] * inv[None, :]
    cos = jnp.cos(ang)
    sin = jnp.sin(ang)
    n = pos.shape[0]
    ones = jnp.ones((n, DH - ROT_DIM), _F32)
    zeros = jnp.zeros((n, DH - ROT_DIM), _F32)
    zh = jnp.zeros((n, half), _F32)
    cos_h = jnp.concatenate([cos, cos, ones], axis=1)
    sinm_h = jnp.concatenate([-sin, zh, zeros], axis=1)
    sinp_h = jnp.concatenate([zh, sin, zeros], axis=1)
    tile2 = lambda t: jnp.concatenate([t, t], axis=1)
    return tile2(cos_h), tile2(sinm_h), tile2(sinp_h)


def _proj_kernel(x_ref, w_ref, cos_ref, sinm_ref, sinp_ref, o_ref, *tail_refs, rope_chunks, tail_lo):
    out = o_ref.at[0, 0]
    xb = x_ref[...].astype(_BF16)
    tm = xb.shape[0]
    n = w_ref.shape[1]
    half = ROT_DIM // 2
    for c0 in range(0, n, MXU_COLS):
        width = min(MXU_COLS, n - c0)
        acc = _dot(xb, w_ref[:, c0:c0 + width])
        for cc in range(width // LANES):
            lo = c0 + cc * LANES
            sub = acc[:, cc * LANES:(cc + 1) * LANES]
            if rope_chunks[lo // LANES]:
                sub = (sub * cos_ref[...]
                       + pltpu.roll(sub, LANES - half, axis=1) * sinm_ref[...]
                       + pltpu.roll(sub, half, axis=1) * sinp_ref[...])
            out[:, lo:lo + LANES] = sub.astype(o_ref.dtype)
            if tail_refs and lo >= tail_lo:
                tail = tail_refs[0]
                tail[0, 0, :, lo - tail_lo:lo - tail_lo + LANES] = sub[tm - tail.shape[2]:, :]


def _project(x, w, tables, rope_chunks, *, n_seq, seq_len, rate, tm, out_dtype=_F32,
             tail_rows=0, tail_lo=0):
    plan = _proj_plan(x, w, tables, rope_chunks, n_seq=n_seq, seq_len=seq_len, rate=rate, tm=tm,
                      out_dtype=out_dtype, tail_rows=tail_rows, tail_lo=tail_lo)
    in_specs, out_specs = plan["specs"](lambda b, r, j: (b, r, j))
    res = pl.pallas_call(
        plan["kernel"], grid=plan["grid"], in_specs=in_specs,
        out_specs=out_specs, out_shape=plan["out_shape"],
        compiler_params=_params(("parallel", "parallel", "arbitrary")),
        name="proj",
    )(*plan["args"])
    return res if tail_rows else res[0]


def _proj_plan(x, w, tables, rope_chunks, *, n_seq, seq_len, rate, tm, out_dtype, tail_rows, tail_lo):
    d = x.shape[1]
    n = w.shape[1]
    assert len(rope_chunks) * LANES == n
    length = seq_len // rate
    tm = min(tm, length)
    nj = length // tm
    xv = x.reshape(n_seq * length, rate * d)
    tabs = [t.reshape(length, rate * LANES) for t in tables]
    out_shape = [jax.ShapeDtypeStruct((n_seq, rate, length, n), out_dtype)]
    if tail_rows:
        out_shape.append(jax.ShapeDtypeStruct((n_seq, rate, tail_rows, n - tail_lo), _F32))

    def specs(dec):
        def at(fn):
            return lambda *idx: fn(*dec(*idx))
        row_tab = pl.BlockSpec((tm, LANES), at(lambda b, r, j: (j, r)))
        in_specs = [pl.BlockSpec((tm, d), at(lambda b, r, j: (b * nj + j, r))),
                    _const_spec(w.shape), row_tab, row_tab, row_tab]
        out_specs = [pl.BlockSpec((1, 1, tm, n), at(lambda b, r, j: (b, r, j, 0)))]
        if tail_rows:
            out_specs.append(pl.BlockSpec((1, 1, tail_rows, n - tail_lo),
                                          at(lambda b, r, j: (b, r, 0, 0))))
        return in_specs, out_specs

    return dict(kernel=functools.partial(_proj_kernel, rope_chunks=tuple(rope_chunks), tail_lo=tail_lo),
                grid=(n_seq, rate, nj), specs=specs, out_shape=out_shape, args=[xv, w, *tabs])


def _mlstm_kernel(q_ref, k_ref, v_ref, oa_ref, g_ref, bias_ref, c0_ref, n0_ref, m0_ref,
                  ga_ref, c_ref, n_ref, m_ref, *, chunk, nb):
    c_idx = pl.program_id(1)

    @pl.when(c_idx == 0)
    def _():
        c_ref[...] = c0_ref[...]
        n_ref[...] = n0_ref[...]
        m_ref[...] = m0_ref[...]

    row = lax.broadcasted_iota(jnp.int32, (chunk, chunk), 0)
    col = lax.broadcasted_iota(jnp.int32, (chunk, chunk), 1)
    eye = row == col
    tril = col <= row

    def to_row(colvec):
        return jnp.sum(jnp.where(eye, colvec, 0.0), axis=0, keepdims=True)

    stores = []
    for bi in range(nb):
        r0 = bi * chunk
        gates = g_ref[r0:r0 + chunk, :] + bias_ref[...]
        for h in range(H_A):
            lo = h * DH_A
            q = q_ref[r0:r0 + chunk, lo:lo + DH_A]
            k = k_ref[r0:r0 + chunk, lo:lo + DH_A] * (DH_A ** -0.5)
            v = v_ref[r0:r0 + chunk, lo:lo + DH_A]
            oa = oa_ref[r0:r0 + chunk, lo:lo + DH_A]
            c_prev = c_ref[bi, h]
            n_prev = n_ref[bi, h:h + 1, :]
            m_prev = m_ref[bi, :, h:h + 1]
            li_col = gates[:, h:h + 1]
            lf_col = _log_sigmoid(gates[:, H_A + h:H_A + h + 1])
            li_row = to_row(li_col)
            lf_row = to_row(lf_col)
            b_col = jnp.sum(jnp.where(tril, lf_row, 0.0), axis=1, keepdims=True)
            b_row = to_row(b_col)
            dmat = jnp.where(tril, b_col - b_row + li_row, NEG_INF)
            a_col = b_col + m_prev
            mt = jnp.maximum(a_col, jnp.max(dmat, axis=1, keepdims=True))
            wts = jnp.exp(dmat - mt)
            inter = jnp.exp(a_col - mt)
            qb = q.astype(_BF16)
            kb = k.astype(_BF16)
            vb = v.astype(_BF16)
            sqk = _nt_dot(qb, kb) * wts
            num = inter * _dot(qb, c_prev.astype(_BF16)) + _dot(sqk.astype(_BF16), vb)
            nq = (inter * jnp.sum(q * n_prev, axis=1, keepdims=True)
                  + jnp.sum(sqk, axis=1, keepdims=True))
            hid = num / jnp.maximum(jnp.abs(nq), jnp.exp(-mt))
            gated = hid * jax.nn.sigmoid(oa)
            b_last = b_col[chunk - 1:chunk, :]
            mt_last = mt[chunk - 1:chunk, :]
            w_last = jnp.exp(b_last - b_col + li_col - mt_last)
            inter_last = inter[chunk - 1:chunk, :]
            kw = k * w_last
            c_new = inter_last * c_prev + _tn_dot(kw.astype(_BF16), vb)
            n_new = inter_last * n_prev + jnp.sum(kw, axis=0, keepdims=True)
            stores.append((bi, h, gated, c_new, n_new, mt_last))
    for bi, h, gated, c_new, n_new, m_new in stores:
        ga_ref[bi * chunk:(bi + 1) * chunk, h * DH_A:(h + 1) * DH_A] = gated
        c_ref[bi, h] = c_new
        n_ref[bi, h:h + 1, :] = n_new
        m_ref[bi, :, h:h + 1] = m_new


def _mlstm(z, bias, c0, n0, m0, *, n_seq, seq_len, chunk, nb):
    t = z.shape[0]
    nc = seq_len // chunk
    rows = nb * chunk
    col = lambda cb: pl.BlockSpec((rows, W_A), lambda b, c: (b * nc + c, cb))
    state_c = pl.BlockSpec((nb, H_A, DH_A, DH_A), lambda b, c: (b, 0, 0, 0))
    state_n = pl.BlockSpec((nb, H_A, DH_A), lambda b, c: (b, 0, 0))
    state_m = pl.BlockSpec((nb, 1, H_A), lambda b, c: (b, 0, 0))
    return pl.pallas_call(
        functools.partial(_mlstm_kernel, chunk=chunk, nb=nb),
        grid=(n_seq // nb, nc),
        in_specs=[col(0), col(1), col(2), col(3),
                  pl.BlockSpec((rows, LANES), lambda b, c: (b * nc + c, 4 * W_A // LANES)),
                  pl.BlockSpec((1, LANES), lambda b, c: (0, 0)),
                  state_c, state_n, state_m],
        out_specs=[pl.BlockSpec((rows, W_A), lambda b, c: (b * nc + c, 0)),
                   state_c, state_n, state_m],
        out_shape=[jax.ShapeDtypeStruct((t, W_A), _F32),
                   jax.ShapeDtypeStruct((n_seq, H_A, DH_A, DH_A), _F32),
                   jax.ShapeDtypeStruct((n_seq, H_A, DH_A), _F32),
                   jax.ShapeDtypeStruct((n_seq, 1, H_A), _F32)],
        compiler_params=_params(("parallel", "arbitrary")),
        name="mlstm",
    )(z, z, z, z, z, bias, c0, n0, m0)


def _band_kernel(*refs, max_dist, n_qtiles, kv_shared, with_sinks, with_lse):
    idx = 0
    if with_sinks:
        sink_ref = refs[0]
        idx = 1
    q_ref, k_ref, v_ref = refs[idx:idx + 3]
    idx += 3
    o_ref = refs[idx]
    idx += 1
    if with_lse:
        l_ref = refs[idx]
        idx += 1
    kprev_ref, vprev_ref = refs[idx:idx + 2]
    j = pl.program_id(2)
    n_kv = kprev_ref.shape[0]

    @pl.when(j == 0)
    def _():
        kprev_ref[...] = jnp.zeros_like(kprev_ref)
        vprev_ref[...] = jnp.zeros_like(vprev_ref)

    lane = lax.broadcasted_iota(jnp.int32, (BLK, LANES), 1)
    low = lane < DH
    if kv_shared:
        k_in, v_in = k_ref[0, 0], v_ref[0, 0]
        low_kv = lax.broadcasted_iota(jnp.int32, k_in.shape, 1) < DH
        k_sw = pltpu.roll(k_in, DH, axis=1)
        v_sw = pltpu.roll(v_in, DH, axis=1)
        k_tiles = [jnp.where(low_kv, k_in, k_sw), jnp.where(low_kv, k_sw, k_in)]
        v_tiles = [jnp.where(low_kv, v_in, v_sw), jnp.where(low_kv, v_sw, v_in)]
    else:
        k_tiles = [k_ref[0, 0, :, t * LANES:(t + 1) * LANES] for t in range(n_kv)]
        v_tiles = [v_ref[0, 0, :, t * LANES:(t + 1) * LANES] for t in range(n_kv)]
    k_tiles = [t.astype(_BF16) for t in k_tiles]
    v_tiles = [t.astype(_BF16) for t in v_tiles]

    qi = lax.broadcasted_iota(jnp.int32, (BLK, 2 * BLK), 0) + BLK
    ki = lax.broadcasted_iota(jnp.int32, (BLK, 2 * BLK), 1)
    dist = qi - ki
    band = (dist >= 0) & (dist <= max_dist)
    first_key = jnp.where(j > 0, 0, BLK)
    n_sub = q_ref.shape[2] // BLK

    tiles_per_kv = n_qtiles // n_kv
    for sb in range(n_sub):
        r0 = sb * BLK
        valid = band & (ki >= first_key) if sb == 0 else band
        q_all = q_ref[0, 0, r0:r0 + BLK, :] * (DH ** -0.5)
        lse_tile = jnp.zeros((BLK, LANES), _F32)
        for kt in range(n_kv):
            if sb == 0:
                k2 = jnp.concatenate([kprev_ref[kt], k_tiles[kt][0:BLK]], axis=0)
                v2 = jnp.concatenate([vprev_ref[kt], v_tiles[kt][0:BLK]], axis=0)
            else:
                k2 = k_tiles[kt][r0 - BLK:r0 + BLK]
                v2 = v_tiles[kt][r0 - BLK:r0 + BLK]
            parts = []
            for p in range(kt * tiles_per_kv, (kt + 1) * tiles_per_kv):
                q2 = q_all[:, p * LANES:(p + 1) * LANES]
                parts.append(jnp.where(low, q2, 0.0).astype(_BF16))
                parts.append(jnp.where(low, 0.0, q2).astype(_BF16))
            s_all = _nt_dot(jnp.concatenate(parts, axis=0), k2)
            probs, dens, lses = [], [], []
            for hh in range(2 * tiles_per_kv):
                s = jnp.where(valid, s_all[hh * BLK:(hh + 1) * BLK, :], NEG_INF)
                m = jnp.max(s, axis=1, keepdims=True)
                if with_sinks:
                    sk = sink_ref[2 * kt * tiles_per_kv + hh]
                    m = jnp.maximum(m, sk)
                p_exp = jnp.exp(s - m)
                den = jnp.sum(p_exp, axis=1, keepdims=True)
                if with_sinks:
                    den = den + jnp.exp(sk - m)
                probs.append(p_exp.astype(_BF16))
                dens.append(den)
                lses.append(m + jnp.log(den))
            o_all = _dot(jnp.concatenate(probs, axis=0), v2)
            for pp in range(tiles_per_kv):
                p = kt * tiles_per_kv + pp
                e, o = 2 * pp, 2 * pp + 1
                o_even = o_all[e * BLK:(e + 1) * BLK, :] / dens[e]
                o_odd = o_all[o * BLK:(o + 1) * BLK, :] / dens[o]
                o_ref[0, r0:r0 + BLK, p * LANES:(p + 1) * LANES] = jnp.where(low, o_even, o_odd)
                if with_lse:
                    lse_tile = jnp.where(lane == 2 * p, lses[e], lse_tile)
                    lse_tile = jnp.where(lane == 2 * p + 1, lses[o], lse_tile)
        if with_lse:
            l_ref[0, r0:r0 + BLK, :] = lse_tile

    last = (n_sub - 1) * BLK
    for kt in range(n_kv):
        kprev_ref[kt] = k_tiles[kt][last:last + BLK]
        vprev_ref[kt] = v_tiles[kt][last:last + BLK]


def _band_attention(z4, *, qcol, kcol, vcol, qw, kw, max_dist, sinks=None, with_lse):
    n_seq, rate, length, _ = z4.shape
    seq_len = length * rate
    nblk = length // BLK
    n_sub = max(s for s in (4, 2, 1) if nblk % s == 0)
    rows = n_sub * BLK
    kv_shared = kw == LANES and qw > LANES
    n_kv = 2 if kv_shared else kw // LANES

    def spec(width, coff):
        assert coff % width == 0
        return pl.BlockSpec((1, 1, rows, width), lambda b, r, j: (b, r, j, coff // width))

    out_spec = pl.BlockSpec((1, rows, qw), lambda b, r, j: (b, j, r))
    out_shape = [jax.ShapeDtypeStruct((n_seq, length, rate * qw), _F32)]
    out_specs = [out_spec]
    if with_lse:
        out_shape.append(jax.ShapeDtypeStruct((n_seq, length, rate * LANES), _F32))
        out_specs.append(pl.BlockSpec((1, rows, LANES), lambda b, r, j: (b, j, r)))
    in_specs = [spec(qw, qcol), spec(kw, kcol), spec(kw, vcol)]
    args = [z4, z4, z4]
    if sinks is not None:
        in_specs = [pl.BlockSpec(memory_space=pltpu.SMEM)] + in_specs
        args = [sinks] + args
    res = pl.pallas_call(
        functools.partial(_band_kernel, max_dist=max_dist, n_qtiles=qw // LANES,
                          kv_shared=kv_shared, with_sinks=sinks is not None, with_lse=with_lse),
        grid=(n_seq, rate, nblk // n_sub),
        in_specs=in_specs, out_specs=out_specs, out_shape=out_shape,
        scratch_shapes=[pltpu.VMEM((n_kv, BLK, LANES), _BF16),
                        pltpu.VMEM((n_kv, BLK, LANES), _BF16)],
        compiler_params=_params(("parallel", "parallel", "arbitrary")),
        name="band_attn",
    )(*args)
    return [r.reshape(n_seq * seq_len, -1) for r in res]


def _mix_ab(ga_ref, o1_ref, o2_ref, o3_ref, l1_ref, l2_ref, l3_ref, w_ref):
    l1, l2, l3 = l1_ref[...], l2_ref[...], l3_ref[...]
    lmax = jnp.maximum(jnp.maximum(l1, l2), l3)
    e1, e2, e3 = jnp.exp(l1 - lmax), jnp.exp(l2 - lmax), jnp.exp(l3 - lmax)
    inv = 1.0 / (e1 + e2 + e3)
    wts = (e1 * inv, e2 * inv, e3 * inv)
    outs = (o1_ref, o2_ref, o3_ref)
    low = lax.broadcasted_iota(jnp.int32, (l1.shape[0], LANES), 1) < DH
    tiles = []
    for p in range(H_B // 2):
        c = p * LANES
        ob = sum(jnp.where(low, wg[:, 2 * p:2 * p + 1], wg[:, 2 * p + 1:2 * p + 2]) * og[:, c:c + LANES]
                 for wg, og in zip(wts, outs))
        tiles.append(ob)
    return (_dot(ga_ref[...].astype(_BF16), w_ref[0:W_A, :])
            + _dot(jnp.concatenate(tiles, axis=1).astype(_BF16), w_ref[W_A:W_A + W_B, :]))


def _post_kernel(*refs, n_parts, ff_chunk):
    parts = refs[:n_parts]
    x_ref, w_ref, g1_ref, b1_ref, wu_ref, wd_ref, g2_ref, b2_ref, y_ref = refs[n_parts:]
    if n_parts == 1:
        mix = _dot(parts[0][...].astype(_BF16), w_ref[...])
    else:
        mix = _mix_ab(*parts, w_ref)
    x1 = _layer_norm(ALPHA * x_ref[...] + mix, g1_ref[...], b1_ref[...])
    xb = x1.astype(_BF16)
    acc = ALPHA * x1
    for c in range(D_FF // ff_chunk):
        hid = _dot(xb, wu_ref[:, c * ff_chunk:(c + 1) * ff_chunk])
        hid = jnp.square(jnp.maximum(hid, 0.0)).astype(_BF16)
        acc = acc + _dot(hid, wd_ref[c * ff_chunk:(c + 1) * ff_chunk, :])
    y_ref[...] = _layer_norm(acc, g2_ref[...], b2_ref[...])


def _const_spec(shape):
    return pl.BlockSpec(shape, lambda *_: (0,) * len(shape), pipeline_mode=pl.Buffered(1))


def _post_mixer(parts, x, w_out, ln1, wu, wd, ln2, *, tm, ff_chunk):
    plan = _post_plan(parts, x, w_out, ln1, wu, wd, ln2, tm=tm, ff_chunk=ff_chunk)
    in_specs, out_specs = plan["specs"](lambda i: (i,))
    return pl.pallas_call(
        plan["kernel"], grid=plan["grid"], in_specs=in_specs, out_specs=out_specs,
        out_shape=plan["out_shape"], compiler_params=_params(("parallel",)),
        name="post_mixer",
    )(*plan["args"])[0]


def _post_plan(parts, x, w_out, ln1, wu, wd, ln2, *, tm, ff_chunk):
    t = x.shape[0]
    tm = min(tm, t)

    def specs(dec):
        row = lambda width: pl.BlockSpec((tm, width), lambda *idx: (dec(*idx)[0], 0))
        vec = _const_spec((1, D_MODEL))
        in_specs = ([row(p.shape[1]) for p in parts]
                    + [row(D_MODEL), _const_spec(w_out.shape), vec, vec,
                       _const_spec(wu.shape), _const_spec(wd.shape), vec, vec])
        return in_specs, [row(D_MODEL)]

    return dict(kernel=functools.partial(_post_kernel, n_parts=len(parts), ff_chunk=ff_chunk),
                grid=(t // tm,), specs=specs, out_shape=[jax.ShapeDtypeStruct((t, D_MODEL), _F32)],
                args=[*parts, x, w_out, *ln1, wu, wd, *ln2])


def _roll_window(old, new, t_len):
    win = old.shape[1]
    rolled = pltpu.roll(old, win - t_len, axis=1)
    new_tail = jnp.concatenate([jnp.zeros((LANES - t_len, LANES), _F32), new], axis=0).T
    tail_lane = lax.broadcasted_iota(jnp.int32, new_tail.shape, 1) >= LANES - t_len
    tail = jnp.where(tail_lane, new_tail, rolled[:, win - LANES:])
    if win == LANES:
        return tail
    return jnp.concatenate([rolled[:, :win - LANES], tail], axis=1)


def _dil_cache_kernel(q_ref, k_ref, v_ref, cache_ref, o_ref, l_ref, out_ref, *, rate, t_len):
    win = cache_ref.shape[-1]
    rows = 2 * t_len
    key_i = lax.broadcasted_iota(jnp.int32, (rows, win), 1)
    row_t = lax.broadcasted_iota(jnp.int32, (rows, win), 0) % t_len
    ok_c = (key_i >= row_t) & (((key_i - row_t) & (rate - 1)) == 0)
    n_j = lax.broadcasted_iota(jnp.int32, (rows, t_len), 1)
    n_t = lax.broadcasted_iota(jnp.int32, (rows, t_len), 0) % t_len
    ok_n = (n_j <= n_t) & (((n_t - n_j) & (rate - 1)) == 0)
    low = lax.broadcasted_iota(jnp.int32, (t_len, LANES), 1) < DH
    _for_each(cache_ref.shape[0], functools.partial(
        _dil_cache_one, q_ref, k_ref, v_ref, cache_ref, o_ref, l_ref, out_ref,
        ok_c, ok_n, low, t_len))


def _for_each(n, body):
    if n == 1:
        body(0)
    else:
        lax.fori_loop(0, n, lambda i, carry: (body(i), carry)[1], 0, unroll=2 if n % 2 == 0 else 1)


def _dil_cache_one(q_ref, k_ref, v_ref, cache_ref, o_ref, l_ref, out_ref, ok_c, ok_n, low, t_len, bi):
    win = cache_ref.shape[-1]
    rows = 2 * t_len
    lane = lax.broadcasted_iota(jnp.int32, (t_len, LANES), 1)
    lse_tile = jnp.zeros((t_len, LANES), _F32)
    for a in range(H_B // 2):
        c = a * LANES
        q2 = q_ref[bi, :, c:c + LANES] * (DH ** -0.5)
        lhs = jnp.concatenate([jnp.where(low, q2, 0.0), jnp.where(low, 0.0, q2)], axis=0).astype(_BF16)
        k_old = cache_ref[bi, 0, 2 * a:2 * a + 2].reshape(2 * DH, win)
        v_old = cache_ref[bi, 1, 2 * a:2 * a + 2].reshape(2 * DH, win)
        k_new = k_ref[bi, :, c:c + LANES]
        v_new = v_ref[bi, :, c:c + LANES]
        s_c = jnp.where(ok_c, _dot(lhs, k_old.astype(_BF16)), NEG_INF)
        s_n = jnp.where(ok_n, _nt_dot(lhs, k_new.astype(_BF16)), NEG_INF)
        m = jnp.maximum(jnp.max(s_c, axis=1, keepdims=True), jnp.max(s_n, axis=1, keepdims=True))
        p_c = jnp.exp(s_c - m)
        p_n = jnp.exp(s_n - m)
        den = jnp.sum(p_c, axis=1, keepdims=True) + jnp.sum(p_n, axis=1, keepdims=True)
        o = (_nt_dot(p_c.astype(_BF16), v_old.astype(_BF16))
             + _dot(p_n.astype(_BF16), v_new.astype(_BF16))) / den
        lse = m + jnp.log(den)
        o_ref[bi, :, c:c + LANES] = jnp.where(low, o[0:t_len], o[t_len:rows])
        lse_tile = jnp.where(lane == 2 * a, lse[0:t_len], lse_tile)
        lse_tile = jnp.where(lane == 2 * a + 1, lse[t_len:rows], lse_tile)
        for kv, old, new in ((0, k_old, k_new), (1, v_old, v_new)):
            out_ref[bi, kv, 2 * a:2 * a + 2] = _roll_window(old, new, t_len).reshape(2, DH, win)
    l_ref[bi] = lse_tile


def _dil_cache_hosting_kernel(*refs, rate, t_len, hosted):
    n_in = 4 + sum(n for _, n, _ in hosted)
    _dil_cache_kernel(*refs[:4], *refs[n_in:n_in + 3], rate=rate, t_len=t_len)
    i_in, i_out = 4, n_in + 3
    for kern, n_ins, n_outs in hosted:
        kern(*refs[i_in:i_in + n_ins], *refs[i_out:i_out + n_outs])
        i_in += n_ins
        i_out += n_outs


def _dil_cache(z_s3, cache_t, *, g, rate, t_len, nb, hosted_plans=()):
    bsz, _, _, _, win = cache_t.shape
    steps = bsz // nb
    q_blk, k_blk, v_blk = 3 * g, 3 * g + 1, 3 * g + 2
    col = lambda cb: pl.BlockSpec((nb, t_len, W_B), lambda b: (b, 0, cb))
    tok = lambda width: pl.BlockSpec((nb, t_len, width), lambda b: (b, 0, 0))
    blk5 = lambda last: pl.BlockSpec((nb, 2, H_B, DH, last), lambda b: (b, 0, 0, 0, 0))
    in_specs = [col(q_blk), col(k_blk), col(v_blk), blk5(win)]
    out_specs = [tok(W_B), tok(LANES), blk5(win)]
    out_shape = [jax.ShapeDtypeStruct((bsz, t_len, W_B), _F32),
                 jax.ShapeDtypeStruct((bsz, t_len, LANES), _F32),
                 jax.ShapeDtypeStruct(cache_t.shape, _F32)]
    args = [z_s3, z_s3, z_s3, cache_t]
    hosted = []
    for plan in hosted_plans:
        grid = plan["grid"]
        assert functools.reduce(lambda a, b: a * b, grid) == steps

        def dec(i, grid=grid):
            idx = []
            for size in reversed(grid):
                idx.append(i % size)
                i = i // size
            return tuple(reversed(idx))

        p_in, p_out = plan["specs"](dec)
        in_specs += p_in
        out_specs += p_out
        out_shape += plan["out_shape"]
        args += plan["args"]
        hosted.append((plan["kernel"], len(p_in), len(p_out)))
    return pl.pallas_call(
        functools.partial(_dil_cache_hosting_kernel, rate=rate, t_len=t_len, hosted=tuple(hosted)),
        grid=(steps,), in_specs=in_specs, out_specs=out_specs, out_shape=out_shape,
        compiler_params=_params(("arbitrary",)),
        name="dil_cache",
    )(*args)


def _swa_sample_kernel(sink_ref, z_ref, cache_ref, o_ref, out_ref, *, t_len):
    grp = H_C // KV_C
    tiles = grp // 2
    rows = t_len * grp
    kw = KV_C * DH
    q_w = H_C * DH
    r_t = lax.broadcasted_iota(jnp.int32, (rows, WIN_C), 0) % t_len
    key_i = lax.broadcasted_iota(jnp.int32, (rows, WIN_C), 1)
    ok_c = key_i > r_t
    n_t = lax.broadcasted_iota(jnp.int32, (rows, t_len), 0) % t_len
    n_j = lax.broadcasted_iota(jnp.int32, (rows, t_len), 1)
    ok_n = n_j <= n_t
    r_h = lax.broadcasted_iota(jnp.int32, (rows, 1), 0) // t_len
    low_n = lax.broadcasted_iota(jnp.int32, (t_len, LANES), 1) < DH

    def dup(x, j):
        sw = pltpu.roll(x, DH, axis=1)
        return (jnp.where(low_n, x, sw) if j == 0 else jnp.where(low_n, sw, x)).astype(_BF16)

    sinks = []
    for j in range(KV_C):
        sk = jnp.zeros((rows, 1), _F32)
        for g in range(grp):
            sk = jnp.where(r_h == g, sink_ref[j * grp + g], sk)
        sinks.append(sk)

    def one(bi):
        k_n, v_n = z_ref[bi, :, q_w:q_w + kw], z_ref[bi, :, q_w + kw:q_w + 2 * kw]
        for kv, new in ((0, k_n), (1, v_n)):
            old = cache_ref[bi, kv].reshape(kw, WIN_C)
            out_ref[bi, kv] = _roll_window(old, new, t_len).reshape(KV_C, DH, WIN_C)
        for j in range(KV_C):
            k_old = cache_ref[bi, 0, j].astype(_BF16)
            v_old = cache_ref[bi, 1, j].astype(_BF16)
            k_dup = jnp.concatenate([k_old, k_old], axis=0)
            v_dup = jnp.concatenate([v_old, v_old], axis=0)
            parts = []
            for a in range(tiles):
                c = (j * tiles + a) * LANES
                q2 = z_ref[bi, :, c:c + LANES] * (DH ** -0.5)
                parts.append(jnp.where(low_n, q2, 0.0).astype(_BF16))
                parts.append(jnp.where(low_n, 0.0, q2).astype(_BF16))
            q = jnp.concatenate(parts, axis=0)
            sk = sinks[j]
            s_c = jnp.where(ok_c, _dot(q, k_dup), NEG_INF)
            s_n = jnp.where(ok_n, _nt_dot(q, dup(k_n, j)), NEG_INF)
            m = jnp.maximum(jnp.maximum(jnp.max(s_c, axis=1, keepdims=True),
                                        jnp.max(s_n, axis=1, keepdims=True)), sk)
            p_c = jnp.exp(s_c - m)
            p_n = jnp.exp(s_n - m)
            den = (jnp.sum(p_c, axis=1, keepdims=True) + jnp.sum(p_n, axis=1, keepdims=True)
                   + jnp.exp(sk - m))
            o = (_nt_dot(p_c.astype(_BF16), v_dup)
                 + _dot(p_n.astype(_BF16), dup(v_n, j))) / den
            for a in range(tiles):
                c = (j * tiles + a) * LANES
                even = o[(2 * a) * t_len:(2 * a + 1) * t_len, :]
                odd = o[(2 * a + 1) * t_len:(2 * a + 2) * t_len, :]
                o_ref[bi, :, c:c + LANES] = jnp.where(low_n, even, odd)

    _for_each(cache_ref.shape[0], one)


def _swa_cache(zc_s3, cache_t, sinks, *, t_len, nb):
    dbs = cache_t.shape[0]
    q_w = H_C * DH
    blk5 = lambda last: pl.BlockSpec((nb, 2, KV_C, DH, last), lambda b: (b, 0, 0, 0, 0))
    return pl.pallas_call(
        functools.partial(_swa_sample_kernel, t_len=t_len), grid=(dbs // nb,),
        in_specs=[pl.BlockSpec(memory_space=pltpu.SMEM),
                  pl.BlockSpec((nb, t_len, N_C), lambda b: (b, 0, 0)),
                  blk5(WIN_C)],
        out_specs=[pl.BlockSpec((nb, t_len, q_w), lambda b: (b, 0, 0)), blk5(WIN_C)],
        out_shape=[jax.ShapeDtypeStruct((dbs, t_len, q_w), _F32),
                   jax.ShapeDtypeStruct(cache_t.shape, _F32)],
        compiler_params=_params(("parallel",)),
        name="swa_cache",
    )(sinks, zc_s3, cache_t)


def _rope_chunks(widths_and_flags):
    out = []
    for width, flag in widths_and_flags:
        out += [flag] * (width // LANES)
    return tuple(out)


def kernel(x_prompt, x_sample, state_mlstm_C, state_mlstm_n, state_mlstm_m, cache_dil1_kv, cache_dil2_kv, cache_dil3_kv, cache_swa_kv, w_in_ab, b_gate_ab, w_out_ab, w_in_c, sinks_c, w_out_c, ln1_g, ln1_b, ln2_g, ln2_b, w_up, w_down):
    bsz, s_len, _ = x_prompt.shape
    dbs, t_len, _ = x_sample.shape
    tp = bsz * s_len
    ts = dbs * t_len
    xp = x_prompt.reshape(tp, D_MODEL)
    xs = x_sample.reshape(ts, D_MODEL)
    xp_bf = xp.astype(_BF16)
    pos_p = jnp.arange(s_len, dtype=jnp.int32)
    pos_s = PAST_LEN + jnp.arange(t_len, dtype=jnp.int32)
    tab_p = _rope_tables(pos_p)
    tab_s = tuple(jnp.tile(t, (ts // t_len, 1)) for t in _rope_tables(pos_s))
    row2 = lambda v: v.reshape(1, -1)

    w_in = w_in_ab[0]
    gate_lo = 4 * W_A
    q_lo = gate_lo + 2 * H_A
    k_lo = q_lo + N_DIL * W_B
    v_lo = k_lo + N_DIL * W_B
    w_a = jnp.concatenate([w_in[:, :gate_lo],
                           jnp.pad(w_in[:, gate_lo:q_lo], ((0, 0), (0, LANES - 2 * H_A)))],
                          axis=1).astype(_BF16)
    w_grp = [jnp.concatenate([w_in[:, lo + g * W_B:lo + (g + 1) * W_B] for lo in (q_lo, k_lo, v_lo)],
                             axis=1).astype(_BF16) for g in range(N_DIL)]
    rope_a = _rope_chunks([(gate_lo + LANES, False)])
    rope_grp = _rope_chunks([(2 * W_B, True), (W_B, False)])
    bias = jnp.pad(b_gate_ab[0], (0, LANES - 2 * H_A)).reshape(1, LANES)
    w_out0 = w_out_ab[0].astype(_BF16)

    za_s = _project(xs, w_a, tab_s, rope_a, n_seq=1, seq_len=ts, rate=1, tm=1024)
    zg_s = _project(xs, jnp.concatenate(w_grp, axis=1), tab_s, rope_grp * N_DIL,
                    n_seq=1, seq_len=ts, rate=1, tm=1024)

    ga_s, mc_s, mn_s, mm_s = _mlstm(za_s.reshape(ts, -1), bias, state_mlstm_C[0], state_mlstm_n[0],
                                    state_mlstm_m[0].reshape(dbs, 1, H_A),
                                    n_seq=dbs, seq_len=t_len, chunk=t_len, nb=8)

    dil_caches = (cache_dil1_kv, cache_dil2_kv, cache_dil3_kv)
    outs_p, lses_p, outs_s, lses_s, dil_kv_p, dil_kv_s = [], [], [], [], [], []
    to_time_minor = lambda c: jnp.transpose(c, (0, 2, 3, 4, 1))
    to_time_major = lambda c: jnp.transpose(c, (0, 4, 1, 2, 3))[None]
    z_s3 = zg_s.reshape(dbs, t_len, N_DIL * 3 * W_B)
    proj_kw = dict(n_seq=bsz, seq_len=s_len, tm=1024, out_dtype=_BF16, tail_lo=W_B)
    keeps = [min(DIL_WINDOWS[g], s_len) // DIL_RATES[g] for g in range(N_DIL)]
    za_p = _project(xp_bf, w_a, tab_p, rope_a, n_seq=bsz, seq_len=s_len, rate=1, tm=1024)
    zeros_c = jnp.zeros((bsz, H_A, DH_A, DH_A), _F32)
    zeros_n = jnp.zeros((bsz, H_A, DH_A), _F32)
    zeros_m = jnp.zeros((bsz, 1, H_A), _F32)
    ga_p, mc_p, mn_p, mm_p = _mlstm(za_p.reshape(tp, -1), bias, zeros_c, zeros_n, zeros_m,
                                    n_seq=bsz, seq_len=s_len, chunk=256, nb=1)
    for g in range(N_DIL):
        rate, win = DIL_RATES[g], DIL_WINDOWS[g]
        zg_p, tail = _project(xp_bf, w_grp[g], tab_p, rope_grp, rate=rate, tail_rows=keeps[g], **proj_kw)
        o, l = _band_attention(zg_p, qcol=0, kcol=W_B, vcol=2 * W_B, qw=W_B, kw=W_B,
                               max_dist=win // rate, with_lse=True)
        outs_p.append(o)
        lses_p.append(l)
        dil_kv_p.append(jnp.transpose(tail, (0, 2, 1, 3))
                        .reshape(1, bsz, keeps[g] * rate, 2, H_B, DH))

    post = functools.partial(_post_mixer, tm=512, ff_chunk=1024)
    wu0, wd0 = w_up[0].astype(_BF16), w_down[0].astype(_BF16)
    ln1_0, ln2_0 = (row2(ln1_g[0]), row2(ln1_b[0])), (row2(ln2_g[0]), row2(ln2_b[0]))

    xp = post([ga_p] + outs_p + lses_p, xp, w_out0, ln1_0, wu0, wd0, ln2_0)

    w_c = w_in_c[0].astype(_BF16)
    kc1 = H_C * DH
    vc1 = kc1 + KV_C * DH
    rope_c = _rope_chunks([(vc1, True), (KV_C * DH, False)])
    zc_p = _project(xp, w_c, tab_p, rope_c, n_seq=bsz, seq_len=s_len, rate=1, tm=1024)
    (o_p,) = _band_attention(zc_p, qcol=0, kcol=kc1, vcol=vc1, qw=H_C * DH, kw=KV_C * DH,
                             max_dist=WIN_C - 1, sinks=sinks_c[0], with_lse=False)
    keep = min(WIN_C, s_len)
    swa_kv_p = zc_p.reshape(bsz, s_len, N_C)[:, s_len - keep:, kc1:].reshape(1, bsz, keep, 2, KV_C, DH)
    w_out1 = w_out_c[0].astype(_BF16)
    wu1, wd1 = w_up[1].astype(_BF16), w_down[1].astype(_BF16)
    ln1_1, ln2_1 = (row2(ln1_g[1]), row2(ln1_b[1])), (row2(ln2_g[1]), row2(ln2_b[1]))

    nbs = [max(1, CACHE_STEP_BYTES // (DIL_WINDOWS[g] * 2 * W_B * 4)) for g in range(N_DIL)]
    host_g = N_DIL - 1
    can_host = tp // BLK == dbs // nbs[host_g]
    layer1 = ([o_p], xp, w_out1, ln1_1, wu1, wd1, ln2_1)
    for g in range(N_DIL):
        plans = [_post_plan(*layer1, tm=BLK, ff_chunk=1024)] if (g == host_g and can_host) else ()
        res = _dil_cache(z_s3, to_time_minor(dil_caches[g][0]), g=g, rate=DIL_RATES[g], t_len=t_len,
                         nb=nbs[g], hosted_plans=plans)
        o, l, rolled = res[:3]
        if plans:
            xp = res[3]
        outs_s.append(o.reshape(ts, W_B))
        lses_s.append(l.reshape(ts, LANES))
        dil_kv_s.append(to_time_major(rolled))
    if not can_host:
        xp = post(*layer1)

    xs = post([ga_s] + outs_s + lses_s, xs, w_out0, ln1_0, wu0, wd0, ln2_0)
    zc_s = _project(xs, w_c, tab_s, rope_c, n_seq=1, seq_len=ts, rate=1, tm=1024)
    o_s, swa_rolled = _swa_cache(zc_s.reshape(dbs, t_len, N_C), to_time_minor(cache_swa_kv[0]),
                                 sinks_c[0], t_len=t_len, nb=8)
    swa_kv_s = to_time_major(swa_rolled)
    xs = post([o_s.reshape(ts, H_C * DH)], xs, w_out1, ln1_1, wu1, wd1, ln2_1)

    return (xp.reshape(bsz, s_len, D_MODEL), xs.reshape(dbs, t_len, D_MODEL),
            mc_p[None], mc_s[None], mn_p[None], mn_s[None],
            mm_p.reshape(1, bsz, H_A), mm_s.reshape(1, dbs, H_A),
            dil_kv_p[0], dil_kv_s[0], dil_kv_p[1], dil_kv_s[1], dil_kv_p[2], dil_kv_s[2],
            swa_kv_p, swa_kv_s)
```

```python
import functools

import jax
import jax.numpy as jnp
from jax import lax
from jax.experimental import pallas as pl
from jax.experimental.pallas import tpu as pltpu

LANES = 128
SUBLANES = 8
MXU_COLS = 256
VMEM_LIMIT = 56 * 1024 * 1024
CACHE_STEP_BYTES = 4 * 1024 * 1024

D_MODEL = 1024
DH = 64
ROT_DIM = DH // 4
ROPE_THETA = 500000.0
PAST_LEN = 8192
BLK = 128
H_A = 4
DH_A = 128
W_A = H_A * DH_A
N_DIL = 3
DIL_WINDOWS = (128, 512, 2048)
DIL_RATES = (1, 4, 16)
H_B = 8
W_B = H_B * DH
H_C = 16
KV_C = 2
WIN_C = 128
D_FF = 4 * D_MODEL
DEPTH = 2
ALPHA = (2.0 * DEPTH) ** 0.25
LN_EPS = 1e-5
N_AB = 4 * W_A + 3 * N_DIL * W_B
N_C = (H_C + 2 * KV_C) * DH
NEG_INF = float("-inf")

_F32 = jnp.float32
_BF16 = jnp.bfloat16


def _params(sem):
    return pltpu.CompilerParams(dimension_semantics=sem, vmem_limit_bytes=VMEM_LIMIT)


def _nt_dot(a, b):
    return lax.dot_general(a, b, (((1,), (1,)), ((), ())), preferred_element_type=_F32)


def _tn_dot(a, b):
    return lax.dot_general(a, b, (((0,), (0,)), ((), ())), preferred_element_type=_F32)


def _dot(a, b):
    return jnp.dot(a, b, preferred_element_type=_F32)


def _log_sigmoid(x):
    return -(jnp.maximum(-x, 0.0) + jnp.log1p(jnp.exp(-jnp.abs(x))))


def _layer_norm(y, g, b):
    mu = jnp.mean(y, axis=-1, keepdims=True)
    yc = y - mu
    var = jnp.mean(yc * yc, axis=-1, keepdims=True)
    return yc * lax.rsqrt(var + LN_EPS) * g + b


def _rope_tables(pos):
    half = ROT_DIM // 2
    inv = ROPE_THETA ** (-jnp.arange(half, dtype=_F32) / half)
    ang = pos.astype(_F32)[:, None] * inv[None, :]
    cos = jnp.cos(ang)
    sin = jnp.sin(ang)
    n = pos.shape[0]
    ones = jnp.ones((n, DH - ROT_DIM), _F32)
    zeros = jnp.zeros((n, DH - ROT_DIM), _F32)
    zh = jnp.zeros((n, half), _F32)
    cos_h = jnp.concatenate([cos, cos, ones], axis=1)
    sinm_h = jnp.concatenate([-sin, zh, zeros], axis=1)
    sinp_h = jnp.concatenate([zh, sin, zeros], axis=1)
    tile2 = lambda t: jnp.concatenate([t, t], axis=1)
    return tile2(cos_h), tile2(sinm_h), tile2(sinp_h)


def _proj_kernel(x_ref, w_ref, cos_ref, sinm_ref, sinp_ref, o_ref, *tail_refs, rope_chunks, tail_lo):
    out = o_ref.at[0, 0]
    xb = x_ref[...].astype(_BF16)
    tm = xb.shape[0]
    n = w_ref.shape[1]
    half = ROT_DIM // 2
    for c0 in range(0, n, MXU_COLS):
        width = min(MXU_COLS, n - c0)
        acc = _dot(xb, w_ref[:, c0:c0 + width])
        for cc in range(width // LANES):
            lo = c0 + cc * LANES
            sub = acc[:, cc * LANES:(cc + 1) * LANES]
            if rope_chunks[lo // LANES]:
                sub = (sub * cos_ref[...]
                       + pltpu.roll(sub, LANES - half, axis=1) * sinm_ref[...]
                       + pltpu.roll(sub, half, axis=1) * sinp_ref[...])
            out[:, lo:lo + LANES] = sub.astype(o_ref.dtype)
            if tail_refs and lo >= tail_lo:
                tail = tail_refs[0]
                tail[0, 0, :, lo - tail_lo:lo - tail_lo + LANES] = sub[tm - tail.shape[2]:, :]


def _project(x, w, tables, rope_chunks, *, n_seq, seq_len, rate, tm, out_dtype=_F32,
             tail_rows=0, tail_lo=0):
    plan = _proj_plan(x, w, tables, rope_chunks, n_seq=n_seq, seq_len=seq_len, rate=rate, tm=tm,
                      out_dtype=out_dtype, tail_rows=tail_rows, tail_lo=tail_lo)
    in_specs, out_specs = plan["specs"](lambda b, r, j: (b, r, j))
    res = pl.pallas_call(
        plan["kernel"], grid=plan["grid"], in_specs=in_specs,
        out_specs=out_specs, out_shape=plan["out_shape"],
        compiler_params=_params(("parallel", "parallel", "arbitrary")),
        name="proj",
    )(*plan["args"])
    return res if tail_rows else res[0]


def _proj_plan(x, w, tables, rope_chunks, *, n_seq, seq_len, rate, tm, out_dtype, tail_rows, tail_lo):
    d = x.shape[1]
    n = w.shape[1]
    assert len(rope_chunks) * LANES == n
    length = seq_len // rate
    tm = min(tm, length)
    nj = length // tm
    xv = x.reshape(n_seq * length, rate * d)
    tabs = [t.reshape(length, rate * LANES) for t in tables]
    out_shape = [jax.ShapeDtypeStruct((n_seq, rate, length, n), out_dtype)]
    if tail_rows:
        out_shape.append(jax.ShapeDtypeStruct((n_seq, rate, tail_rows, n - tail_lo), _F32))

    def specs(dec):
        def at(fn):
            return lambda *idx: fn(*dec(*idx))
        row_tab = pl.BlockSpec((tm, LANES), at(lambda b, r, j: (j, r)))
        in_specs = [pl.BlockSpec((tm, d), at(lambda b, r, j: (b * nj + j, r))),
                    _const_spec(w.shape), row_tab, row_tab, row_tab]
        out_specs = [pl.BlockSpec((1, 1, tm, n), at(lambda b, r, j: (b, r, j, 0)))]
        if tail_rows:
            out_specs.append(pl.BlockSpec((1, 1, tail_rows, n - tail_lo),
                                          at(lambda b, r, j: (b, r, 0, 0))))
        return in_specs, out_specs

    return dict(kernel=functools.partial(_proj_kernel, rope_chunks=tuple(rope_chunks), tail_lo=tail_lo),
                grid=(n_seq, rate, nj), specs=specs, out_shape=out_shape, args=[xv, w, *tabs])


def _mlstm_kernel(q_ref, k_ref, v_ref, oa_ref, g_ref, bias_ref, c0_ref, n0_ref, m0_ref,
                  ga_ref, c_ref, n_ref, m_ref, *, chunk, nb):
    c_idx = pl.program_id(1)

    @pl.when(c_idx == 0)
    def _():
        c_ref[...] = c0_ref[...]
        n_ref[...] = n0_ref[...]
        m_ref[...] = m0_ref[...]

    row = lax.broadcasted_iota(jnp.int32, (chunk, chunk), 0)
    col = lax.broadcasted_iota(jnp.int32, (chunk, chunk), 1)
    eye = row == col
    tril = col <= row

    def to_row(colvec):
        return jnp.sum(jnp.where(eye, colvec, 0.0), axis=0, keepdims=True)

    stores = []
    for bi in range(nb):
        r0 = bi * chunk
        gates = g_ref[r0:r0 + chunk, :] + bias_ref[...]
        for h in range(H_A):
            lo = h * DH_A
            q = q_ref[r0:r0 + chunk, lo:lo + DH_A]
            k = k_ref[r0:r0 + chunk, lo:lo + DH_A] * (DH_A ** -0.5)
            v = v_ref[r0:r0 + chunk, lo:lo + DH_A]
            oa = oa_ref[r0:r0 + chunk, lo:lo + DH_A]
            c_prev = c_ref[bi, h]
            n_prev = n_ref[bi, h:h + 1, :]
            m_prev = m_ref[bi, :, h:h + 1]
            li_col = gates[:, h:h + 1]
            lf_col = _log_sigmoid(gates[:, H_A + h:H_A + h + 1])
            li_row = to_row(li_col)
            lf_row = to_row(lf_col)
            b_col = jnp.sum(jnp.where(tril, lf_row, 0.0), axis=1, keepdims=True)
            b_row = to_row(b_col)
            dmat = jnp.where(tril, b_col - b_row + li_row, NEG_INF)
            a_col = b_col + m_prev
            mt = jnp.maximum(a_col, jnp.max(dmat, axis=1, keepdims=True))
            wts = jnp.exp(dmat - mt)
            inter = jnp.exp(a_col - mt)
            qb = q.astype(_BF16)
            kb = k.astype(_BF16)
            vb = v.astype(_BF16)
            sqk = _nt_dot(qb, kb) * wts
            num = inter * _dot(qb, c_prev.astype(_BF16)) + _dot(sqk.astype(_BF16), vb)
            nq = (inter * jnp.sum(q * n_prev, axis=1, keepdims=True)
                  + jnp.sum(sqk, axis=1, keepdims=True))
            hid = num / jnp.maximum(jnp.abs(nq), jnp.exp(-mt))
            gated = hid * jax.nn.sigmoid(oa)
            b_last = b_col[chunk - 1:chunk, :]
            mt_last = mt[chunk - 1:chunk, :]
            w_last = jnp.exp(b_last - b_col + li_col - mt_last)
            inter_last = inter[chunk - 1:chunk, :]
            kw = k * w_last
            c_new = inter_last * c_prev + _tn_dot(kw.astype(_BF16), vb)
            n_new = inter_last * n_prev + jnp.sum(kw, axis=0, keepdims=True)
            stores.append((bi, h, gated, c_new, n_new, mt_last))
    for bi, h, gated, c_new, n_new, m_new in stores:
        ga_ref[bi * chunk:(bi + 1) * chunk, h * DH_A:(h + 1) * DH_A] = gated
        c_ref[bi, h] = c_new
        n_ref[bi, h:h + 1, :] = n_new
        m_ref[bi, :, h:h + 1] = m_new


def _mlstm(z, bias, c0, n0, m0, *, n_seq, seq_len, chunk, nb):
    t = z.shape[0]
    nc = seq_len // chunk
    rows = nb * chunk
    col = lambda cb: pl.BlockSpec((rows, W_A), lambda b, c: (b * nc + c, cb))
    state_c = pl.BlockSpec((nb, H_A, DH_A, DH_A), lambda b, c: (b, 0, 0, 0))
    state_n = pl.BlockSpec((nb, H_A, DH_A), lambda b, c: (b, 0, 0))
    state_m = pl.BlockSpec((nb, 1, H_A), lambda b, c: (b, 0, 0))
    return pl.pallas_call(
        functools.partial(_mlstm_kernel, chunk=chunk, nb=nb),
        grid=(n_seq // nb, nc),
        in_specs=[col(0), col(1), col(2), col(3),
                  pl.BlockSpec((rows, LANES), lambda b, c: (b * nc + c, 4 * W_A // LANES)),
                  pl.BlockSpec((1, LANES), lambda b, c: (0, 0)),
                  state_c, state_n, state_m],
        out_specs=[pl.BlockSpec((rows, W_A), lambda b, c: (b * nc + c, 0)),
                   state_c, state_n, state_m],
        out_shape=[jax.ShapeDtypeStruct((t, W_A), _F32),
                   jax.ShapeDtypeStruct((n_seq, H_A, DH_A, DH_A), _F32),
                   jax.ShapeDtypeStruct((n_seq, H_A, DH_A), _F32),
                   jax.ShapeDtypeStruct((n_seq, 1, H_A), _F32)],
        compiler_params=_params(("parallel", "arbitrary")),
        name="mlstm",
    )(z, z, z, z, z, bias, c0, n0, m0)


def _band_kernel(*refs, max_dist, n_qtiles, kv_shared, with_sinks, with_lse):
    idx = 0
    if with_sinks:
        sink_ref = refs[0]
        idx = 1
    q_ref, k_ref, v_ref = refs[idx:idx + 3]
    idx += 3
    o_ref = refs[idx]
    idx += 1
    if with_lse:
        l_ref = refs[idx]
        idx += 1
    kprev_ref, vprev_ref = refs[idx:idx + 2]
    j = pl.program_id(2)
    n_kv = kprev_ref.shape[0]

    @pl.when(j == 0)
    def _():
        kprev_ref[...] = jnp.zeros_like(kprev_ref)
        vprev_ref[...] = jnp.zeros_like(vprev_ref)

    lane = lax.broadcasted_iota(jnp.int32, (BLK, LANES), 1)
    low = lane < DH
    if kv_shared:
        k_in, v_in = k_ref[0, 0], v_ref[0, 0]
        low_kv = lax.broadcasted_iota(jnp.int32, k_in.shape, 1) < DH
        k_sw = pltpu.roll(k_in, DH, axis=1)
        v_sw = pltpu.roll(v_in, DH, axis=1)
        k_tiles = [jnp.where(low_kv, k_in, k_sw), jnp.where(low_kv, k_sw, k_in)]
        v_tiles = [jnp.where(low_kv, v_in, v_sw), jnp.where(low_kv, v_sw, v_in)]
    else:
        k_tiles = [k_ref[0, 0, :, t * LANES:(t + 1) * LANES] for t in range(n_kv)]
        v_tiles = [v_ref[0, 0, :, t * LANES:(t + 1) * LANES] for t in range(n_kv)]
    k_tiles = [t.astype(_BF16) for t in k_tiles]
    v_tiles = [t.astype(_BF16) for t in v_tiles]

    qi = lax.broadcasted_iota(jnp.int32, (BLK, 2 * BLK), 0) + BLK
    ki = lax.broadcasted_iota(jnp.int32, (BLK, 2 * BLK), 1)
    dist = qi - ki
    band = (dist >= 0) & (dist <= max_dist)
    first_key = jnp.where(j > 0, 0, BLK)
    n_sub = q_ref.shape[2] // BLK

    tiles_per_kv = n_qtiles // n_kv
    for sb in range(n_sub):
        r0 = sb * BLK
        valid = band & (ki >= first_key) if sb == 0 else band
        q_all = q_ref[0, 0, r0:r0 + BLK, :] * (DH ** -0.5)
        lse_tile = jnp.zeros((BLK, LANES), _F32)
        for kt in range(n_kv):
            if sb == 0:
                k2 = jnp.concatenate([kprev_ref[kt], k_tiles[kt][0:BLK]], axis=0)
                v2 = jnp.concatenate([vprev_ref[kt], v_tiles[kt][0:BLK]], axis=0)
            else:
                k2 = k_tiles[kt][r0 - BLK:r0 + BLK]
                v2 = v_tiles[kt][r0 - BLK:r0 + BLK]
            parts = []
            for p in range(kt * tiles_per_kv, (kt + 1) * tiles_per_kv):
                q2 = q_all[:, p * LANES:(p + 1) * LANES]
                parts.append(jnp.where(low, q2, 0.0).astype(_BF16))
                parts.append(jnp.where(low, 0.0, q2).astype(_BF16))
            s_all = _nt_dot(jnp.concatenate(parts, axis=0), k2)
            probs, dens, lses = [], [], []
            for hh in range(2 * tiles_per_kv):
                s = jnp.where(valid, s_all[hh * BLK:(hh + 1) * BLK, :], NEG_INF)
                m = jnp.max(s, axis=1, keepdims=True)
                if with_sinks:
                    sk = sink_ref[2 * kt * tiles_per_kv + hh]
                    m = jnp.maximum(m, sk)
                p_exp = jnp.exp(s - m)
                den = jnp.sum(p_exp, axis=1, keepdims=True)
                if with_sinks:
                    den = den + jnp.exp(sk - m)
                probs.append(p_exp.astype(_BF16))
                dens.append(den)
                lses.append(m + jnp.log(den))
            o_all = _dot(jnp.concatenate(probs, axis=0), v2)
            for pp in range(tiles_per_kv):
                p = kt * tiles_per_kv + pp
                e, o = 2 * pp, 2 * pp + 1
                o_even = o_all[e * BLK:(e + 1) * BLK, :] / dens[e]
                o_odd = o_all[o * BLK:(o + 1) * BLK, :] / dens[o]
                o_ref[0, r0:r0 + BLK, p * LANES:(p + 1) * LANES] = jnp.where(low, o_even, o_odd)
                if with_lse:
                    lse_tile = jnp.where(lane == 2 * p, lses[e], lse_tile)
                    lse_tile = jnp.where(lane == 2 * p + 1, lses[o], lse_tile)
        if with_lse:
            l_ref[0, r0:r0 + BLK, :] = lse_tile

    last = (n_sub - 1) * BLK
    for kt in range(n_kv):
        kprev_ref[kt] = k_tiles[kt][last:last + BLK]
        vprev_ref[kt] = v_tiles[kt][last:last + BLK]


def _band_attention(z4, *, qcol, kcol, vcol, qw, kw, max_dist, sinks=None, with_lse):
    n_seq, rate, length, _ = z4.shape
    seq_len = length * rate
    nblk = length // BLK
    n_sub = max(s for s in (4, 2, 1) if nblk % s == 0)
    rows = n_sub * BLK
    kv_shared = kw == LANES and qw > LANES
    n_kv = 2 if kv_shared else kw // LANES

    def spec(width, coff):
        assert coff % width == 0
        return pl.BlockSpec((1, 1, rows, width), lambda b, r, j: (b, r, j, coff // width))

    out_spec = pl.BlockSpec((1, rows, qw), lambda b, r, j: (b, j, r))
    out_shape = [jax.ShapeDtypeStruct((n_seq, length, rate * qw), _F32)]
    out_specs = [out_spec]
    if with_lse:
        out_shape.append(jax.ShapeDtypeStruct((n_seq, length, rate * LANES), _F32))
        out_specs.append(pl.BlockSpec((1, rows, LANES), lambda b, r, j: (b, j, r)))
    in_specs = [spec(qw, qcol), spec(kw, kcol), spec(kw, vcol)]
    args = [z4, z4, z4]
    if sinks is not None:
        in_specs = [pl.BlockSpec(memory_space=pltpu.SMEM)] + in_specs
        args = [sinks] + args
    res = pl.pallas_call(
        functools.partial(_band_kernel, max_dist=max_dist, n_qtiles=qw // LANES,
                          kv_shared=kv_shared, with_sinks=sinks is not None, with_lse=with_lse),
        grid=(n_seq, rate, nblk // n_sub),
        in_specs=in_specs, out_specs=out_specs, out_shape=out_shape,
        scratch_shapes=[pltpu.VMEM((n_kv, BLK, LANES), _BF16),
                        pltpu.VMEM((n_kv, BLK, LANES), _BF16)],
        compiler_params=_params(("parallel", "parallel", "arbitrary")),
        name="band_attn",
    )(*args)
    return [r.reshape(n_seq * seq_len, -1) for r in res]


def _mix_ab(ga_ref, o1_ref, o2_ref, o3_ref, l1_ref, l2_ref, l3_ref, w_ref):
    l1, l2, l3 = l1_ref[...], l2_ref[...], l3_ref[...]
    lmax = jnp.maximum(jnp.maximum(l1, l2), l3)
    e1, e2, e3 = jnp.exp(l1 - lmax), jnp.exp(l2 - lmax), jnp.exp(l3 - lmax)
    inv = 1.0 / (e1 + e2 + e3)
    head = lax.broadcasted_iota(jnp.int32, (LANES, W_B), 0)
    lane_head = lax.broadcasted_iota(jnp.int32, (LANES, W_B), 1) // DH
    spread = jnp.where(head == lane_head, 1.0, 0.0).astype(_BF16)
    ob = jnp.zeros((l1.shape[0], W_B), _F32)
    for e, o_ref in ((e1, o1_ref), (e2, o2_ref), (e3, o3_ref)):
        wg = e * inv
        hi = wg.astype(_BF16)
        lo = (wg - hi.astype(_F32)).astype(_BF16)
        ob = ob + (_dot(hi, spread) + _dot(lo, spread)) * o_ref[...]
    return (_dot(ga_ref[...].astype(_BF16), w_ref[0:W_A, :])
            + _dot(ob.astype(_BF16), w_ref[W_A:W_A + W_B, :]))


def _post_kernel(*refs, n_parts, ff_chunk):
    parts = refs[:n_parts]
    x_ref, w_ref, g1_ref, b1_ref, wu_ref, wd_ref, g2_ref, b2_ref, y_ref = refs[n_parts:]
    if n_parts == 1:
        mix = _dot(parts[0][...].astype(_BF16), w_ref[...])
    else:
        mix = _mix_ab(*parts, w_ref)
    x1 = _layer_norm(ALPHA * x_ref[...] + mix, g1_ref[...], b1_ref[...])
    xb = x1.astype(_BF16)
    acc = ALPHA * x1
    for c in range(D_FF // ff_chunk):
        hid = _dot(xb, wu_ref[:, c * ff_chunk:(c + 1) * ff_chunk])
        hid = jnp.square(jnp.maximum(hid, 0.0)).astype(_BF16)
        acc = acc + _dot(hid, wd_ref[c * ff_chunk:(c + 1) * ff_chunk, :])
    y_ref[...] = _layer_norm(acc, g2_ref[...], b2_ref[...])


def _const_spec(shape):
    return pl.BlockSpec(shape, lambda *_: (0,) * len(shape), pipeline_mode=pl.Buffered(1))


def _post_mixer(parts, x, w_out, ln1, wu, wd, ln2, *, tm, ff_chunk):
    plan = _post_plan(parts, x, w_out, ln1, wu, wd, ln2, tm=tm, ff_chunk=ff_chunk)
    in_specs, out_specs = plan["specs"](lambda i: (i,))
    return pl.pallas_call(
        plan["kernel"], grid=plan["grid"], in_specs=in_specs, out_specs=out_specs,
        out_shape=plan["out_shape"], compiler_params=_params(("parallel",)),
        name="post_mixer",
    )(*plan["args"])[0]


def _post_plan(parts, x, w_out, ln1, wu, wd, ln2, *, tm, ff_chunk):
    t = x.shape[0]
    tm = min(tm, t)

    def specs(dec):
        row = lambda width: pl.BlockSpec((tm, width), lambda *idx: (dec(*idx)[0], 0))
        vec = _const_spec((1, D_MODEL))
        in_specs = ([row(p.shape[1]) for p in parts]
                    + [row(D_MODEL), _const_spec(w_out.shape), vec, vec,
                       _const_spec(wu.shape), _const_spec(wd.shape), vec, vec])
        return in_specs, [row(D_MODEL)]

    return dict(kernel=functools.partial(_post_kernel, n_parts=len(parts), ff_chunk=ff_chunk),
                grid=(t // tm,), specs=specs, out_shape=[jax.ShapeDtypeStruct((t, D_MODEL), _F32)],
                args=[*parts, x, w_out, *ln1, wu, wd, *ln2])


def _roll_window(old, new, t_len):
    win = old.shape[1]
    rolled = pltpu.roll(old, win - t_len, axis=1)
    new_tail = jnp.concatenate([jnp.zeros((LANES - t_len, LANES), _F32), new], axis=0).T
    tail_lane = lax.broadcasted_iota(jnp.int32, new_tail.shape, 1) >= LANES - t_len
    tail = jnp.where(tail_lane, new_tail, rolled[:, win - LANES:])
    if win == LANES:
        return tail
    return jnp.concatenate([rolled[:, :win - LANES], tail], axis=1)


def _dil_cache_kernel(q_ref, k_ref, v_ref, cache_ref, o_ref, l_ref, out_ref, *, rate, t_len):
    win = cache_ref.shape[-1]
    rows = 2 * t_len
    key_i = lax.broadcasted_iota(jnp.int32, (rows, win), 1)
    row_t = lax.broadcasted_iota(jnp.int32, (rows, win), 0) % t_len
    ok_c = (key_i >= row_t) & (((key_i - row_t) & (rate - 1)) == 0)
    n_j = lax.broadcasted_iota(jnp.int32, (rows, t_len), 1)
    n_t = lax.broadcasted_iota(jnp.int32, (rows, t_len), 0) % t_len
    ok_n = (n_j <= n_t) & (((n_t - n_j) & (rate - 1)) == 0)
    low = lax.broadcasted_iota(jnp.int32, (t_len, LANES), 1) < DH
    _for_each(cache_ref.shape[0], functools.partial(
        _dil_cache_one, q_ref, k_ref, v_ref, cache_ref, o_ref, l_ref, out_ref,
        ok_c, ok_n, low, t_len))


def _for_each(n, body):
    if n == 1:
        body(0)
    else:
        lax.fori_loop(0, n, lambda i, carry: (body(i), carry)[1], 0, unroll=2 if n % 2 == 0 else 1)


def _dil_cache_one(q_ref, k_ref, v_ref, cache_ref, o_ref, l_ref, out_ref, ok_c, ok_n, low, t_len, bi):
    win = cache_ref.shape[-1]
    rows = 2 * t_len
    lane = lax.broadcasted_iota(jnp.int32, (t_len, LANES), 1)
    lse_tile = jnp.zeros((t_len, LANES), _F32)
    for a in range(H_B // 2):
        c = a * LANES
        q2 = q_ref[bi, :, c:c + LANES] * (DH ** -0.5)
        lhs = jnp.concatenate([jnp.where(low, q2, 0.0), jnp.where(low, 0.0, q2)], axis=0).astype(_BF16)
        k_old = cache_ref[bi, 0, 2 * a:2 * a + 2].reshape(2 * DH, win)
        v_old = cache_ref[bi, 1, 2 * a:2 * a + 2].reshape(2 * DH, win)
        k_new = k_ref[bi, :, c:c + LANES]
        v_new = v_ref[bi, :, c:c + LANES]
        s_c = jnp.where(ok_c, _dot(lhs, k_old.astype(_BF16)), NEG_INF)
        s_n = jnp.where(ok_n, _nt_dot(lhs, k_new.astype(_BF16)), NEG_INF)
        m = jnp.maximum(jnp.max(s_c, axis=1, keepdims=True), jnp.max(s_n, axis=1, keepdims=True))
        p_c = jnp.exp(s_c - m)
        p_n = jnp.exp(s_n - m)
        den = jnp.sum(p_c, axis=1, keepdims=True) + jnp.sum(p_n, axis=1, keepdims=True)
        o = (_nt_dot(p_c.astype(_BF16), v_old.astype(_BF16))
             + _dot(p_n.astype(_BF16), v_new.astype(_BF16))) / den
        lse = m + jnp.log(den)
        o_ref[bi, :, c:c + LANES] = jnp.where(low, o[0:t_len], o[t_len:rows])
        lse_tile = jnp.where(lane == 2 * a, lse[0:t_len], lse_tile)
        lse_tile = jnp.where(lane == 2 * a + 1, lse[t_len:rows], lse_tile)
        for kv, old, new in ((0, k_old, k_new), (1, v_old, v_new)):
            out_ref[bi, kv, 2 * a:2 * a + 2] = _roll_window(old, new, t_len).reshape(2, DH, win)
    l_ref[bi] = lse_tile


def _dil_cache_hosting_kernel(*refs, rate, t_len, hosted):
    n_in = 4 + sum(n for _, n, _ in hosted)
    _dil_cache_kernel(*refs[:4], *refs[n_in:n_in + 3], rate=rate, t_len=t_len)
    i_in, i_out = 4, n_in + 3
    for kern, n_ins, n_outs in hosted:
        kern(*refs[i_in:i_in + n_ins], *refs[i_out:i_out + n_outs])
        i_in += n_ins
        i_out += n_outs


def _dil_cache(z_s3, cache_t, *, g, rate, t_len, nb, hosted_plans=()):
    bsz, _, _, _, win = cache_t.shape
    steps = bsz // nb
    q_blk, k_blk, v_blk = 3 * g, 3 * g + 1, 3 * g + 2
    col = lambda cb: pl.BlockSpec((nb, t_len, W_B), lambda b: (b, 0, cb))
    tok = lambda width: pl.BlockSpec((nb, t_len, width), lambda b: (b, 0, 0))
    blk5 = lambda last: pl.BlockSpec((nb, 2, H_B, DH, last), lambda b: (b, 0, 0, 0, 0))
    in_specs = [col(q_blk), col(k_blk), col(v_blk), blk5(win)]
    out_specs = [tok(W_B), tok(LANES), blk5(win)]
    out_shape = [jax.ShapeDtypeStruct((bsz, t_len, W_B), _F32),
                 jax.ShapeDtypeStruct((bsz, t_len, LANES), _F32),
                 jax.ShapeDtypeStruct(cache_t.shape, _F32)]
    args = [z_s3, z_s3, z_s3, cache_t]
    hosted = []
    for plan in hosted_plans:
        grid = plan["grid"]
        assert functools.reduce(lambda a, b: a * b, grid) == steps

        def dec(i, grid=grid):
            idx = []
            for size in reversed(grid):
                idx.append(i % size)
                i = i // size
            return tuple(reversed(idx))

        p_in, p_out = plan["specs"](dec)
        in_specs += p_in
        out_specs += p_out
        out_shape += plan["out_shape"]
        args += plan["args"]
        hosted.append((plan["kernel"], len(p_in), len(p_out)))
    return pl.pallas_call(
        functools.partial(_dil_cache_hosting_kernel, rate=rate, t_len=t_len, hosted=tuple(hosted)),
        grid=(steps,), in_specs=in_specs, out_specs=out_specs, out_shape=out_shape,
        compiler_params=_params(("arbitrary",)),
        name="dil_cache",
    )(*args)


def _swa_sample_kernel(sink_ref, z_ref, cache_ref, o_ref, out_ref, *, t_len):
    grp = H_C // KV_C
    tiles = grp // 2
    rows = t_len * grp
    kw = KV_C * DH
    q_w = H_C * DH
    r_t = lax.broadcasted_iota(jnp.int32, (rows, WIN_C), 0) % t_len
    key_i = lax.broadcasted_iota(jnp.int32, (rows, WIN_C), 1)
    ok_c = key_i > r_t
    n_t = lax.broadcasted_iota(jnp.int32, (rows, t_len), 0) % t_len
    n_j = lax.broadcasted_iota(jnp.int32, (rows, t_len), 1)
    ok_n = n_j <= n_t
    r_h = lax.broadcasted_iota(jnp.int32, (rows, 1), 0) // t_len
    low_n = lax.broadcasted_iota(jnp.int32, (t_len, LANES), 1) < DH

    def dup(x, j):
        sw = pltpu.roll(x, DH, axis=1)
        return (jnp.where(low_n, x, sw) if j == 0 else jnp.where(low_n, sw, x)).astype(_BF16)

    sinks = []
    for j in range(KV_C):
        sk = jnp.zeros((rows, 1), _F32)
        for g in range(grp):
            sk = jnp.where(r_h == g, sink_ref[j * grp + g], sk)
        sinks.append(sk)

    def one(bi):
        k_n, v_n = z_ref[bi, :, q_w:q_w + kw], z_ref[bi, :, q_w + kw:q_w + 2 * kw]
        for kv, new in ((0, k_n), (1, v_n)):
            old = cache_ref[bi, kv].reshape(kw, WIN_C)
            out_ref[bi, kv] = _roll_window(old, new, t_len).reshape(KV_C, DH, WIN_C)
        for j in range(KV_C):
            k_old = cache_ref[bi, 0, j].astype(_BF16)
            v_old = cache_ref[bi, 1, j].astype(_BF16)
            k_dup = jnp.concatenate([k_old, k_old], axis=0)
            v_dup = jnp.concatenate([v_old, v_old], axis=0)
            parts = []
            for a in range(tiles):
                c = (j * tiles + a) * LANES
                q2 = z_ref[bi, :, c:c + LANES] * (DH ** -0.5)
                parts.append(jnp.where(low_n, q2, 0.0).astype(_BF16))
                parts.append(jnp.where(low_n, 0.0, q2).astype(_BF16))
            q = jnp.concatenate(parts, axis=0)
            sk = sinks[j]
            s_c = jnp.where(ok_c, _dot(q, k_dup), NEG_INF)
            s_n = jnp.where(ok_n, _nt_dot(q, dup(k_n, j)), NEG_INF)
            m = jnp.maximum(jnp.maximum(jnp.max(s_c, axis=1, keepdims=True),
                                        jnp.max(s_n, axis=1, keepdims=True)), sk)
            p_c = jnp.exp(s_c - m)
            p_n = jnp.exp(s_n - m)
            den = (jnp.sum(p_c, axis=1, keepdims=True) + jnp.sum(p_n, axis=1, keepdims=True)
                   + jnp.exp(sk - m))
            o = (_nt_dot(p_c.astype(_BF16), v_dup)
                 + _dot(p_n.astype(_BF16), dup(v_n, j))) / den
            for a in range(tiles):
                c = (j * tiles + a) * LANES
                even = o[(2 * a) * t_len:(2 * a + 1) * t_len, :]
                odd = o[(2 * a + 1) * t_len:(2 * a + 2) * t_len, :]
                o_ref[bi, :, c:c + LANES] = jnp.where(low_n, even, odd)

    _for_each(cache_ref.shape[0], one)


def _swa_cache(zc_s3, cache_t, sinks, *, t_len, nb):
    dbs = cache_t.shape[0]
    q_w = H_C * DH
    blk5 = lambda last: pl.BlockSpec((nb, 2, KV_C, DH, last), lambda b: (b, 0, 0, 0, 0))
    return pl.pallas_call(
        functools.partial(_swa_sample_kernel, t_len=t_len), grid=(dbs // nb,),
        in_specs=[pl.BlockSpec(memory_space=pltpu.SMEM),
                  pl.BlockSpec((nb, t_len, N_C), lambda b: (b, 0, 0)),
                  blk5(WIN_C)],
        out_specs=[pl.BlockSpec((nb, t_len, q_w), lambda b: (b, 0, 0)), blk5(WIN_C)],
        out_shape=[jax.ShapeDtypeStruct((dbs, t_len, q_w), _F32),
                   jax.ShapeDtypeStruct(cache_t.shape, _F32)],
        compiler_params=_params(("parallel",)),
        name="swa_cache",
    )(sinks, zc_s3, cache_t)


def _rope_chunks(widths_and_flags):
    out = []
    for width, flag in widths_and_flags:
        out += [flag] * (width // LANES)
    return tuple(out)


def kernel(x_prompt, x_sample, state_mlstm_C, state_mlstm_n, state_mlstm_m, cache_dil1_kv, cache_dil2_kv, cache_dil3_kv, cache_swa_kv, w_in_ab, b_gate_ab, w_out_ab, w_in_c, sinks_c, w_out_c, ln1_g, ln1_b, ln2_g, ln2_b, w_up, w_down):
    bsz, s_len, _ = x_prompt.shape
    dbs, t_len, _ = x_sample.shape
    tp = bsz * s_len
    ts = dbs * t_len
    xp = x_prompt.reshape(tp, D_MODEL)
    xs = x_sample.reshape(ts, D_MODEL)
    xp_bf = xp.astype(_BF16)
    pos_p = jnp.arange(s_len, dtype=jnp.int32)
    pos_s = PAST_LEN + jnp.arange(t_len, dtype=jnp.int32)
    tab_p = _rope_tables(pos_p)
    tab_s = tuple(jnp.tile(t, (ts // t_len, 1)) for t in _rope_tables(pos_s))
    row2 = lambda v: v.reshape(1, -1)

    w_in = w_in_ab[0]
    gate_lo = 4 * W_A
    q_lo = gate_lo + 2 * H_A
    k_lo = q_lo + N_DIL * W_B
    v_lo = k_lo + N_DIL * W_B
    w_a = jnp.concatenate([w_in[:, :gate_lo],
                           jnp.pad(w_in[:, gate_lo:q_lo], ((0, 0), (0, LANES - 2 * H_A)))],
                          axis=1).astype(_BF16)
    w_grp = [jnp.concatenate([w_in[:, lo + g * W_B:lo + (g + 1) * W_B] for lo in (q_lo, k_lo, v_lo)],
                             axis=1).astype(_BF16) for g in range(N_DIL)]
    rope_a = _rope_chunks([(gate_lo + LANES, False)])
    rope_grp = _rope_chunks([(2 * W_B, True), (W_B, False)])
    bias = jnp.pad(b_gate_ab[0], (0, LANES - 2 * H_A)).reshape(1, LANES)
    w_out0 = w_out_ab[0].astype(_BF16)

    za_s = _project(xs, w_a, tab_s, rope_a, n_seq=1, seq_len=ts, rate=1, tm=1024)
    zg_s = _project(xs, jnp.concatenate(w_grp, axis=1), tab_s, rope_grp * N_DIL,
                    n_seq=1, seq_len=ts, rate=1, tm=1024)

    ga_s, mc_s, mn_s, mm_s = _mlstm(za_s.reshape(ts, -1), bias, state_mlstm_C[0], state_mlstm_n[0],
                                    state_mlstm_m[0].reshape(dbs, 1, H_A),
                                    n_seq=dbs, seq_len=t_len, chunk=t_len, nb=8)

    dil_caches = (cache_dil1_kv, cache_dil2_kv, cache_dil3_kv)
    outs_p, lses_p, outs_s, lses_s, dil_kv_p, dil_kv_s = [], [], [], [], [], []
    to_time_minor = lambda c: jnp.transpose(c, (0, 2, 3, 4, 1))
    to_time_major = lambda c: jnp.transpose(c, (0, 4, 1, 2, 3))[None]
    z_s3 = zg_s.reshape(dbs, t_len, N_DIL * 3 * W_B)
    proj_kw = dict(n_seq=bsz, seq_len=s_len, tm=1024, out_dtype=_BF16, tail_lo=W_B)
    keeps = [min(DIL_WINDOWS[g], s_len) // DIL_RATES[g] for g in range(N_DIL)]
    za_p = _project(xp, w_a, tab_p, rope_a, n_seq=bsz, seq_len=s_len, rate=1, tm=1024)
    zeros_c = jnp.zeros((bsz, H_A, DH_A, DH_A), _F32)
    zeros_n = jnp.zeros((bsz, H_A, DH_A), _F32)
    zeros_m = jnp.zeros((bsz, 1, H_A), _F32)
    ga_p, mc_p, mn_p, mm_p = _mlstm(za_p.reshape(tp, -1), bias, zeros_c, zeros_n, zeros_m,
                                    n_seq=bsz, seq_len=s_len, chunk=256, nb=1)
    for g in range(N_DIL):
        rate, win = DIL_RATES[g], DIL_WINDOWS[g]
        zg_p, tail = _project(xp if rate == 1 else xp_bf, w_grp[g], tab_p, rope_grp, rate=rate,
                              tail_rows=keeps[g], **proj_kw)
        o, l = _band_attention(zg_p, qcol=0, kcol=W_B, vcol=2 * W_B, qw=W_B, kw=W_B,
                               max_dist=win // rate, with_lse=True)
        outs_p.append(o)
        lses_p.append(l)
        dil_kv_p.append(jnp.transpose(tail, (0, 2, 1, 3))
                        .reshape(1, bsz, keeps[g] * rate, 2, H_B, DH))

    post = functools.partial(_post_mixer, tm=512, ff_chunk=1024)
    wu0, wd0 = w_up[0].astype(_BF16), w_down[0].astype(_BF16)
    ln1_0, ln2_0 = (row2(ln1_g[0]), row2(ln1_b[0])), (row2(ln2_g[0]), row2(ln2_b[0]))

    xp = post([ga_p] + outs_p + lses_p, xp, w_out0, ln1_0, wu0, wd0, ln2_0)

    w_c = w_in_c[0].astype(_BF16)
    kc1 = H_C * DH
    vc1 = kc1 + KV_C * DH
    rope_c = _rope_chunks([(vc1, True), (KV_C * DH, False)])
    zc_p = _project(xp, w_c, tab_p, rope_c, n_seq=bsz, seq_len=s_len, rate=1, tm=1024)
    (o_p,) = _band_attention(zc_p, qcol=0, kcol=kc1, vcol=vc1, qw=H_C * DH, kw=KV_C * DH,
                             max_dist=WIN_C - 1, sinks=sinks_c[0], with_lse=False)
    keep = min(WIN_C, s_len)
    swa_kv_p = zc_p.reshape(bsz, s_len, N_C)[:, s_len - keep:, kc1:].reshape(1, bsz, keep, 2, KV_C, DH)
    w_out1 = w_out_c[0].astype(_BF16)
    wu1, wd1 = w_up[1].astype(_BF16), w_down[1].astype(_BF16)
    ln1_1, ln2_1 = (row2(ln1_g[1]), row2(ln1_b[1])), (row2(ln2_g[1]), row2(ln2_b[1]))

    nbs = [max(1, CACHE_STEP_BYTES // (DIL_WINDOWS[g] * 2 * W_B * 4)) for g in range(N_DIL)]
    host_g = N_DIL - 1
    can_host = tp // BLK == dbs // nbs[host_g]
    layer1 = ([o_p], xp, w_out1, ln1_1, wu1, wd1, ln2_1)
    for g in range(N_DIL):
        plans = [_post_plan(*layer1, tm=BLK, ff_chunk=1024)] if (g == host_g and can_host) else ()
        res = _dil_cache(z_s3, to_time_minor(dil_caches[g][0]), g=g, rate=DIL_RATES[g], t_len=t_len,
                         nb=nbs[g], hosted_plans=plans)
        o, l, rolled = res[:3]
        if plans:
            xp = res[3]
        outs_s.append(o.reshape(ts, W_B))
        lses_s.append(l.reshape(ts, LANES))
        dil_kv_s.append(to_time_major(rolled))
    if not can_host:
        xp = post(*layer1)

    xs = post([ga_s] + outs_s + lses_s, xs, w_out0, ln1_0, wu0, wd0, ln2_0)
    zc_s = _project(xs, w_c, tab_s, rope_c, n_seq=1, seq_len=ts, rate=1, tm=1024)
    o_s, swa_rolled = _swa_cache(zc_s.reshape(dbs, t_len, N_C), to_time_minor(cache_swa_kv[0]),
                                 sinks_c[0], t_len=t_len, nb=8)
    swa_kv_s = to_time_major(swa_rolled)
    xs = post([o_s.reshape(ts, H_C * DH)], xs, w_out1, ln1_1, wu1, wd1, ln2_1)

    return (xp.reshape(bsz, s_len, D_MODEL), xs.reshape(dbs, t_len, D_MODEL),
            mc_p[None], mc_s[None], mn_p[None], mn_s[None],
            mm_p.reshape(1, bsz, H_A), mm_s.reshape(1, dbs, H_A),
            dil_kv_p[0], dil_kv_s[0], dil_kv_p[1], dil_kv_s[1], dil_kv_p[2], dil_kv_s[2],
            swa_kv_p, swa_kv_s)
```

```python
import functools

import jax
import jax.numpy as jnp
from jax import lax
from jax.experimental import pallas as pl
from jax.experimental.pallas import tpu as pltpu

LANES = 128
SUBLANES = 8
MXU_COLS = 256
VMEM_LIMIT = 56 * 1024 * 1024
CACHE_STEP_BYTES = 4 * 1024 * 1024
PROJ_ROWS = 1024
POST_ROWS = 512
FF_CHUNK = 1024
MLSTM_CHUNK = 512
SAMPLE_SEQS_PER_STEP = 8

D_MODEL = 1024
DH = 64
ROT_DIM = DH // 4
ROPE_THETA = 500000.0
PAST_LEN = 8192
BLK = 128
H_A = 4
DH_A = 128
W_A = H_A * DH_A
N_DIL = 3
DIL_WINDOWS = (128, 512, 2048)
DIL_RATES = (1, 4, 16)
H_B = 8
W_B = H_B * DH
H_C = 16
KV_C = 2
WIN_C = 128
D_FF = 4 * D_MODEL
DEPTH = 2
ALPHA = (2.0 * DEPTH) ** 0.25
LN_EPS = 1e-5
N_AB = 4 * W_A + 3 * N_DIL * W_B
N_C = (H_C + 2 * KV_C) * DH
NEG_INF = float("-inf")

_F32 = jnp.float32
_BF16 = jnp.bfloat16


def _params(sem):
    return pltpu.CompilerParams(dimension_semantics=sem, vmem_limit_bytes=VMEM_LIMIT)


def _nt_dot(a, b):
    return lax.dot_general(a, b, (((1,), (1,)), ((), ())), preferred_element_type=_F32)


def _tn_dot(a, b):
    return lax.dot_general(a, b, (((0,), (0,)), ((), ())), preferred_element_type=_F32)


def _dot(a, b):
    return jnp.dot(a, b, preferred_element_type=_F32)


def _log_sigmoid(x):
    return -(jnp.maximum(-x, 0.0) + jnp.log1p(jnp.exp(-jnp.abs(x))))


def _layer_norm(y, g, b):
    mu = jnp.mean(y, axis=-1, keepdims=True)
    yc = y - mu
    var = jnp.mean(yc * yc, axis=-1, keepdims=True)
    return yc * lax.rsqrt(var + LN_EPS) * g + b


def _rope_tables(pos):
    half = ROT_DIM // 2
    inv = ROPE_THETA ** (-jnp.arange(half, dtype=_F32) / half)
    ang = pos.astype(_F32)[:, None] * inv[None, :]
    cos = jnp.cos(ang)
    sin = jnp.sin(ang)
    n = pos.shape[0]
    ones = jnp.ones((n, DH - ROT_DIM), _F32)
    zeros = jnp.zeros((n, DH - ROT_DIM), _F32)
    zh = jnp.zeros((n, half), _F32)
    cos_h = jnp.concatenate([cos, cos, ones], axis=1)
    sinm_h = jnp.concatenate([-sin, zh, zeros], axis=1)
    sinp_h = jnp.concatenate([zh, sin, zeros], axis=1)
    tile2 = lambda t: jnp.concatenate([t, t], axis=1)
    return tile2(cos_h), tile2(sinm_h), tile2(sinp_h)


def _proj_kernel(x_ref, w_ref, cos_ref, sinm_ref, sinp_ref, o_ref, *tail_refs, rope_chunks, tail_lo):
    out = o_ref.at[0, 0]
    xb = x_ref[...].astype(_BF16)
    tm = xb.shape[0]
    n = w_ref.shape[1]
    half = ROT_DIM // 2
    for c0 in range(0, n, MXU_COLS):
        width = min(MXU_COLS, n - c0)
        acc = _dot(xb, w_ref[:, c0:c0 + width])
        for cc in range(width // LANES):
            lo = c0 + cc * LANES
            sub = acc[:, cc * LANES:(cc + 1) * LANES]
            if rope_chunks[lo // LANES]:
                sub = (sub * cos_ref[...]
                       + pltpu.roll(sub, LANES - half, axis=1) * sinm_ref[...]
                       + pltpu.roll(sub, half, axis=1) * sinp_ref[...])
            out[:, lo:lo + LANES] = sub.astype(o_ref.dtype)
            if tail_refs and lo >= tail_lo:
                tail = tail_refs[0]
                tail[0, 0, :, lo - tail_lo:lo - tail_lo + LANES] = sub[tm - tail.shape[2]:, :]


def _project(x, w, tables, rope_chunks, *, n_seq, seq_len, rate, tm, out_dtype=_F32,
             tail_rows=0, tail_lo=0):
    plan = _proj_plan(x, w, tables, rope_chunks, n_seq=n_seq, seq_len=seq_len, rate=rate, tm=tm,
                      out_dtype=out_dtype, tail_rows=tail_rows, tail_lo=tail_lo)
    in_specs, out_specs = plan["specs"](lambda b, r, j: (b, r, j))
    res = pl.pallas_call(
        plan["kernel"], grid=plan["grid"], in_specs=in_specs,
        out_specs=out_specs, out_shape=plan["out_shape"],
        compiler_params=_params(("parallel", "parallel", "arbitrary")),
        name="proj",
    )(*plan["args"])
    return res if tail_rows else res[0]


def _proj_plan(x, w, tables, rope_chunks, *, n_seq, seq_len, rate, tm, out_dtype, tail_rows, tail_lo):
    d = x.shape[1]
    n = w.shape[1]
    assert len(rope_chunks) * LANES == n
    length = seq_len // rate
    tm = min(tm, length)
    nj = length // tm
    xv = x.reshape(n_seq * length, rate * d)
    tabs = [t.reshape(length, rate * LANES) for t in tables]
    out_shape = [jax.ShapeDtypeStruct((n_seq, rate, length, n), out_dtype)]
    if tail_rows:
        out_shape.append(jax.ShapeDtypeStruct((n_seq, rate, tail_rows, n - tail_lo), _F32))

    def specs(dec):
        def at(fn):
            return lambda *idx: fn(*dec(*idx))
        row_tab = pl.BlockSpec((tm, LANES), at(lambda b, r, j: (j, r)))
        in_specs = [pl.BlockSpec((tm, d), at(lambda b, r, j: (b * nj + j, r))),
                    _const_spec(w.shape), row_tab, row_tab, row_tab]
        out_specs = [pl.BlockSpec((1, 1, tm, n), at(lambda b, r, j: (b, r, j, 0)))]
        if tail_rows:
            out_specs.append(pl.BlockSpec((1, 1, tail_rows, n - tail_lo),
                                          at(lambda b, r, j: (b, r, 0, 0))))
        return in_specs, out_specs

    return dict(kernel=functools.partial(_proj_kernel, rope_chunks=tuple(rope_chunks), tail_lo=tail_lo),
                grid=(n_seq, rate, nj), specs=specs, out_shape=out_shape, args=[xv, w, *tabs])


def _mlstm_kernel(q_ref, k_ref, v_ref, oa_ref, g_ref, bias_ref, c0_ref, n0_ref, m0_ref,
                  ga_ref, c_ref, n_ref, m_ref, *, chunk, nb):
    c_idx = pl.program_id(1)

    @pl.when(c_idx == 0)
    def _():
        c_ref[...] = c0_ref[...]
        n_ref[...] = n0_ref[...]
        m_ref[...] = m0_ref[...]

    row = lax.broadcasted_iota(jnp.int32, (chunk, chunk), 0)
    col = lax.broadcasted_iota(jnp.int32, (chunk, chunk), 1)
    eye = row == col
    tril = col <= row

    def to_row(colvec):
        return jnp.sum(jnp.where(eye, colvec, 0.0), axis=0, keepdims=True)

    stores = []
    for bi in range(nb):
        r0 = bi * chunk
        gates = g_ref[r0:r0 + chunk, :] + bias_ref[...]
        for h in range(H_A):
            lo = h * DH_A
            q = q_ref[r0:r0 + chunk, lo:lo + DH_A]
            k = k_ref[r0:r0 + chunk, lo:lo + DH_A] * (DH_A ** -0.5)
            v = v_ref[r0:r0 + chunk, lo:lo + DH_A]
            oa = oa_ref[r0:r0 + chunk, lo:lo + DH_A]
            c_prev = c_ref[bi, h]
            n_prev = n_ref[bi, h:h + 1, :]
            m_prev = m_ref[bi, :, h:h + 1]
            li_col = gates[:, h:h + 1]
            lf_col = _log_sigmoid(gates[:, H_A + h:H_A + h + 1])
            li_row = to_row(li_col)
            lf_row = to_row(lf_col)
            b_col = jnp.sum(jnp.where(tril, lf_row, 0.0), axis=1, keepdims=True)
            b_row = to_row(b_col)
            dmat = jnp.where(tril, b_col - b_row + li_row, NEG_INF)
            a_col = b_col + m_prev
            mt = jnp.maximum(a_col, jnp.max(dmat, axis=1, keepdims=True))
            wts = jnp.exp(dmat - mt)
            inter = jnp.exp(a_col - mt)
            qb = q.astype(_BF16)
            kb = k.astype(_BF16)
            vb = v.astype(_BF16)
            sqk = _nt_dot(qb, kb) * wts
            num = inter * _dot(qb, c_prev.astype(_BF16)) + _dot(sqk.astype(_BF16), vb)
            nq = (inter * jnp.sum(q * n_prev, axis=1, keepdims=True)
                  + jnp.sum(sqk, axis=1, keepdims=True))
            hid = num / jnp.maximum(jnp.abs(nq), jnp.exp(-mt))
            gated = hid * jax.nn.sigmoid(oa)
            b_last = b_col[chunk - 1:chunk, :]
            mt_last = mt[chunk - 1:chunk, :]
            w_last = jnp.exp(b_last - b_col + li_col - mt_last)
            inter_last = inter[chunk - 1:chunk, :]
            kw = k * w_last
            c_new = inter_last * c_prev + _tn_dot(kw.astype(_BF16), vb)
            n_new = inter_last * n_prev + jnp.sum(kw, axis=0, keepdims=True)
            stores.append((bi, h, gated, c_new, n_new, mt_last))
    for bi, h, gated, c_new, n_new, m_new in stores:
        ga_ref[bi * chunk:(bi + 1) * chunk, h * DH_A:(h + 1) * DH_A] = gated
        c_ref[bi, h] = c_new
        n_ref[bi, h:h + 1, :] = n_new
        m_ref[bi, :, h:h + 1] = m_new


def _mlstm(z, bias, c0, n0, m0, *, n_seq, seq_len, chunk, nb):
    t = z.shape[0]
    nc = seq_len // chunk
    rows = nb * chunk
    col = lambda cb: pl.BlockSpec((rows, W_A), lambda b, c: (b * nc + c, cb))
    state_c = pl.BlockSpec((nb, H_A, DH_A, DH_A), lambda b, c: (b, 0, 0, 0))
    state_n = pl.BlockSpec((nb, H_A, DH_A), lambda b, c: (b, 0, 0))
    state_m = pl.BlockSpec((nb, 1, H_A), lambda b, c: (b, 0, 0))
    return pl.pallas_call(
        functools.partial(_mlstm_kernel, chunk=chunk, nb=nb),
        grid=(n_seq // nb, nc),
        in_specs=[col(0), col(1), col(2), col(3),
                  pl.BlockSpec((rows, LANES), lambda b, c: (b * nc + c, 4 * W_A // LANES)),
                  pl.BlockSpec((1, LANES), lambda b, c: (0, 0)),
                  state_c, state_n, state_m],
        out_specs=[pl.BlockSpec((rows, W_A), lambda b, c: (b * nc + c, 0)),
                   state_c, state_n, state_m],
        out_shape=[jax.ShapeDtypeStruct((t, W_A), _F32),
                   jax.ShapeDtypeStruct((n_seq, H_A, DH_A, DH_A), _F32),
                   jax.ShapeDtypeStruct((n_seq, H_A, DH_A), _F32),
                   jax.ShapeDtypeStruct((n_seq, 1, H_A), _F32)],
        compiler_params=_params(("parallel", "arbitrary")),
        name="mlstm",
    )(z, z, z, z, z, bias, c0, n0, m0)


def _band_kernel(*refs, max_dist, n_qtiles, kv_shared, with_sinks, with_lse):
    idx = 0
    if with_sinks:
        sink_ref = refs[0]
        idx = 1
    q_ref, k_ref, v_ref = refs[idx:idx + 3]
    idx += 3
    o_ref = refs[idx]
    idx += 1
    if with_lse:
        l_ref = refs[idx]
        idx += 1
    kprev_ref, vprev_ref = refs[idx:idx + 2]
    j = pl.program_id(2)
    n_kv = kprev_ref.shape[0]

    @pl.when(j == 0)
    def _():
        kprev_ref[...] = jnp.zeros_like(kprev_ref)
        vprev_ref[...] = jnp.zeros_like(vprev_ref)

    lane = lax.broadcasted_iota(jnp.int32, (BLK, LANES), 1)
    low = lane < DH
    if kv_shared:
        k_in, v_in = k_ref[0, 0], v_ref[0, 0]
        low_kv = lax.broadcasted_iota(jnp.int32, k_in.shape, 1) < DH
        k_sw = pltpu.roll(k_in, DH, axis=1)
        v_sw = pltpu.roll(v_in, DH, axis=1)
        k_tiles = [jnp.where(low_kv, k_in, k_sw), jnp.where(low_kv, k_sw, k_in)]
        v_tiles = [jnp.where(low_kv, v_in, v_sw), jnp.where(low_kv, v_sw, v_in)]
    else:
        k_tiles = [k_ref[0, 0, :, t * LANES:(t + 1) * LANES] for t in range(n_kv)]
        v_tiles = [v_ref[0, 0, :, t * LANES:(t + 1) * LANES] for t in range(n_kv)]
    k_tiles = [t.astype(_BF16) for t in k_tiles]
    v_tiles = [t.astype(_BF16) for t in v_tiles]

    qi = lax.broadcasted_iota(jnp.int32, (BLK, 2 * BLK), 0) + BLK
    ki = lax.broadcasted_iota(jnp.int32, (BLK, 2 * BLK), 1)
    dist = qi - ki
    band = (dist >= 0) & (dist <= max_dist)
    first_key = jnp.where(j > 0, 0, BLK)
    n_sub = q_ref.shape[2] // BLK

    tiles_per_kv = n_qtiles // n_kv
    for sb in range(n_sub):
        r0 = sb * BLK
        valid = band & (ki >= first_key) if sb == 0 else band
        q_all = q_ref[0, 0, r0:r0 + BLK, :] * (DH ** -0.5)
        lse_tile = jnp.zeros((BLK, LANES), _F32)
        for kt in range(n_kv):
            if sb == 0:
                k2 = jnp.concatenate([kprev_ref[kt], k_tiles[kt][0:BLK]], axis=0)
                v2 = jnp.concatenate([vprev_ref[kt], v_tiles[kt][0:BLK]], axis=0)
            else:
                k2 = k_tiles[kt][r0 - BLK:r0 + BLK]
                v2 = v_tiles[kt][r0 - BLK:r0 + BLK]
            parts = []
            for p in range(kt * tiles_per_kv, (kt + 1) * tiles_per_kv):
                q2 = q_all[:, p * LANES:(p + 1) * LANES]
                parts.append(jnp.where(low, q2, 0.0).astype(_BF16))
                parts.append(jnp.where(low, 0.0, q2).astype(_BF16))
            s_all = _nt_dot(jnp.concatenate(parts, axis=0), k2)
            probs, dens, lses = [], [], []
            for hh in range(2 * tiles_per_kv):
                s = jnp.where(valid, s_all[hh * BLK:(hh + 1) * BLK, :], NEG_INF)
                m = jnp.max(s, axis=1, keepdims=True)
                if with_sinks:
                    sk = sink_ref[2 * kt * tiles_per_kv + hh]
                    m = jnp.maximum(m, sk)
                p_exp = jnp.exp(s - m)
                den = jnp.sum(p_exp, axis=1, keepdims=True)
                if with_sinks:
                    den = den + jnp.exp(sk - m)
                probs.append(p_exp.astype(_BF16))
                dens.append(den)
                lses.append(m + jnp.log(den))
            o_all = _dot(jnp.concatenate(probs, axis=0), v2)
            for pp in range(tiles_per_kv):
                p = kt * tiles_per_kv + pp
                e, o = 2 * pp, 2 * pp + 1
                o_even = o_all[e * BLK:(e + 1) * BLK, :] / dens[e]
                o_odd = o_all[o * BLK:(o + 1) * BLK, :] / dens[o]
                o_ref[0, r0:r0 + BLK, p * LANES:(p + 1) * LANES] = jnp.where(low, o_even, o_odd)
                if with_lse:
                    lse_tile = jnp.where(lane == 2 * p, lses[e], lse_tile)
                    lse_tile = jnp.where(lane == 2 * p + 1, lses[o], lse_tile)
        if with_lse:
            l_ref[0, r0:r0 + BLK, :] = lse_tile

    last = (n_sub - 1) * BLK
    for kt in range(n_kv):
        kprev_ref[kt] = k_tiles[kt][last:last + BLK]
        vprev_ref[kt] = v_tiles[kt][last:last + BLK]


def _band_attention(z4, *, qcol, kcol, vcol, qw, kw, max_dist, sinks=None, with_lse):
    n_seq, rate, length, _ = z4.shape
    seq_len = length * rate
    nblk = length // BLK
    n_sub = max(s for s in (4, 2, 1) if nblk % s == 0)
    rows = n_sub * BLK
    kv_shared = kw == LANES and qw > LANES
    n_kv = 2 if kv_shared else kw // LANES

    def spec(width, coff):
        assert coff % width == 0
        return pl.BlockSpec((1, 1, rows, width), lambda b, r, j: (b, r, j, coff // width))

    out_spec = pl.BlockSpec((1, rows, qw), lambda b, r, j: (b, j, r))
    out_shape = [jax.ShapeDtypeStruct((n_seq, length, rate * qw), _F32)]
    out_specs = [out_spec]
    if with_lse:
        out_shape.append(jax.ShapeDtypeStruct((n_seq, length, rate * LANES), _F32))
        out_specs.append(pl.BlockSpec((1, rows, LANES), lambda b, r, j: (b, j, r)))
    in_specs = [spec(qw, qcol), spec(kw, kcol), spec(kw, vcol)]
    args = [z4, z4, z4]
    if sinks is not None:
        in_specs = [pl.BlockSpec(memory_space=pltpu.SMEM)] + in_specs
        args = [sinks] + args
    res = pl.pallas_call(
        functools.partial(_band_kernel, max_dist=max_dist, n_qtiles=qw // LANES,
                          kv_shared=kv_shared, with_sinks=sinks is not None, with_lse=with_lse),
        grid=(n_seq, rate, nblk // n_sub),
        in_specs=in_specs, out_specs=out_specs, out_shape=out_shape,
        scratch_shapes=[pltpu.VMEM((n_kv, BLK, LANES), _BF16),
                        pltpu.VMEM((n_kv, BLK, LANES), _BF16)],
        compiler_params=_params(("parallel", "parallel", "arbitrary")),
        name="band_attn",
    )(*args)
    return [r.reshape(n_seq * seq_len, -1) for r in res]


def _mix_ab(ga_ref, o1_ref, o2_ref, o3_ref, l1_ref, l2_ref, l3_ref, w_ref):
    l1, l2, l3 = l1_ref[...], l2_ref[...], l3_ref[...]
    lmax = jnp.maximum(jnp.maximum(l1, l2), l3)
    e1, e2, e3 = jnp.exp(l1 - lmax), jnp.exp(l2 - lmax), jnp.exp(l3 - lmax)
    inv = 1.0 / (e1 + e2 + e3)
    head = lax.broadcasted_iota(jnp.int32, (LANES, W_B), 0)
    lane_head = lax.broadcasted_iota(jnp.int32, (LANES, W_B), 1) // DH
    spread = jnp.where(head == lane_head, 1.0, 0.0).astype(_BF16)
    ob = jnp.zeros((l1.shape[0], W_B), _F32)
    for e, o_ref in ((e1, o1_ref), (e2, o2_ref), (e3, o3_ref)):
        wg = e * inv
        hi = wg.astype(_BF16)
        lo = (wg - hi.astype(_F32)).astype(_BF16)
        ob = ob + (_dot(hi, spread) + _dot(lo, spread)) * o_ref[...]
    return (_dot(ga_ref[...].astype(_BF16), w_ref[0:W_A, :])
            + _dot(ob.astype(_BF16), w_ref[W_A:W_A + W_B, :]))


def _post_kernel(*refs, n_parts, ff_chunk):
    parts = refs[:n_parts]
    x_ref, w_ref, g1_ref, b1_ref, wu_ref, wd_ref, g2_ref, b2_ref, y_ref = refs[n_parts:]
    if n_parts == 1:
        mix = _dot(parts[0][...].astype(_BF16), w_ref[...])
    else:
        mix = _mix_ab(*parts, w_ref)
    x1 = _layer_norm(ALPHA * x_ref[...] + mix, g1_ref[...], b1_ref[...])
    xb = x1.astype(_BF16)
    acc = ALPHA * x1
    for c in range(D_FF // ff_chunk):
        hid = _dot(xb, wu_ref[:, c * ff_chunk:(c + 1) * ff_chunk])
        hid = jnp.square(jnp.maximum(hid, 0.0)).astype(_BF16)
        acc = acc + _dot(hid, wd_ref[c * ff_chunk:(c + 1) * ff_chunk, :])
    y_ref[...] = _layer_norm(acc, g2_ref[...], b2_ref[...])


def _const_spec(shape):
    return pl.BlockSpec(shape, lambda *_: (0,) * len(shape), pipeline_mode=pl.Buffered(1))


def _post_mixer(parts, x, w_out, ln1, wu, wd, ln2, *, tm, ff_chunk):
    plan = _post_plan(parts, x, w_out, ln1, wu, wd, ln2, tm=tm, ff_chunk=ff_chunk)
    in_specs, out_specs = plan["specs"](lambda i: (i,))
    return pl.pallas_call(
        plan["kernel"], grid=plan["grid"], in_specs=in_specs, out_specs=out_specs,
        out_shape=plan["out_shape"], compiler_params=_params(("parallel",)),
        name="post_mixer",
    )(*plan["args"])[0]


def _post_plan(parts, x, w_out, ln1, wu, wd, ln2, *, tm, ff_chunk):
    t = x.shape[0]
    tm = min(tm, t)

    def specs(dec):
        row = lambda width: pl.BlockSpec((tm, width), lambda *idx: (dec(*idx)[0], 0))
        vec = _const_spec((1, D_MODEL))
        in_specs = ([row(p.shape[1]) for p in parts]
                    + [row(D_MODEL), _const_spec(w_out.shape), vec, vec,
                       _const_spec(wu.shape), _const_spec(wd.shape), vec, vec])
        return in_specs, [row(D_MODEL)]

    return dict(kernel=functools.partial(_post_kernel, n_parts=len(parts), ff_chunk=ff_chunk),
                grid=(t // tm,), specs=specs, out_shape=[jax.ShapeDtypeStruct((t, D_MODEL), _F32)],
                args=[*parts, x, w_out, *ln1, wu, wd, *ln2])


def _roll_window(old, new, t_len):
    win = old.shape[1]
    rolled = pltpu.roll(old, win - t_len, axis=1)
    new_tail = jnp.concatenate([jnp.zeros((LANES - t_len, LANES), _F32), new], axis=0).T
    tail_lane = lax.broadcasted_iota(jnp.int32, new_tail.shape, 1) >= LANES - t_len
    tail = jnp.where(tail_lane, new_tail, rolled[:, win - LANES:])
    if win == LANES:
        return tail
    return jnp.concatenate([rolled[:, :win - LANES], tail], axis=1)


def _dil_cache_kernel(q_ref, k_ref, v_ref, cache_ref, o_ref, l_ref, out_ref, *, rate, t_len):
    win = cache_ref.shape[-1]
    rows = 2 * t_len
    key_i = lax.broadcasted_iota(jnp.int32, (rows, win), 1)
    row_t = lax.broadcasted_iota(jnp.int32, (rows, win), 0) % t_len
    ok_c = (key_i >= row_t) & (((key_i - row_t) & (rate - 1)) == 0)
    n_j = lax.broadcasted_iota(jnp.int32, (rows, t_len), 1)
    n_t = lax.broadcasted_iota(jnp.int32, (rows, t_len), 0) % t_len
    ok_n = (n_j <= n_t) & (((n_t - n_j) & (rate - 1)) == 0)
    low = lax.broadcasted_iota(jnp.int32, (t_len, LANES), 1) < DH
    _for_each(cache_ref.shape[0], functools.partial(
        _dil_cache_one, q_ref, k_ref, v_ref, cache_ref, o_ref, l_ref, out_ref,
        ok_c, ok_n, low, t_len))


def _for_each(n, body):
    if n == 1:
        body(0)
    else:
        lax.fori_loop(0, n, lambda i, carry: (body(i), carry)[1], 0, unroll=2 if n % 2 == 0 else 1)


def _dil_cache_one(q_ref, k_ref, v_ref, cache_ref, o_ref, l_ref, out_ref, ok_c, ok_n, low, t_len, bi):
    win = cache_ref.shape[-1]
    rows = 2 * t_len
    lane = lax.broadcasted_iota(jnp.int32, (t_len, LANES), 1)
    lse_tile = jnp.zeros((t_len, LANES), _F32)
    for a in range(H_B // 2):
        c = a * LANES
        q2 = q_ref[bi, :, c:c + LANES] * (DH ** -0.5)
        lhs = jnp.concatenate([jnp.where(low, q2, 0.0), jnp.where(low, 0.0, q2)], axis=0).astype(_BF16)
        k_old = cache_ref[bi, 0, 2 * a:2 * a + 2].reshape(2 * DH, win)
        v_old = cache_ref[bi, 1, 2 * a:2 * a + 2].reshape(2 * DH, win)
        k_new = k_ref[bi, :, c:c + LANES]
        v_new = v_ref[bi, :, c:c + LANES]
        s_c = jnp.where(ok_c, _dot(lhs, k_old.astype(_BF16)), NEG_INF)
        s_n = jnp.where(ok_n, _nt_dot(lhs, k_new.astype(_BF16)), NEG_INF)
        m = jnp.maximum(jnp.max(s_c, axis=1, keepdims=True), jnp.max(s_n, axis=1, keepdims=True))
        p_c = jnp.exp(s_c - m)
        p_n = jnp.exp(s_n - m)
        den = jnp.sum(p_c, axis=1, keepdims=True) + jnp.sum(p_n, axis=1, keepdims=True)
        o = (_nt_dot(p_c.astype(_BF16), v_old.astype(_BF16))
             + _dot(p_n.astype(_BF16), v_new.astype(_BF16))) / den
        lse = m + jnp.log(den)
        o_ref[bi, :, c:c + LANES] = jnp.where(low, o[0:t_len], o[t_len:rows])
        lse_tile = jnp.where(lane == 2 * a, lse[0:t_len], lse_tile)
        lse_tile = jnp.where(lane == 2 * a + 1, lse[t_len:rows], lse_tile)
        for kv, old, new in ((0, k_old, k_new), (1, v_old, v_new)):
            out_ref[bi, kv, 2 * a:2 * a + 2] = _roll_window(old, new, t_len).reshape(2, DH, win)
    l_ref[bi] = lse_tile


def _dil_cache_hosting_kernel(*refs, rate, t_len, hosted):
    n_in = 4 + sum(n for _, n, _ in hosted)
    _dil_cache_kernel(*refs[:4], *refs[n_in:n_in + 3], rate=rate, t_len=t_len)
    i_in, i_out = 4, n_in + 3
    for kern, n_ins, n_outs in hosted:
        kern(*refs[i_in:i_in + n_ins], *refs[i_out:i_out + n_outs])
        i_in += n_ins
        i_out += n_outs


def _dil_cache(z_s3, cache_t, *, g, rate, t_len, nb, hosted_plans=()):
    bsz, _, _, _, win = cache_t.shape
    steps = bsz // nb
    q_blk, k_blk, v_blk = 3 * g, 3 * g + 1, 3 * g + 2
    col = lambda cb: pl.BlockSpec((nb, t_len, W_B), lambda b: (b, 0, cb))
    tok = lambda width: pl.BlockSpec((nb, t_len, width), lambda b: (b, 0, 0))
    blk5 = lambda last: pl.BlockSpec((nb, 2, H_B, DH, last), lambda b: (b, 0, 0, 0, 0))
    in_specs = [col(q_blk), col(k_blk), col(v_blk), blk5(win)]
    out_specs = [tok(W_B), tok(LANES), blk5(win)]
    out_shape = [jax.ShapeDtypeStruct((bsz, t_len, W_B), _F32),
                 jax.ShapeDtypeStruct((bsz, t_len, LANES), _F32),
                 jax.ShapeDtypeStruct(cache_t.shape, _F32)]
    args = [z_s3, z_s3, z_s3, cache_t]
    hosted = []
    for plan in hosted_plans:
        grid = plan["grid"]
        assert functools.reduce(lambda a, b: a * b, grid) == steps

        def dec(i, grid=grid):
            idx = []
            for size in reversed(grid):
                idx.append(i % size)
                i = i // size
            return tuple(reversed(idx))

        p_in, p_out = plan["specs"](dec)
        in_specs += p_in
        out_specs += p_out
        out_shape += plan["out_shape"]
        args += plan["args"]
        hosted.append((plan["kernel"], len(p_in), len(p_out)))
    return pl.pallas_call(
        functools.partial(_dil_cache_hosting_kernel, rate=rate, t_len=t_len, hosted=tuple(hosted)),
        grid=(steps,), in_specs=in_specs, out_specs=out_specs, out_shape=out_shape,
        compiler_params=_params(("arbitrary",)),
        name="dil_cache",
    )(*args)


def _swa_sample_kernel(sink_ref, z_ref, cache_ref, o_ref, out_ref, *, t_len):
    grp = H_C // KV_C
    tiles = grp // 2
    rows = t_len * grp
    kw = KV_C * DH
    q_w = H_C * DH
    r_t = lax.broadcasted_iota(jnp.int32, (rows, WIN_C), 0) % t_len
    key_i = lax.broadcasted_iota(jnp.int32, (rows, WIN_C), 1)
    ok_c = key_i > r_t
    n_t = lax.broadcasted_iota(jnp.int32, (rows, t_len), 0) % t_len
    n_j = lax.broadcasted_iota(jnp.int32, (rows, t_len), 1)
    ok_n = n_j <= n_t
    r_h = lax.broadcasted_iota(jnp.int32, (rows, 1), 0) // t_len
    low_n = lax.broadcasted_iota(jnp.int32, (t_len, LANES), 1) < DH

    def dup(x, j):
        sw = pltpu.roll(x, DH, axis=1)
        return (jnp.where(low_n, x, sw) if j == 0 else jnp.where(low_n, sw, x)).astype(_BF16)

    sinks = []
    for j in range(KV_C):
        sk = jnp.zeros((rows, 1), _F32)
        for g in range(grp):
            sk = jnp.where(r_h == g, sink_ref[j * grp + g], sk)
        sinks.append(sk)

    def one(bi):
        k_n, v_n = z_ref[bi, :, q_w:q_w + kw], z_ref[bi, :, q_w + kw:q_w + 2 * kw]
        for kv, new in ((0, k_n), (1, v_n)):
            old = cache_ref[bi, kv].reshape(kw, WIN_C)
            out_ref[bi, kv] = _roll_window(old, new, t_len).reshape(KV_C, DH, WIN_C)
        for j in range(KV_C):
            k_old = cache_ref[bi, 0, j].astype(_BF16)
            v_old = cache_ref[bi, 1, j].astype(_BF16)
            k_dup = jnp.concatenate([k_old, k_old], axis=0)
            v_dup = jnp.concatenate([v_old, v_old], axis=0)
            parts = []
            for a in range(tiles):
                c = (j * tiles + a) * LANES
                q2 = z_ref[bi, :, c:c + LANES] * (DH ** -0.5)
                parts.append(jnp.where(low_n, q2, 0.0).astype(_BF16))
                parts.append(jnp.where(low_n, 0.0, q2).astype(_BF16))
            q = jnp.concatenate(parts, axis=0)
            sk = sinks[j]
            s_c = jnp.where(ok_c, _dot(q, k_dup), NEG_INF)
            s_n = jnp.where(ok_n, _nt_dot(q, dup(k_n, j)), NEG_INF)
            m = jnp.maximum(jnp.maximum(jnp.max(s_c, axis=1, keepdims=True),
                                        jnp.max(s_n, axis=1, keepdims=True)), sk)
            p_c = jnp.exp(s_c - m)
            p_n = jnp.exp(s_n - m)
            den = (jnp.sum(p_c, axis=1, keepdims=True) + jnp.sum(p_n, axis=1, keepdims=True)
                   + jnp.exp(sk - m))
            o = (_nt_dot(p_c.astype(_BF16), v_dup)
                 + _dot(p_n.astype(_BF16), dup(v_n, j))) / den
            for a in range(tiles):
                c = (j * tiles + a) * LANES
                even = o[(2 * a) * t_len:(2 * a + 1) * t_len, :]
                odd = o[(2 * a + 1) * t_len:(2 * a + 2) * t_len, :]
                o_ref[bi, :, c:c + LANES] = jnp.where(low_n, even, odd)

    _for_each(cache_ref.shape[0], one)


def _swa_cache(zc_s3, cache_t, sinks, *, t_len, nb):
    dbs = cache_t.shape[0]
    q_w = H_C * DH
    blk5 = lambda last: pl.BlockSpec((nb, 2, KV_C, DH, last), lambda b: (b, 0, 0, 0, 0))
    return pl.pallas_call(
        functools.partial(_swa_sample_kernel, t_len=t_len), grid=(dbs // nb,),
        in_specs=[pl.BlockSpec(memory_space=pltpu.SMEM),
                  pl.BlockSpec((nb, t_len, N_C), lambda b: (b, 0, 0)),
                  blk5(WIN_C)],
        out_specs=[pl.BlockSpec((nb, t_len, q_w), lambda b: (b, 0, 0)), blk5(WIN_C)],
        out_shape=[jax.ShapeDtypeStruct((dbs, t_len, q_w), _F32),
                   jax.ShapeDtypeStruct(cache_t.shape, _F32)],
        compiler_params=_params(("parallel",)),
        name="swa_cache",
    )(sinks, zc_s3, cache_t)


def _rope_chunks(widths_and_flags):
    out = []
    for width, flag in widths_and_flags:
        out += [flag] * (width // LANES)
    return tuple(out)


def kernel(x_prompt, x_sample, state_mlstm_C, state_mlstm_n, state_mlstm_m, cache_dil1_kv, cache_dil2_kv, cache_dil3_kv, cache_swa_kv, w_in_ab, b_gate_ab, w_out_ab, w_in_c, sinks_c, w_out_c, ln1_g, ln1_b, ln2_g, ln2_b, w_up, w_down):
    bsz, s_len, _ = x_prompt.shape
    dbs, t_len, _ = x_sample.shape
    tp = bsz * s_len
    ts = dbs * t_len
    xp = x_prompt.reshape(tp, D_MODEL)
    xs = x_sample.reshape(ts, D_MODEL)
    xp_bf = xp.astype(_BF16)
    pos_p = jnp.arange(s_len, dtype=jnp.int32)
    pos_s = PAST_LEN + jnp.arange(t_len, dtype=jnp.int32)
    tab_p = _rope_tables(pos_p)
    tab_s = tuple(jnp.tile(t, (ts // t_len, 1)) for t in _rope_tables(pos_s))
    row2 = lambda v: v.reshape(1, -1)

    w_in = w_in_ab[0]
    gate_lo = 4 * W_A
    q_lo = gate_lo + 2 * H_A
    k_lo = q_lo + N_DIL * W_B
    v_lo = k_lo + N_DIL * W_B
    w_a = jnp.concatenate([w_in[:, :gate_lo],
                           jnp.pad(w_in[:, gate_lo:q_lo], ((0, 0), (0, LANES - 2 * H_A)))],
                          axis=1).astype(_BF16)
    w_grp = [jnp.concatenate([w_in[:, lo + g * W_B:lo + (g + 1) * W_B] for lo in (q_lo, k_lo, v_lo)],
                             axis=1).astype(_BF16) for g in range(N_DIL)]
    rope_a = _rope_chunks([(gate_lo + LANES, False)])
    rope_grp = _rope_chunks([(2 * W_B, True), (W_B, False)])
    bias = jnp.pad(b_gate_ab[0], (0, LANES - 2 * H_A)).reshape(1, LANES)
    w_out0 = w_out_ab[0].astype(_BF16)

    za_s = _project(xs, w_a, tab_s, rope_a, n_seq=1, seq_len=ts, rate=1, tm=PROJ_ROWS)
    zg_s = _project(xs, jnp.concatenate(w_grp, axis=1), tab_s, rope_grp * N_DIL,
                    n_seq=1, seq_len=ts, rate=1, tm=PROJ_ROWS)

    ga_s, mc_s, mn_s, mm_s = _mlstm(za_s.reshape(ts, -1), bias, state_mlstm_C[0], state_mlstm_n[0],
                                    state_mlstm_m[0].reshape(dbs, 1, H_A),
                                    n_seq=dbs, seq_len=t_len, chunk=t_len, nb=SAMPLE_SEQS_PER_STEP)

    dil_caches = (cache_dil1_kv, cache_dil2_kv, cache_dil3_kv)
    outs_p, lses_p, outs_s, lses_s, dil_kv_p, dil_kv_s = [], [], [], [], [], []
    to_time_minor = lambda c: jnp.transpose(c, (0, 2, 3, 4, 1))
    to_time_major = lambda c: jnp.transpose(c, (0, 4, 1, 2, 3))[None]
    z_s3 = zg_s.reshape(dbs, t_len, N_DIL * 3 * W_B)
    proj_kw = dict(n_seq=bsz, seq_len=s_len, tm=PROJ_ROWS, out_dtype=_BF16, tail_lo=W_B)
    keeps = [min(DIL_WINDOWS[g], s_len) // DIL_RATES[g] for g in range(N_DIL)]
    za_p = _project(xp, w_a, tab_p, rope_a, n_seq=bsz, seq_len=s_len, rate=1, tm=PROJ_ROWS)
    zeros_c = jnp.zeros((bsz, H_A, DH_A, DH_A), _F32)
    zeros_n = jnp.zeros((bsz, H_A, DH_A), _F32)
    zeros_m = jnp.zeros((bsz, 1, H_A), _F32)
    ga_p, mc_p, mn_p, mm_p = _mlstm(za_p.reshape(tp, -1), bias, zeros_c, zeros_n, zeros_m,
                                    n_seq=bsz, seq_len=s_len, chunk=MLSTM_CHUNK, nb=1)
    for g in range(N_DIL):
        rate, win = DIL_RATES[g], DIL_WINDOWS[g]
        zg_p, tail = _project(xp if rate == 1 else xp_bf, w_grp[g], tab_p, rope_grp, rate=rate,
                              tail_rows=keeps[g], **proj_kw)
        o, l = _band_attention(zg_p, qcol=0, kcol=W_B, vcol=2 * W_B, qw=W_B, kw=W_B,
                               max_dist=win // rate, with_lse=True)
        outs_p.append(o)
        lses_p.append(l)
        dil_kv_p.append(jnp.transpose(tail, (0, 2, 1, 3))
                        .reshape(1, bsz, keeps[g] * rate, 2, H_B, DH))

    post = functools.partial(_post_mixer, tm=POST_ROWS, ff_chunk=FF_CHUNK)
    wu0, wd0 = w_up[0].astype(_BF16), w_down[0].astype(_BF16)
    ln1_0, ln2_0 = (row2(ln1_g[0]), row2(ln1_b[0])), (row2(ln2_g[0]), row2(ln2_b[0]))

    xp = post([ga_p] + outs_p + lses_p, xp, w_out0, ln1_0, wu0, wd0, ln2_0)

    w_c = w_in_c[0].astype(_BF16)
    kc1 = H_C * DH
    vc1 = kc1 + KV_C * DH
    rope_c = _rope_chunks([(vc1, True), (KV_C * DH, False)])
    zc_p = _project(xp, w_c, tab_p, rope_c, n_seq=bsz, seq_len=s_len, rate=1, tm=PROJ_ROWS)
    (o_p,) = _band_attention(zc_p, qcol=0, kcol=kc1, vcol=vc1, qw=H_C * DH, kw=KV_C * DH,
                             max_dist=WIN_C - 1, sinks=sinks_c[0], with_lse=False)
    keep = min(WIN_C, s_len)
    swa_kv_p = zc_p.reshape(bsz, s_len, N_C)[:, s_len - keep:, kc1:].reshape(1, bsz, keep, 2, KV_C, DH)
    w_out1 = w_out_c[0].astype(_BF16)
    wu1, wd1 = w_up[1].astype(_BF16), w_down[1].astype(_BF16)
    ln1_1, ln2_1 = (row2(ln1_g[1]), row2(ln1_b[1])), (row2(ln2_g[1]), row2(ln2_b[1]))

    nbs = [max(1, CACHE_STEP_BYTES // (DIL_WINDOWS[g] * 2 * W_B * 4)) for g in range(N_DIL)]
    host_g = N_DIL - 1
    can_host = tp // BLK == dbs // nbs[host_g]
    layer1 = ([o_p], xp, w_out1, ln1_1, wu1, wd1, ln2_1)
    for g in range(N_DIL):
        plans = [_post_plan(*layer1, tm=BLK, ff_chunk=FF_CHUNK)] if (g == host_g and can_host) else ()
        res = _dil_cache(z_s3, to_time_minor(dil_caches[g][0]), g=g, rate=DIL_RATES[g], t_len=t_len,
                         nb=nbs[g], hosted_plans=plans)
        o, l, rolled = res[:3]
        if plans:
            xp = res[3]
        outs_s.append(o.reshape(ts, W_B))
        lses_s.append(l.reshape(ts, LANES))
        dil_kv_s.append(to_time_major(rolled))
    if not can_host:
        xp = post(*layer1)

    xs = post([ga_s] + outs_s + lses_s, xs, w_out0, ln1_0, wu0, wd0, ln2_0)
    zc_s = _project(xs, w_c, tab_s, rope_c, n_seq=1, seq_len=ts, rate=1, tm=PROJ_ROWS)
    o_s, swa_rolled = _swa_cache(zc_s.reshape(dbs, t_len, N_C), to_time_minor(cache_swa_kv[0]),
                                 sinks_c[0], t_len=t_len, nb=SAMPLE_SEQS_PER_STEP)
    swa_kv_s = to_time_major(swa_rolled)
    xs = post([o_s.reshape(ts, H_C * DH)], xs, w_out1, ln1_1, wu1, wd1, ln2_1)

    return (xp.reshape(bsz, s_len, D_MODEL), xs.reshape(dbs, t_len, D_MODEL),
            mc_p[None], mc_s[None], mn_p[None], mn_s[None],
            mm_p.reshape(1, bsz, H_A), mm_s.reshape(1, dbs, H_A),
            dil_kv_p[0], dil_kv_s[0], dil_kv_p[1], dil_kv_s[1], dil_kv_p[2], dil_kv_s[2],
            swa_kv_p, swa_kv_s)
```
